```python
import math
import jax
import jax.numpy as jnp
from jax import lax
import numpy as np

D_MODEL = 2048
BATCH = 4
SEQ = 4096
DEPTH = 2

GRID_W = 64
CTX_LEN = 256
BRANCH_W = D_MODEL // 2
N_BRANCH = 3
MLA_NOPE = 128
MLA_ROPE = 64
MLA_V = 128
MLA_HEADS = BRANCH_W // MLA_V
MLA_Q_LORA = 512
MLA_KV_LORA = 512
ROPE_THETA = 10000.0
Q_BLOCK = 128
RWKV_HEAD = 64
RWKV_HEADS = BRANCH_W // RWKV_HEAD
RWKV_W = BRANCH_W
RWKV_DECAY_LORA = 64
RWKV_A_LORA = 64
RWKV_GATE_LORA = 160
RWKV_CONV = 3
RWKV_GN_EPS = 64e-5
L2_EPS = 1e-12
S5_WIDTH = BRANCH_W
S5_GROUP = 16
S5_GROUPS = S5_WIDTH // S5_GROUP
S5_STATE = 64
S5_DT_MIN = 1e-3
S5_DT_MAX = 1e-1
D_FF = 4 * D_MODEL
NORM_EPS = 1e-6
IN_SPLITS = (MLA_Q_LORA, MLA_KV_LORA, MLA_ROPE,
             RWKV_W, RWKV_W, RWKV_W,
             RWKV_DECAY_LORA, RWKV_DECAY_LORA, RWKV_A_LORA, RWKV_A_LORA, RWKV_GATE_LORA,
             S5_WIDTH, N_BRANCH * D_MODEL)
N_IN = sum(IN_SPLITS)

kernel_name = 'hybrid_mla_rwkv7_s5_prefix_dit'


def rms_norm(x, g, eps=NORM_EPS):
    xf = x.astype(jnp.float32)
    y = xf * lax.rsqrt(jnp.mean(xf * xf, axis=-1, keepdims=True) + eps)
    return (y * g.astype(jnp.float32)).astype(x.dtype)


def modulate(h, shift, scale):
    return h * (1.0 + scale) + shift


def split_cols(z):
    out, start = [], 0
    for n in IN_SPLITS:
        out.append(z[..., start:start + n])
        start += n
    return out


def axial_rope_tables(rows, dtype):
    row = jnp.repeat(jnp.arange(rows, dtype=jnp.float32), GRID_W)
    col = jnp.tile(jnp.arange(GRID_W, dtype=jnp.float32), rows)
    n_freq = MLA_ROPE // 4
    inv = ROPE_THETA ** (-jnp.arange(n_freq, dtype=jnp.float32) / n_freq)
    ang = jnp.concatenate([row[:, None] * inv, col[:, None] * inv], axis=-1)
    return jnp.cos(ang).astype(dtype), jnp.sin(ang).astype(dtype)


def apply_rope(x, cos, sin):
    cos = cos[None, :, None, :]
    sin = sin[None, :, None, :]
    x1, x2 = jnp.split(x, 2, axis=-1)
    return jnp.concatenate([x1 * cos - x2 * sin, x1 * sin + x2 * cos], axis=-1)


def mla_queries(cq, q_lora_g, w_uq, qn_nope_g, qn_rope_g, rope):
    b, t, _ = cq.shape
    q = (rms_norm(cq, q_lora_g) @ w_uq).reshape(b, t, MLA_HEADS, MLA_NOPE + MLA_ROPE)
    q_nope = rms_norm(q[..., :MLA_NOPE], qn_nope_g)
    q_rope = rms_norm(q[..., MLA_NOPE:], qn_rope_g)
    if rope is not None:
        q_rope = apply_rope(q_rope, rope[0], rope[1])
    return jnp.concatenate([q_nope, q_rope], axis=-1)


def mla_keys_values(ckv, kr, kv_lora_g, w_ukv, kn_nope_g, kn_rope_g, rope):
    b, t, _ = ckv.shape
    kv = (rms_norm(ckv, kv_lora_g) @ w_ukv).reshape(b, t, MLA_HEADS, MLA_NOPE + MLA_V)
    k_nope = rms_norm(kv[..., :MLA_NOPE], kn_nope_g)
    k_rope = rms_norm(kr, kn_rope_g)[:, :, None, :]
    if rope is not None:
        k_rope = apply_rope(k_rope, rope[0], rope[1])
    k = jnp.concatenate([k_nope, jnp.broadcast_to(k_rope, (b, t, MLA_HEADS, MLA_ROPE))], axis=-1)
    return k, kv[..., MLA_NOPE:]


def softmax_attend(q, k, v):
    s = jnp.einsum('bqhd,bkhd->bhqk', q, k).astype(jnp.float32) * (1.0 / math.sqrt(q.shape[-1]))
    p = jax.nn.softmax(s, axis=-1).astype(v.dtype)
    return jnp.einsum('bhqk,bkhd->bqhd', p, v)


def blocked_attend(q, k, v):
    b, t, h, d = q.shape
    nb = t // Q_BLOCK
    qb = q.reshape(b, nb, Q_BLOCK, h, d).transpose(1, 0, 2, 3, 4)
    o = lax.map(lambda qi: softmax_attend(qi, k, v), qb)
    return o.transpose(1, 0, 2, 3, 4).reshape(b, t, h * v.shape[-1])


def centred_dwconv(x, w):
    return lax.conv_general_dilated(x, w[:, None, :].astype(x.dtype), window_strides=(1,), padding='SAME',
                                    dimension_numbers=('NWC', 'WIO', 'NWC'), feature_group_count=x.shape[-1])


def rwkv7_scan(r, decay, k, v, kk, a, s0, reverse, readout):
    f32 = jnp.float32
    xs = [jnp.swapaxes(z, 0, 1).astype(f32) for z in (r, decay, k, v, kk, a)]

    def step(S, inp):
        r_t, w_t, k_t, v_t, kk_t, a_t = inp
        sa = jnp.einsum('bhij,bhj->bhi', S, -kk_t)
        S = (S * w_t[:, :, None, :] + sa[..., None] * (kk_t * a_t)[:, :, None, :]
             + v_t[..., None] * k_t[:, :, None, :])
        y = jnp.einsum('bhij,bhj->bhi', S, r_t) if readout else None
        return S, y

    s, ys = lax.scan(step, s0, xs, reverse=reverse)
    return (jnp.swapaxes(ys, 0, 1) if readout else None), s


def rwkv_branch(r, k, v, wd, ad, gd, conv_w, w0, w2, a0, a2, g2, k_k, k_a, r_k, ln_g, ln_b, s0, readout):
    b, t, _ = r.shape
    heads = lambda z: z.reshape(b, t, RWKV_HEADS, RWKV_HEAD)
    r, k, v = jnp.split(centred_dwconv(jnp.concatenate([r, k, v], axis=-1), conv_w), 3, axis=-1)
    kk = heads(k * k_k).astype(jnp.float32)
    kk = kk * lax.rsqrt(jnp.sum(kk * kk, axis=-1, keepdims=True) + L2_EPS)
    ys, states, k_reps = [], [], []
    for d in range(2):
        w_log = -jax.nn.softplus(-(w0[d] + jnp.tanh(wd[d]) @ w2[d])) - 0.5
        decay = jnp.exp(-jnp.exp(w_log))
        a = jax.nn.sigmoid(a0[d] + ad[d] @ a2[d])
        k_rep = heads(k * (1.0 + (a - 1.0) * k_a))
        y, s = rwkv7_scan(heads(r), heads(decay), k_rep, heads(v), kk, heads(a), s0[d], d == 1, readout)
        ys.append(y)
        states.append(s)
        k_reps.append(k_rep)
    if not readout:
        return None, states
    y = ys[0] + ys[1]
    mu = jnp.mean(y, axis=-1, keepdims=True)
    var = jnp.mean(jnp.square(y - mu), axis=-1, keepdims=True)
    yn = ((y - mu) * lax.rsqrt(var + RWKV_GN_EPS)).reshape(b, t, RWKV_W) * ln_g + ln_b
    k_bonus = 0.5 * (k_reps[0] + k_reps[1])
    bonus = (jnp.sum(heads(r) * k_bonus * r_k, axis=-1, keepdims=True) * heads(v)).reshape(b, t, RWKV_W)
    out = (yn + bonus) * (jax.nn.sigmoid(gd) @ g2)
    return out.astype(r.dtype), states


def s5_discretise(lam_re, lam_im, log_dt, b_re, b_im):
    f32 = jnp.float32
    lam_re, lam_im, b_re, b_im = (z.astype(f32) for z in (lam_re, lam_im, b_re, b_im))
    dt = jnp.exp(log_dt.astype(f32))[:, None]
    mag = jnp.exp(lam_re * dt)
    a_re = mag * jnp.cos(lam_im * dt)
    a_im = mag * jnp.sin(lam_im * dt)
    den = lam_re * lam_re + lam_im * lam_im
    q_re = ((a_re - 1.0) * lam_re + a_im * lam_im) / den
    q_im = (a_im * lam_re - (a_re - 1.0) * lam_im) / den
    bb_re = q_re[..., None] * b_re - q_im[..., None] * b_im
    bb_im = q_re[..., None] * b_im + q_im[..., None] * b_re
    return a_re, a_im, bb_re, bb_im


def complex_linear_combine(e1, e2):
    a1r, a1i, b1r, b1i = e1
    a2r, a2i, b2r, b2i = e2
    return (a2r * a1r - a2i * a1i, a2r * a1i + a2i * a1r,
            a2r * b1r - a2i * b1i + b2r, a2r * b1i + a2i * b1r + b2i)


def s5_states(u, a_re, a_im, bb_re, bb_im, h0, reverse):
    bu_re = jnp.einsum('gpi,tbgi->tbgp', bb_re, u)
    bu_im = jnp.einsum('gpi,tbgi->tbgp', bb_im, u)
    if h0 is not None:
        h0_re, h0_im = h0
        first = -1 if reverse else 0
        bu_re = bu_re.at[first].add(a_re * h0_re - a_im * h0_im)
        bu_im = bu_im.at[first].add(a_re * h0_im + a_im * h0_re)
    t = u.shape[0]
    a_re_t = jnp.broadcast_to(a_re, (t, 1) + a_re.shape)
    a_im_t = jnp.broadcast_to(a_im, (t, 1) + a_im.shape)
    _, _, h_re, h_im = lax.associative_scan(complex_linear_combine, (a_re_t, a_im_t, bu_re, bu_im),
                                            reverse=reverse, axis=0)
    return h_re, h_im


def s5_branch(u, lam_re, lam_im, log_dt, b_re, b_im, c_re, c_im, d_skip, glu_w, glu_b, h0, readout):
    b, t, _ = u.shape
    f32 = jnp.float32
    ut = jnp.swapaxes(u, 0, 1).astype(f32).reshape(t, b, S5_GROUPS, S5_GROUP)
    ys, finals = [], []
    for d in range(2):
        rev = d == 1
        a_re, a_im, bb_re, bb_im = s5_discretise(lam_re[d], lam_im[d], log_dt[d], b_re[d], b_im[d])
        h_re, h_im = s5_states(ut, a_re, a_im, bb_re, bb_im, None if h0 is None else h0[d], rev)
        fin = 0 if rev else -1
        finals.append((h_re[fin], h_im[fin]))
        if readout:
            ys.append(jnp.einsum('gip,tbgp->tbgi', c_re[d].astype(f32), h_re)
                      - jnp.einsum('gip,tbgp->tbgi', c_im[d].astype(f32), h_im))
    if not readout:
        return None, finals
    y = jnp.swapaxes(ys[0] + ys[1], 0, 1).reshape(b, t, S5_WIDTH) + d_skip * u.astype(f32)
    z = jax.nn.gelu(y)
    out = z * jax.nn.sigmoid(z @ glu_w.astype(f32) + glu_b)
    return out.astype(u.dtype), finals


def gated_merge(gate_logits, branches, w_branch, w_out):
    b, t, _ = gate_logits.shape
    gates = jax.nn.sigmoid(gate_logits).reshape(b, t, N_BRANCH, D_MODEL)
    proj = jnp.einsum('btnc,ncd->btnd', jnp.stack(branches, axis=2), w_branch)
    return jnp.einsum('btnd,btnd->btd', gates, proj) @ w_out


def sq_relu_mlp(h, w1, w2):
    return jnp.square(jax.nn.relu(h @ w1)) @ w2


def setup_inputs(seed: int = 0) -> dict:
    key = jax.random.key(seed)
    ks = iter(jax.random.split(key, 48))
    f32 = jnp.float32
    L, D = DEPTH, D_MODEL

    def nrm(shape, scale):
        return scale * jax.random.normal(next(ks), shape, f32)

    def gain(shape):
        return 1.0 + nrm(shape, 0.02)

    n_frac = jnp.arange(RWKV_W, dtype=f32) / (RWKV_W - 1)
    depth_frac = jnp.arange(L, dtype=f32) / max(L - 1, 1)
    decay_speed = -7.0 + 5.0 * n_frac[None, :] ** (0.85 + depth_frac[:, None] ** 0.5)
    lam_im0 = math.pi * jnp.arange(S5_STATE, dtype=f32)
    s5_state_shape = (L, 2, S5_GROUPS, S5_STATE)
    return {
        'x': nrm((BATCH, SEQ, D), 1.0),
        'c': nrm((BATCH, D), 1.0),
        'ctx': nrm((BATCH, CTX_LEN, D), 1.0),
        'c_ctx': nrm((D,), 1.0),
        'ada_w': nrm((L, D, 6 * D), 0.5 * D ** -0.5),
        'ada_b': nrm((L, 6 * D), 0.02),
        'norm1_g': gain((L, D)),
        'norm2_g': gain((L, D)),
        'w_in': nrm((L, D, N_IN), D ** -0.5),
        'mla_q_lora_g': gain((L, MLA_Q_LORA)),
        'mla_kv_lora_g': gain((L, MLA_KV_LORA)),
        'mla_w_uq': nrm((L, MLA_Q_LORA, MLA_HEADS * (MLA_NOPE + MLA_ROPE)), MLA_Q_LORA ** -0.5),
        'mla_w_ukv': nrm((L, MLA_KV_LORA, MLA_HEADS * (MLA_NOPE + MLA_V)), MLA_KV_LORA ** -0.5),
        'mla_qn_nope_g': gain((L, MLA_NOPE)),
        'mla_qn_rope_g': gain((L, MLA_ROPE)),
        'mla_kn_nope_g': gain((L, MLA_NOPE)),
        'mla_kn_rope_g': gain((L, MLA_ROPE)),
        'rwkv_conv': 1.0 / RWKV_CONV + nrm((L, RWKV_CONV, 3 * RWKV_W), 0.1),
        'rwkv_w0': (decay_speed + 0.5)[:, None, :] + nrm((L, 2, RWKV_W), 0.05),
        'rwkv_w2': nrm((L, 2, RWKV_DECAY_LORA, RWKV_W), 0.1 * RWKV_DECAY_LORA ** -0.5),
        'rwkv_a0': nrm((L, 2, RWKV_W), 0.1),
        'rwkv_a2': nrm((L, 2, RWKV_A_LORA, RWKV_W), 0.5 * RWKV_A_LORA ** -0.5),
        'rwkv_g2': nrm((L, RWKV_GATE_LORA, RWKV_W), RWKV_GATE_LORA ** -0.5),
        'rwkv_k_k': 0.85 + nrm((L, RWKV_W), 0.02),
        'rwkv_k_a': 1.0 + nrm((L, RWKV_W), 0.02),
        'rwkv_r_k': nrm((L, RWKV_HEADS, RWKV_HEAD), 0.1),
        'rwkv_ln_g': gain((L, RWKV_W)),
        'rwkv_ln_b': nrm((L, RWKV_W), 0.02),
        's5_lam_re': -0.5 + nrm(s5_state_shape, 0.01),
        's5_lam_im': lam_im0 + nrm(s5_state_shape, 0.01),
        's5_log_dt': jax.random.uniform(next(ks), (L, 2, S5_GROUPS), f32,
                                        math.log(S5_DT_MIN), math.log(S5_DT_MAX)),
        's5_b_re': nrm((L, 2, S5_GROUPS, S5_STATE, S5_GROUP), (2 * S5_GROUP) ** -0.5),
        's5_b_im': nrm((L, 2, S5_GROUPS, S5_STATE, S5_GROUP), (2 * S5_GROUP) ** -0.5),
        's5_c_re': nrm((L, 2, S5_GROUPS, S5_GROUP, S5_STATE), (2 * S5_STATE) ** -0.5),
        's5_c_im': nrm((L, 2, S5_GROUPS, S5_GROUP, S5_STATE), (2 * S5_STATE) ** -0.5),
        's5_d': nrm((L, S5_WIDTH), 1.0),
        's5_glu_w': nrm((L, S5_WIDTH, S5_WIDTH), S5_WIDTH ** -0.5),
        's5_glu_b': nrm((L, S5_WIDTH), 0.02),
        'w_branch': nrm((L, N_BRANCH, BRANCH_W, D), BRANCH_W ** -0.5),
        'w_out': nrm((L, D, D), D ** -0.5),
        'w_mlp1': nrm((L, D, D_FF), D ** -0.5),
        'w_mlp2': nrm((L, D_FF, D), D_FF ** -0.5),
    }


def reference(x, c, ctx, c_ctx, ada_w, ada_b, norm1_g, norm2_g, w_in,
              mla_q_lora_g, mla_kv_lora_g, mla_w_uq, mla_w_ukv,
              mla_qn_nope_g, mla_qn_rope_g, mla_kn_nope_g, mla_kn_rope_g,
              rwkv_conv, rwkv_w0, rwkv_w2, rwkv_a0, rwkv_a2, rwkv_g2,
              rwkv_k_k, rwkv_k_a, rwkv_r_k, rwkv_ln_g, rwkv_ln_b,
              s5_lam_re, s5_lam_im, s5_log_dt, s5_b_re, s5_b_im, s5_c_re, s5_c_im,
              s5_d, s5_glu_w, s5_glu_b, w_branch, w_out, w_mlp1, w_mlp2):
    bsz, n_tok, _ = x.shape
    rows = n_tok // GRID_W
    rope = axial_rope_tables(rows, x.dtype)
    zero = jnp.zeros((bsz, RWKV_HEADS, RWKV_HEAD, RWKV_HEAD), jnp.float32)
    xc = ctx
    for l in range(DEPTH):
        last = l == DEPTH - 1
        mod_t = [m[:, None, :] for m in jnp.split(jax.nn.silu(c) @ ada_w[l] + ada_b[l], 6, axis=-1)]
        mod_c = jnp.split(jax.nn.silu(c_ctx) @ ada_w[l] + ada_b[l], 6, axis=-1)
        (cq_t, ckv_t, kr_t, r_t, k_t, v_t, wdf_t, wdb_t, adf_t, adb_t, gd_t, u_t, gate_t) = split_cols(
            modulate(rms_norm(x, norm1_g[l]), mod_t[0], mod_t[1]) @ w_in[l])
        (cq_c, ckv_c, kr_c, r_c, k_c, v_c, wdf_c, wdb_c, adf_c, adb_c, gd_c, u_c, gate_c) = split_cols(
            modulate(rms_norm(xc, norm1_g[l]), mod_c[0], mod_c[1]) @ w_in[l])

        key_c, val_c = mla_keys_values(ckv_c, kr_c, mla_kv_lora_g[l], mla_w_ukv[l],
                                       mla_kn_nope_g[l], mla_kn_rope_g[l], None)
        key_t, val_t = mla_keys_values(ckv_t, kr_t, mla_kv_lora_g[l], mla_w_ukv[l],
                                       mla_kn_nope_g[l], mla_kn_rope_g[l], rope)
        q_t = mla_queries(cq_t, mla_q_lora_g[l], mla_w_uq[l], mla_qn_nope_g[l], mla_qn_rope_g[l], rope)
        o_a_t = blocked_attend(q_t, jnp.concatenate([key_t, key_c], axis=1),
                               jnp.concatenate([val_t, val_c], axis=1))

        rwkv_p = (rwkv_conv[l], rwkv_w0[l], rwkv_w2[l], rwkv_a0[l], rwkv_a2[l], rwkv_g2[l],
                  rwkv_k_k[l], rwkv_k_a[l], rwkv_r_k[l], rwkv_ln_g[l], rwkv_ln_b[l])
        o_b_c, s_ctx = rwkv_branch(r_c, k_c, v_c, (wdf_c, wdb_c), (adf_c, adb_c), gd_c, *rwkv_p,
                                   (zero, zero), not last)
        o_b_t, _ = rwkv_branch(r_t, k_t, v_t, (wdf_t, wdb_t), (adf_t, adb_t), gd_t, *rwkv_p, s_ctx, True)

        s5_p = (s5_lam_re[l], s5_lam_im[l], s5_log_dt[l], s5_b_re[l], s5_b_im[l], s5_c_re[l], s5_c_im[l],
                s5_d[l], s5_glu_w[l], s5_glu_b[l])
        o_c_c, h_ctx = s5_branch(u_c, *s5_p, None, not last)
        o_c_t, _ = s5_branch(u_t, *s5_p, h_ctx, True)

        x = x + mod_t[2] * gated_merge(gate_t, (o_a_t, o_b_t, o_c_t), w_branch[l], w_out[l])
        x = x + mod_t[5] * sq_relu_mlp(modulate(rms_norm(x, norm2_g[l]), mod_t[3], mod_t[4]),
                                       w_mlp1[l], w_mlp2[l])

        if not last:
            q_c = mla_queries(cq_c, mla_q_lora_g[l], mla_w_uq[l], mla_qn_nope_g[l], mla_qn_rope_g[l], None)
            o_a_c = softmax_attend(q_c, key_c, val_c).reshape(bsz, xc.shape[1], BRANCH_W)
            xc = xc + mod_c[2] * gated_merge(gate_c, (o_a_c, o_b_c, o_c_c), w_branch[l], w_out[l])
            xc = xc + mod_c[5] * sq_relu_mlp(modulate(rms_norm(xc, norm2_g[l]), mod_c[3], mod_c[4]),
                                             w_mlp1[l], w_mlp2[l])
    return x
```

```python
import functools
import math

import jax
import jax.numpy as jnp
from jax import lax
from jax.experimental import pallas as pl
from jax.experimental.pallas import tpu as pltpu

F32 = jnp.float32
BF = jnp.bfloat16

GRID_W = 64
N_HEADS = 8
MLA_NOPE = 128
MLA_ROPE = 64
MLA_V = 128
ROPE_THETA = 10000.0
RWKV_HEAD = 64
RWKV_GN_EPS = 64e-5
L2_EPS = 1e-12
S5_GROUP = 16
S5_STATE = 64
NORM_EPS = 1e-6

LANES = 128
VMEM_LIMIT = 48 * 1024 * 1024

COL_R, COL_K, COL_V, COL_U, COL_GATE, COL_CQ, COL_CKV, COL_SMALL = (
    0, 1024, 2048, 3072, 4096, 10240, 10752, 11264)
N_IN_PAD = 11776
SM_KR, SM_WD, SM_AD, SM_GD = 0, 64, 192, 320

RWKV_CHUNK = 64
HEAD_GROUP = 256
S5_CHUNK = 16


def _cp(sem, vmem=VMEM_LIMIT):
    return pltpu.CompilerParams(dimension_semantics=sem, vmem_limit_bytes=vmem)


def _mm(a, b):
    return jnp.dot(a.astype(BF), b.astype(BF), preferred_element_type=F32)


def _mm_nt(a, b):
    return lax.dot_general(a.astype(BF), b.astype(BF), (((1,), (1,)), ((), ())),
                           preferred_element_type=F32)


def _mm_tn(a, b):
    return lax.dot_general(a.astype(BF), b.astype(BF), (((0,), (0,)), ((), ())),
                           preferred_element_type=F32)


def _idiv(x, n):
    assert n & (n - 1) == 0, n
    return lax.shift_right_logical(x, n.bit_length() - 1)


def _imod(x, n):
    assert n & (n - 1) == 0, n
    return lax.bitwise_and(x, n - 1)


def _rms(x, g, eps=NORM_EPS):
    return x * lax.rsqrt(jnp.mean(x * x, axis=-1, keepdims=True) + eps) * g


def _softplus(x):
    return jnp.maximum(x, 0.0) + jnp.log(1.0 + jnp.exp(-jnp.abs(x)))


def _ada_kernel(c_ref, w_ref, b_ref, o_ref):
    c = c_ref[...]
    s = c * jax.nn.sigmoid(c)
    o_ref[0] = _mm(s, w_ref[0]) + b_ref[0]


def _ada(cc, ada_w, ada_b):
    n_layer, d, n = ada_w.shape
    tn = 1024
    return pl.pallas_call(
        _ada_kernel,
        grid=(n_layer, n // tn),
        in_specs=[pl.BlockSpec((8, d), lambda l, j: (0, 0)),
                  pl.BlockSpec((1, d, tn), lambda l, j: (l, 0, j)),
                  pl.BlockSpec((1, 1, tn), lambda l, j: (l, 0, j))],
        out_specs=pl.BlockSpec((1, 8, tn), lambda l, j: (l, 0, j)),
        out_shape=jax.ShapeDtypeStruct((n_layer, 8, n), F32),
        compiler_params=_cp(("arbitrary", "arbitrary")),
    )(cc, ada_w, ada_b.reshape(n_layer, 1, n))


def _inproj_kernel(x_ref, g_ref, sh_ref, sc_ref, w_ref, o_ref, xn_ref):
    @pl.when(pl.program_id(1) == 0)
    def _():
        y = _rms(x_ref[...], g_ref[...])
        xn_ref[...] = (y * (1.0 + sc_ref[0, 0]) + sh_ref[0, 0]).astype(BF)

    o_ref[...] = jnp.dot(xn_ref[...], w_ref[...], preferred_element_type=F32).astype(o_ref.dtype)


def _inproj(x2, g, mod, w, row_fn, tm):
    m, d = x2.shape
    n = w.shape[1]
    tn = 512
    return pl.pallas_call(
        _inproj_kernel,
        grid=(m // tm, n // tn),
        in_specs=[pl.BlockSpec((tm, d), lambda i, j: (i, 0)),
                  pl.BlockSpec((1, d), lambda i, j: (0, 0)),
                  pl.BlockSpec((1, 1, 1, d), lambda i, j: (row_fn(i), 0, 0, 0)),
                  pl.BlockSpec((1, 1, 1, d), lambda i, j: (row_fn(i), 1, 0, 0)),
                  pl.BlockSpec((d, tn), lambda i, j: (0, j))],
        out_specs=pl.BlockSpec((tm, tn), lambda i, j: (i, j)),
        out_shape=jax.ShapeDtypeStruct((m, n), BF),
        scratch_shapes=[pltpu.VMEM((tm, d), BF)],
        compiler_params=_cp(("arbitrary", "arbitrary")),
    )(x2, g, mod, mod, w)


def _rope_rotate(x, cos_t, sin_t):
    lane = lax.broadcasted_iota(jnp.int32, x.shape, 1)
    sw = jnp.where(lane < 32, pltpu.roll(x, 96, 1), pltpu.roll(x, 32, 1))
    return x * cos_t + sw * sin_t


def _qprep_kernel(scale, cq_ref, g_ref, w_ref, gn_ref, gr_ref, cos_ref, sin_ref, q_ref):
    xn = _rms(cq_ref[...].astype(F32), g_ref[...])
    q = _mm(xn, w_ref[...])
    cos_t, sin_t = cos_ref[...], sin_ref[...]
    for h in range(N_HEADS):
        nope = q[:, h * 256:h * 256 + 128]
        rp = q[:, h * 256 + 128:h * 256 + 256]
        nope = _rms(nope, gn_ref[...])
        rp = rp * lax.rsqrt(jnp.sum(rp * rp, axis=-1, keepdims=True) * (1.0 / MLA_ROPE) + NORM_EPS) * gr_ref[...]
        rp = _rope_rotate(rp, cos_t, sin_t)
        q_ref[0, h, :, 0:128] = (nope * scale).astype(BF)
        q_ref[0, h, :, 128:256] = (rp * scale).astype(BF)


def _qprep(h2, nb, t, g, w, gn, gr, cos_t, sin_t, tm):
    nt = t // tm
    scale = 1.0 / math.sqrt(MLA_NOPE + MLA_ROPE)
    return pl.pallas_call(
        functools.partial(_qprep_kernel, scale),
        grid=(nb * nt,),
        in_specs=[pl.BlockSpec((tm, 512), lambda i: (i, COL_CQ // 512)),
                  pl.BlockSpec((1, 512), lambda i: (0, 0)),
                  pl.BlockSpec((512, N_HEADS * 256), lambda i: (0, 0)),
                  pl.BlockSpec((1, 128), lambda i: (0, 0)),
                  pl.BlockSpec((1, 128), lambda i: (0, 0)),
                  pl.BlockSpec((tm, 128), lambda i: (i % nt, 0)),
                  pl.BlockSpec((tm, 128), lambda i: (i % nt, 0))],
        out_specs=pl.BlockSpec((1, N_HEADS, tm, 256), lambda i: (i // nt, 0, i % nt, 0)),
        out_shape=jax.ShapeDtypeStruct((nb, N_HEADS, t, 256), BF),
        compiler_params=_cp(("arbitrary",)),
    )(h2, g, w, gn, gr, cos_t, sin_t)


def _kvprep_kernel(ckv_ref, sm_ref, g_ref, w_ref, gn_ref, gr_ref, cos_ref, sin_ref, k_ref, v_ref):
    xn = _rms(ckv_ref[...].astype(F32), g_ref[...])
    kv = _mm(xn, w_ref[...])
    sm = sm_ref[...].astype(F32)
    lane = lax.broadcasted_iota(jnp.int32, sm.shape, 1)
    kr = jnp.where(lane < MLA_ROPE, sm, 0.0)
    kr = kr * lax.rsqrt(jnp.sum(kr * kr, axis=-1, keepdims=True) * (1.0 / MLA_ROPE) + NORM_EPS) * gr_ref[...]
    kr = _rope_rotate(kr, cos_ref[...], sin_ref[...]).astype(BF)
    for h in range(N_HEADS):
        kn = _rms(kv[:, h * 256:h * 256 + 128], gn_ref[...])
        k_ref[0, h, :, 0:128] = kn.astype(BF)
        k_ref[0, h, :, 128:256] = kr
        v_ref[0, h] = kv[:, h * 256 + 128:h * 256 + 256].astype(BF)


def _kvprep(h2, nb, t, g, w, gn, gr, cos_t, sin_t, tm):
    nt = t // tm
    return pl.pallas_call(
        _kvprep_kernel,
        grid=(nb * nt,),
        in_specs=[pl.BlockSpec((tm, 512), lambda i: (i, COL_CKV // 512)),
                  pl.BlockSpec((tm, 128), lambda i: (i, COL_SMALL // 128)),
                  pl.BlockSpec((1, 512), lambda i: (0, 0)),
                  pl.BlockSpec((512, N_HEADS * 256), lambda i: (0, 0)),
                  pl.BlockSpec((1, 128), lambda i: (0, 0)),
                  pl.BlockSpec((1, 128), lambda i: (0, 0)),
                  pl.BlockSpec((tm, 128), lambda i: (i % nt, 0)),
                  pl.BlockSpec((tm, 128), lambda i: (i % nt, 0))],
        out_specs=[pl.BlockSpec((1, N_HEADS, tm, 256), lambda i: (i // nt, 0, i % nt, 0)),
                   pl.BlockSpec((1, N_HEADS, tm, 128), lambda i: (i // nt, 0, i % nt, 0))],
        out_shape=[jax.ShapeDtypeStruct((nb, N_HEADS, t, 256), BF),
                   jax.ShapeDtypeStruct((nb, N_HEADS, t, 128), BF)],
        compiler_params=_cp(("arbitrary",)),
    )(h2, h2, g, w, gn, gr, cos_t, sin_t)


def _attn_kernel(has_extra, nk, *refs):
    if has_extra:
        q_ref, k_ref, v_ref, kc_ref, vc_ref, o_ref, m_ref, l_ref, acc_ref = refs
    else:
        q_ref, k_ref, v_ref, o_ref, m_ref, l_ref, acc_ref = refs
    j = pl.program_id(3)

    @pl.when(j == 0)
    def _():
        m_ref[...] = jnp.full(m_ref.shape, -1e30, F32)
        l_ref[...] = jnp.zeros(l_ref.shape, F32)
        acc_ref[...] = jnp.zeros(acc_ref.shape, F32)

    q = q_ref[0, 0]

    def step(k, v):
        s = lax.dot_general(q, k, (((1,), (1,)), ((), ())), preferred_element_type=F32)
        m_prev = m_ref[...]
        m_new = jnp.maximum(m_prev, jnp.max(s, axis=1, keepdims=True))
        alpha = jnp.exp(m_prev - m_new)
        p = jnp.exp(s - m_new)
        l_ref[...] = alpha * l_ref[...] + jnp.sum(p, axis=1, keepdims=True)
        acc_ref[...] = alpha * acc_ref[...] + jnp.dot(p.astype(BF), v, preferred_element_type=F32)
        m_ref[...] = m_new

    step(k_ref[0, 0], v_ref[0, 0])

    @pl.when(j == nk - 1)
    def _():
        if has_extra:
            step(kc_ref[0, 0], vc_ref[0, 0])
        o_ref[0] = (acc_ref[...] / l_ref[...]).astype(o_ref.dtype)


def _attention(q, k, v, extra, bq, bk):
    nb, nh, t, _ = q.shape
    tk = k.shape[2]
    nk = tk // bk
    in_specs = [pl.BlockSpec((1, 1, bq, 256), lambda b, h, i, j: (b, h, i, 0)),
                pl.BlockSpec((1, 1, bk, 256), lambda b, h, i, j: (b, h, j, 0)),
                pl.BlockSpec((1, 1, bk, 128), lambda b, h, i, j: (b, h, j, 0))]
    args = [q, k, v]
    if extra is not None:
        kc, vc = extra
        tc = kc.shape[2]
        in_specs += [pl.BlockSpec((1, 1, tc, 256), lambda b, h, i, j: (b, h, 0, 0)),
                     pl.BlockSpec((1, 1, tc, 128), lambda b, h, i, j: (b, h, 0, 0))]
        args += [kc, vc]
    return pl.pallas_call(
        functools.partial(_attn_kernel, extra is not None, nk),
        grid=(nb, nh, t // bq, nk),
        in_specs=in_specs,
        out_specs=pl.BlockSpec((1, bq, 128), lambda b, h, i, j: (b, i, h)),
        out_shape=jax.ShapeDtypeStruct((nb, t, nh * MLA_V), BF),
        scratch_shapes=[pltpu.VMEM((bq, 1), F32), pltpu.VMEM((bq, 1), F32),
                        pltpu.VMEM((bq, MLA_V), F32)],
        compiler_params=_cp(("arbitrary",) * 4),
    )(*args)


def _conv_kernel(r_ref, k_ref, v_ref, wr_ref, wk_ref, wv_ref, kk_w_ref, ones_ref,
                 ro_ref, ko_ref, vo_ref, kko_ref):
    t = r_ref.shape[1]
    row = lax.broadcasted_iota(jnp.int32, (t, LANES), 0)

    def conv(x_ref, w_ref):
        x = x_ref[0].astype(F32)
        xm = jnp.where(row == 0, 0.0, pltpu.roll(x, 1, 0))
        xp = jnp.where(row == t - 1, 0.0, pltpu.roll(x, t - 1, 0))
        w = w_ref[...]
        return xm * w[0:1] + x * w[1:2] + xp * w[2:3]

    ro_ref[0] = conv(r_ref, wr_ref).astype(BF)
    vo_ref[0] = conv(v_ref, wv_ref).astype(BF)
    k = conv(k_ref, wk_ref)
    ko_ref[0] = k.astype(BF)
    kk = k * kk_w_ref[...]
    ss = _mm(kk * kk, ones_ref[...])
    kko_ref[0] = (kk * lax.rsqrt(ss + L2_EPS)).astype(BF)


def _rwkv_conv(h3, conv_w, k_k, ones_bd):
    nb, t, _ = h3.shape
    nj = 1024 // LANES
    blk = lambda off: pl.BlockSpec((1, t, LANES), lambda b, j: (b, 0, off // LANES + j))
    wblk = lambda off: pl.BlockSpec((3, LANES), lambda b, j: (0, off // LANES + j))
    oblk = pl.BlockSpec((1, t, LANES), lambda b, j: (b, 0, j))
    osh = jax.ShapeDtypeStruct((nb, t, 1024), BF)
    return pl.pallas_call(
        _conv_kernel,
        grid=(nb, nj),
        in_specs=[blk(COL_R), blk(COL_K), blk(COL_V), wblk(0), wblk(1024), wblk(2048),
                  pl.BlockSpec((1, LANES), lambda b, j: (0, j)),
                  pl.BlockSpec((LANES, LANES), lambda b, j: (0, 0))],
        out_specs=[oblk, oblk, oblk, oblk],
        out_shape=[osh, osh, osh, osh],
        compiler_params=_cp(("arbitrary", "arbitrary")),
    )(h3, h3, h3, conv_w, conv_w, conv_w, k_k, ones_bd)


def _bd_rows(x, bdmask):
    reps = HEAD_GROUP // x.shape[0]
    return jnp.where(bdmask, jnp.concatenate([x] * reps, axis=0), 0.0)


def _scan_direction(d, r_ref, k_ref, v_ref, kk_ref, sm_ref, w0_ref, w2_ref, a0_ref, a2_ref,
                    ka_ref, s_ref, y_ref, masks):
    bdmask, strict, incl, eye, tri, blk = masks[d]
    L = RWKV_CHUNK
    r = r_ref[0].astype(F32)
    k = k_ref[0].astype(F32)
    v = v_ref[0].astype(F32)
    kk = kk_ref[0].astype(F32)
    sm = sm_ref[...].astype(F32)
    wd = sm[:, SM_WD + 64 * d:SM_WD + 64 * d + 64]
    ad = sm[:, SM_AD + 64 * d:SM_AD + 64 * d + 64]
    w_log = -_softplus(-(w0_ref[d:d + 1] + _mm(jnp.tanh(wd), w2_ref[d]))) - 0.5
    lw = -jnp.exp(w_log)
    a = jax.nn.sigmoid(a0_ref[d:d + 1] + _mm(ad, a2_ref[d]))
    b = kk * a
    krep = k * (1.0 + (a - 1.0) * ka_ref[...])
    cl = jnp.dot(tri, lw, precision=lax.Precision.HIGHEST, preferred_element_type=F32)
    cl_last = cl[0:1] if d == 1 else cl[L - 1:L]
    e_neg = jnp.exp(-cl)
    e_last = jnp.exp(cl_last - cl)
    at = -(kk * jnp.exp(cl - lw))
    rt = r * jnp.exp(cl)
    bt = b * e_neg
    kt = krep * e_neg
    bh = b * e_last
    kh = krep * e_last
    p_last = jnp.exp(cl_last)
    zeros = jnp.zeros((L, HEAD_GROUP), F32)
    for gi in range(1024 // HEAD_GROUP):
        sl = slice(gi * HEAD_GROUP, (gi + 1) * HEAD_GROUP)
        at_g, rt_g, bt_g, kt_g, bh_g, kh_g, v_g = (z[:, sl] for z in (at, rt, bt, kt, bh, kh, v))
        ac = _mm_nt(jnp.concatenate([at_g, rt_g], axis=0),
                    jnp.concatenate([_bd_rows(bt_g, bdmask), _bd_rows(kt_g, bdmask)], axis=0))
        a_ab = jnp.where(strict, ac[:L, :HEAD_GROUP], 0.0)
        a_ak = jnp.where(strict, ac[:L, HEAD_GROUP:], 0.0)
        c_b = jnp.where(incl, ac[L:, :HEAD_GROUP], 0.0)
        c_k = jnp.where(incl, ac[L:, HEAD_GROUP:], 0.0)
        a8 = jnp.where(blk[0], a_ab, 0.0)
        p2 = _mm(a8, _bd_rows(a8, bdmask))
        tm = jnp.where(eye, 1.0, 0.0) + a8
        pt = _mm(jnp.concatenate([p2, tm], axis=0), _bd_rows(p2, bdmask))
        tm = tm + pt[L:]
        tm = tm + _mm(tm, _bd_rows(pt[:L], bdmask))
        for lvl in range(3):
            off = jnp.where(blk[lvl + 1] & ~blk[lvl], a_ab, 0.0)
            z = _mm(off, _bd_rows(tm, bdmask))
            tm = tm + _mm(tm, _bd_rows(z, bdmask))
        av = _mm(a_ak, _bd_rows(v_g, bdmask))
        wu = _mm(tm, jnp.concatenate([_bd_rows(at_g, bdmask), _bd_rows(av, bdmask)], axis=1))
        w = wu[:, :HEAD_GROUP]
        uv = wu[:, HEAD_GROUP:]
        q = rt_g + _mm(c_b, _bd_rows(w, bdmask))
        y0 = _mm(jnp.concatenate([c_b, c_k], axis=1),
                 jnp.concatenate([_bd_rows(uv, bdmask), _bd_rows(v_g, bdmask)], axis=0))
        s0 = s_ref[d, gi]
        y_ref[0, :, sl] = y0 + _mm_nt(q, s0)
        gh = _mm_tn(jnp.concatenate([jnp.concatenate([w, uv], axis=1),
                                     jnp.concatenate([zeros, v_g], axis=1)], axis=0),
                    jnp.concatenate([bh_g, kh_g], axis=0))
        g = jnp.where(bdmask, gh[:HEAD_GROUP], 0.0)
        hh = jnp.where(bdmask, gh[HEAD_GROUP:], 0.0)
        s_ref[d, gi] = s0 * p_last[:, sl] + _mm(s0, g) + hh


def _scan_kernel(nc, rf, kf, vf, kkf, smf, rb, kb, vb, kkb, smb, w0, w2, a0, a2, ka, s_in,
                 yf, yb, s_out, s_ref):
    c = pl.program_id(1)

    @pl.when(c == 0)
    def _():
        s_ref[...] = s_in[0]

    L = RWKV_CHUNK
    row2 = lax.broadcasted_iota(jnp.int32, (HEAD_GROUP, HEAD_GROUP), 0)
    col2 = lax.broadcasted_iota(jnp.int32, (HEAD_GROUP, HEAD_GROUP), 1)
    bdmask = _idiv(row2, RWKV_HEAD) == _idiv(col2, RWKV_HEAD)
    tt = lax.broadcasted_iota(jnp.int32, (L, HEAD_GROUP), 0)
    ss = _imod(lax.broadcasted_iota(jnp.int32, (L, HEAD_GROUP), 1), L)
    rl = lax.broadcasted_iota(jnp.int32, (L, L), 0)
    cl = lax.broadcasted_iota(jnp.int32, (L, L), 1)
    blk = tuple(_idiv(ss, n) == _idiv(tt, n) for n in (8, 16, 32, 64))
    masks = ((bdmask, ss < tt, ss <= tt, ss == tt, (rl >= cl).astype(F32), blk),
             (bdmask, ss > tt, ss >= tt, ss == tt, (rl <= cl).astype(F32), blk))
    _scan_direction(0, rf, kf, vf, kkf, smf, w0, w2, a0, a2, ka, s_ref, yf, masks)
    _scan_direction(1, rb, kb, vb, kkb, smb, w0, w2, a0, a2, ka, s_ref, yb, masks)

    @pl.when(c == nc - 1)
    def _():
        s_out[0] = s_ref[...]


def _rwkv_scan(rc, kc, vc, kk, h2, p, s_in):
    nb, t, _ = rc.shape
    L = RWKV_CHUNK
    nc = t // L
    fw = lambda b, c: (b, c, 0)
    bw = lambda b, c: (b, nc - 1 - c, 0)
    seq = lambda im: pl.BlockSpec((1, L, 1024), im)
    smf = pl.BlockSpec((L, 512), lambda b, c: (b * nc + c, COL_SMALL // 512))
    smb = pl.BlockSpec((L, 512), lambda b, c: (b * nc + nc - 1 - c, COL_SMALL // 512))
    full = lambda shape: pl.BlockSpec(shape, lambda b, c: (0,) * len(shape))
    st = pl.BlockSpec((1, 2, 4, HEAD_GROUP, HEAD_GROUP), lambda b, c: (b, 0, 0, 0, 0))
    ysh = jax.ShapeDtypeStruct((nb, t, 1024), F32)
    return pl.pallas_call(
        functools.partial(_scan_kernel, nc),
        grid=(nb, nc),
        in_specs=[seq(fw), seq(fw), seq(fw), seq(fw), smf, seq(bw), seq(bw), seq(bw), seq(bw), smb,
                  full((2, 1024)), full((2, 64, 1024)), full((2, 1024)), full((2, 64, 1024)),
                  full((1, 1024)), st],
        out_specs=[seq(fw), seq(bw), st],
        out_shape=[ysh, ysh, jax.ShapeDtypeStruct(s_in.shape, F32)],
        scratch_shapes=[pltpu.VMEM((2, 4, HEAD_GROUP, HEAD_GROUP), F32)],
        compiler_params=_cp(("arbitrary", "arbitrary")),
    )(rc, kc, vc, kk, h2, rc, kc, vc, kk, h2, p["w0"], p["w2"], p["a0"], p["a2"], p["k_a"], s_in)


def _rpost_kernel(yf_ref, yb_ref, r_ref, k_ref, v_ref, sm_ref, a0_ref, a2_ref, ka_ref, rk_ref,
                  lng_ref, lnb_ref, g2_ref, ones_ref, o_ref):
    y = yf_ref[...] + yb_ref[...]
    r = r_ref[...].astype(F32)
    k = k_ref[...].astype(F32)
    v = v_ref[...].astype(F32)
    sm = sm_ref[...].astype(F32)
    a_sum = 0.0
    for d in range(2):
        ad = sm[:, SM_AD + 64 * d:SM_AD + 64 * d + 64]
        a_sum = a_sum + jax.nn.sigmoid(a0_ref[d:d + 1] + _mm(ad, a2_ref[d]))
    kb = k * (1.0 + (0.5 * a_sum - 1.0) * ka_ref[...])
    rkb = r * kb * rk_ref[...]
    gate = _mm(jax.nn.sigmoid(sm), g2_ref[...])
    ones = ones_ref[...]
    inv_n = 1.0 / RWKV_HEAD
    for gi in range(1024 // HEAD_GROUP):
        sl = slice(gi * HEAD_GROUP, (gi + 1) * HEAD_GROUP)
        yg = y[:, sl]
        mu = _mm(yg, ones) * inv_n
        dlt = yg - mu
        var = _mm(dlt * dlt, ones) * inv_n
        yn = dlt * lax.rsqrt(var + RWKV_GN_EPS) * lng_ref[:, sl] + lnb_ref[:, sl]
        bonus = _mm(rkb[:, sl], ones) * v[:, sl]
        o_ref[:, sl] = ((yn + bonus) * gate[:, sl]).astype(o_ref.dtype)


def _rwkv_post(yf, yb, rc, kc, vc, h2, p, ones_bd, tm):
    m = yf.shape[0]
    row = lambda i: (i, 0)
    full = lambda shape: pl.BlockSpec(shape, lambda i: (0,) * len(shape))
    seq = pl.BlockSpec((tm, 1024), row)
    return pl.pallas_call(
        _rpost_kernel,
        grid=(m // tm,),
        in_specs=[seq, seq, seq, seq, seq,
                  pl.BlockSpec((tm, 512), lambda i: (i, COL_SMALL // 512)),
                  full((2, 1024)), full((2, 64, 1024)), full((1, 1024)), full((1, 1024)),
                  full((1, 1024)), full((1, 1024)), full((512, 1024)),
                  full((HEAD_GROUP, HEAD_GROUP))],
        out_specs=seq,
        out_shape=jax.ShapeDtypeStruct((m, 1024), BF),
        compiler_params=_cp(("arbitrary",)),
    )(yf, yb, rc, kc, vc, h2, p["a0"], p["a2"], p["k_a"], p["r_k"], p["ln_g"], p["ln_b"],
      p["g2_pad"], ones_bd)


def _cmul(x, y):
    return x[0] * y[0] - x[1] * y[1], x[0] * y[1] + x[1] * y[0]


def _s5setup_kernel(lre_ref, lim_ref, ldt_ref, btr_ref, bti_ref, cr_ref, ci_ref,
                    mw_ref, wot_ref, a16_ref):
    n = S5_CHUNK
    half = n // 2
    rt = _idiv(lax.broadcasted_iota(jnp.int32, (n * S5_GROUP, n * S5_GROUP), 0), S5_GROUP)
    ct = _idiv(lax.broadcasted_iota(jnp.int32, (n * S5_GROUP, n * S5_GROUP), 1), S5_GROUP)
    for d in range(2):
        lr = lre_ref[0, d, 0]
        li = lim_ref[0, d, 0]
        dt = jnp.exp(ldt_ref[0, d, 0])

        def cexp(mult):
            mag = jnp.exp(mult * lr * dt)
            ang = mult * li * dt
            return mag * jnp.cos(ang), mag * jnp.sin(ang)

        a1 = cexp(1.0)
        den = lr * lr + li * li
        q_re = ((a1[0] - 1.0) * lr + a1[1] * li) / den
        q_im = (a1[1] * lr - (a1[0] - 1.0) * li) / den
        bb = (q_re * btr_ref[0, d, 0] - q_im * bti_ref[0, d, 0],
              q_re * bti_ref[0, d, 0] + q_im * btr_ref[0, d, 0])
        cc = (cr_ref[0, d, 0], ci_ref[0, d, 0])
        pw = {0: (jnp.ones_like(lr), jnp.zeros_like(lr)), 1: a1, -1: cexp(-1.0)}
        for m in range(2, n + 1):
            pw[m] = _cmul(pw[m - 1], pw[1])
        for m in range(2, half + 1):
            pw[-m] = _cmul(pw[-m + 1], pw[-1])

        def rows(base, efn):
            parts = [_cmul(base, pw[efn(t)]) for t in range(n)]
            return (jnp.concatenate([z[0] for z in parts], axis=0),
                    jnp.concatenate([z[1] for z in parts], axis=0))

        if d == 0:
            e_pow, f_pow = (lambda t: half - t), (lambda t: t - half)
            wi_pow, wo_pow = (lambda t: n - 1 - t), (lambda t: t + 1)
            mask = ct >= rt
        else:
            e_pow, f_pow = (lambda t: t - half), (lambda t: half - t)
            wi_pow, wo_pow = (lambda t: t), (lambda t: n - t)
            mask = ct <= rt
        e = rows(bb, e_pow)
        f = rows(cc, f_pow)
        mmat = lax.dot_general(jnp.concatenate([e[0], -e[1]], axis=1),
                               jnp.concatenate([f[0], f[1]], axis=1),
                               (((1,), (1,)), ((), ())),
                               precision=lax.Precision.HIGHEST, preferred_element_type=F32)
        mmat = jnp.where(mask, mmat, 0.0)
        wi = rows(bb, wi_pow)
        wo = rows(cc, wo_pow)
        mw_ref[0, d, 0] = jnp.concatenate([mmat, wi[0], wi[1]], axis=1).astype(BF)
        wot_ref[0, d, 0] = jnp.concatenate([wo[0], -wo[1]], axis=1).astype(BF)
        a16_ref[0, d, 0] = jnp.concatenate([pw[n][0], pw[n][1]], axis=1)


def _s5_setup(lam_re, lam_im, log_dt, bt_re, bt_im, c_re, c_im):
    nl, _, ng, _ = lam_re.shape
    n2 = S5_CHUNK * S5_GROUP
    vec = pl.BlockSpec((1, 2, 1, 1, S5_STATE), lambda l, g: (l, 0, g, 0, 0))
    sc = pl.BlockSpec((1, 2, 1, 1, 1), lambda l, g: (l, 0, g, 0, 0))
    mat = pl.BlockSpec((1, 2, 1, S5_GROUP, S5_STATE), lambda l, g: (l, 0, g, 0, 0))
    return pl.pallas_call(
        _s5setup_kernel,
        grid=(nl, ng),
        in_specs=[vec, vec, sc, mat, mat, mat, mat],
        out_specs=[pl.BlockSpec((1, 2, 1, n2, n2 + 2 * S5_STATE), lambda l, g: (l, 0, g, 0, 0)),
                   pl.BlockSpec((1, 2, 1, n2, 2 * S5_STATE), lambda l, g: (l, 0, g, 0, 0)),
                   pl.BlockSpec((1, 2, 1, 1, 2 * S5_STATE), lambda l, g: (l, 0, g, 0, 0))],
        out_shape=[jax.ShapeDtypeStruct((nl, 2, ng, n2, n2 + 2 * S5_STATE), BF),
                   jax.ShapeDtypeStruct((nl, 2, ng, n2, 2 * S5_STATE), BF),
                   jax.ShapeDtypeStruct((nl, 2, ng, 1, 2 * S5_STATE), F32)],
        compiler_params=_cp(("arbitrary", "arbitrary")),
    )(lam_re.reshape(nl, 2, ng, 1, S5_STATE), lam_im.reshape(nl, 2, ng, 1, S5_STATE),
      log_dt.reshape(nl, 2, ng, 1, 1), bt_re, bt_im, c_re, c_im)


def _s5_kernel(nb, nc, x_ref, mw_ref, wot_ref, a16_ref, h0_ref, y_ref, hfin_ref):
    x = x_ref[0]
    rows = nb * nc
    n2 = S5_CHUNK * S5_GROUP
    ns = S5_STATE
    cpos = _imod(lax.broadcasted_iota(jnp.int32, (rows, 2 * ns), 0), nc)
    ridx = lax.broadcasted_iota(jnp.int32, (rows, 2 * ns), 0)
    lane = lax.broadcasted_iota(jnp.int32, (1, 2 * ns), 1)
    y = None
    for d in range(2):
        yz = jnp.dot(x, mw_ref[0, d, 0], preferred_element_type=F32)
        yi = yz[:, :n2]
        z = yz[:, n2:]
        first = 0 if d == 0 else nc - 1
        last = nc - 1 if d == 0 else 0
        h0 = h0_ref[0, d]
        h0_rows = jnp.zeros((rows, 2 * ns), F32)
        for bi in range(nb):
            h0_rows = jnp.where(ridx == bi * nc + first, h0[bi:bi + 1], h0_rows)

        def cmul_rows(coef, val):
            c_re = jnp.where(lane < ns, coef, pltpu.roll(coef, ns, 1))
            c_im = jnp.where(lane < ns, -pltpu.roll(coef, ns, 1), coef)
            return val * c_re + pltpu.roll(val, ns, 1) * c_im

        def csq(coef):
            c_re = jnp.where(lane < ns, coef, pltpu.roll(coef, ns, 1))
            c_im = jnp.where(lane < ns, -pltpu.roll(coef, ns, 1), coef)
            return coef * c_re + pltpu.roll(coef, ns, 1) * c_im

        ap = a16_ref[0, d, 0]
        e = z + cmul_rows(ap, h0_rows)
        sh = 1
        while sh < nc:
            if d == 0:
                shifted = jnp.where(cpos >= sh, pltpu.roll(e, sh, 0), 0.0)
            else:
                shifted = jnp.where(cpos < nc - sh, pltpu.roll(e, rows - sh, 0), 0.0)
            e = e + cmul_rows(ap, shifted)
            ap = csq(ap)
            sh *= 2
        if d == 0:
            hs = jnp.where(cpos >= 1, pltpu.roll(e, 1, 0), h0_rows)
        else:
            hs = jnp.where(cpos < nc - 1, pltpu.roll(e, rows - 1, 0), h0_rows)
        yd = yi + _mm_nt(hs, wot_ref[0, d, 0])
        y = yd if y is None else y + yd
        hfin_ref[0, d] = jnp.concatenate(
            [e[bi * nc + last:bi * nc + last + 1] for bi in range(nb)], axis=0)
    y_ref[0] = y.astype(y_ref.dtype)


def _s5_scan(xg, mw, wot, a16, h0, layer, nb):
    ng, rows, n2 = xg.shape
    nc = rows // nb
    par = lambda shape: pl.BlockSpec((1, 2, 1) + shape, lambda g: (layer, 0, g, 0, 0))
    st = pl.BlockSpec((1, 2, nb, 2 * S5_STATE), lambda g: (g, 0, 0, 0))
    return pl.pallas_call(
        functools.partial(_s5_kernel, nb, nc),
        grid=(ng,),
        in_specs=[pl.BlockSpec((1, rows, n2), lambda g: (g, 0, 0)),
                  par((n2, n2 + 2 * S5_STATE)), par((n2, 2 * S5_STATE)), par((1, 2 * S5_STATE)), st],
        out_specs=[pl.BlockSpec((1, rows, n2), lambda g: (g, 0, 0)), st],
        out_shape=[jax.ShapeDtypeStruct((ng, rows, n2), BF),
                   jax.ShapeDtypeStruct((ng, 2, nb, 2 * S5_STATE), F32)],
        compiler_params=_cp(("arbitrary",)),
    )(xg, mw, wot, a16, h0)


def _s5post_kernel(y_ref, u_ref, d_ref, w_ref, b_ref, o_ref):
    yy = y_ref[...].astype(F32) + d_ref[...] * u_ref[...].astype(F32)
    z = jax.nn.gelu(yy)
    o_ref[...] = (z * jax.nn.sigmoid(_mm(z, w_ref[...]) + b_ref[...])).astype(o_ref.dtype)


def _s5_post(y2, h2, d_skip, glu_w, glu_b, tm):
    m = y2.shape[0]
    full = lambda shape: pl.BlockSpec(shape, lambda i: (0,) * len(shape))
    return pl.pallas_call(
        _s5post_kernel,
        grid=(m // tm,),
        in_specs=[pl.BlockSpec((tm, 1024), lambda i: (i, 0)),
                  pl.BlockSpec((tm, 1024), lambda i: (i, COL_U // 1024)),
                  full((1, 1024)), full((1024, 1024)), full((1, 1024))],
        out_specs=pl.BlockSpec((tm, 1024), lambda i: (i, 0)),
        out_shape=jax.ShapeDtypeStruct((m, 1024), BF),
        compiler_params=_cp(("arbitrary",)),
    )(y2, h2, d_skip, glu_w, glu_b)


def _merge_kernel(oa_ref, ob_ref, oc_ref, g0_ref, g1_ref, g2_ref, w_ref, o_ref):
    acc = None
    for o_r, g_r, n in ((oa_ref, g0_ref, 0), (ob_ref, g1_ref, 1), (oc_ref, g2_ref, 2)):
        pr = jnp.dot(o_r[...], w_ref[n], preferred_element_type=F32)
        term = jax.nn.sigmoid(g_r[...].astype(F32)) * pr
        acc = term if acc is None else acc + term
    o_ref[...] = acc.astype(o_ref.dtype)


def _merge(oa, ob, oc, h2, w_branch, tm):
    m = oa.shape[0]
    d = w_branch.shape[2]
    tn = 512
    seq = pl.BlockSpec((tm, 1024), lambda i, j: (i, 0))
    gate = lambda n: pl.BlockSpec((tm, tn), lambda i, j: (i, (COL_GATE + n * d) // tn + j))
    return pl.pallas_call(
        _merge_kernel,
        grid=(m // tm, d // tn),
        in_specs=[seq, seq, seq, gate(0), gate(1), gate(2),
                  pl.BlockSpec((3, 1024, tn), lambda i, j: (0, 0, j))],
        out_specs=pl.BlockSpec((tm, tn), lambda i, j: (i, j)),
        out_shape=jax.ShapeDtypeStruct((m, d), BF),
        compiler_params=_cp(("arbitrary", "arbitrary")),
    )(oa, ob, oc, h2, h2, h2, w_branch)


def _outproj_kernel(m_ref, w_ref, x_ref, gt_ref, o_ref):
    o_ref[...] = x_ref[...] + gt_ref[0, 0] * jnp.dot(m_ref[...], w_ref[...], preferred_element_type=F32)


def _outproj(mg, w_out, x2, mod, row_fn, tm):
    m, d = x2.shape
    tn = 512
    return pl.pallas_call(
        _outproj_kernel,
        grid=(m // tm, d // tn),
        in_specs=[pl.BlockSpec((tm, d), lambda i, j: (i, 0)),
                  pl.BlockSpec((d, tn), lambda i, j: (0, j)),
                  pl.BlockSpec((tm, tn), lambda i, j: (i, j)),
                  pl.BlockSpec((1, 1, 1, tn), lambda i, j: (row_fn(i), 2, 0, j))],
        out_specs=pl.BlockSpec((tm, tn), lambda i, j: (i, j)),
        out_shape=jax.ShapeDtypeStruct((m, d), F32),
        compiler_params=_cp(("arbitrary", "arbitrary")),
    )(mg, w_out, x2, mod)


def _mlp_kernel(nf, x_ref, g_ref, sh_ref, sc_ref, gt_ref, w1_ref, w2_ref, o_ref, xn_ref, acc_ref):
    f = pl.program_id(1)

    @pl.when(f == 0)
    def _():
        y = _rms(x_ref[...], g_ref[...])
        xn_ref[...] = (y * (1.0 + sc_ref[0, 0]) + sh_ref[0, 0]).astype(BF)
        acc_ref[...] = jnp.zeros(acc_ref.shape, F32)

    hmid = jnp.maximum(jnp.dot(xn_ref[...], w1_ref[...], preferred_element_type=F32), 0.0)
    acc_ref[...] += jnp.dot((hmid * hmid).astype(BF), w2_ref[...], preferred_element_type=F32)

    @pl.when(f == nf - 1)
    def _():
        o_ref[...] = x_ref[...] + gt_ref[0, 0] * acc_ref[...]


def _mlp(x2, g, mod, w1, w2, row_fn, tm):
    m, d = x2.shape
    dff = w1.shape[1]
    tf = 512
    nf = dff // tf
    modspec = lambda k: pl.BlockSpec((1, 1, 1, d), lambda i, f: (row_fn(i), k, 0, 0))
    return pl.pallas_call(
        functools.partial(_mlp_kernel, nf),
        grid=(m // tm, nf),
        in_specs=[pl.BlockSpec((tm, d), lambda i, f: (i, 0)),
                  pl.BlockSpec((1, d), lambda i, f: (0, 0)),
                  modspec(3), modspec(4), modspec(5),
                  pl.BlockSpec((d, tf), lambda i, f: (0, f)),
                  pl.BlockSpec((tf, d), lambda i, f: (f, 0))],
        out_specs=pl.BlockSpec((tm, d), lambda i, f: (i, 0)),
        out_shape=jax.ShapeDtypeStruct((m, d), F32),
        scratch_shapes=[pltpu.VMEM((tm, d), BF), pltpu.VMEM((tm, d), F32)],
        compiler_params=_cp(("arbitrary", "arbitrary")),
    )(x2, g, mod, mod, mod, w1, w2)


def _rope_tables(t):
    rows = t // GRID_W
    row = jnp.repeat(jnp.arange(rows, dtype=F32), GRID_W)
    col = jnp.tile(jnp.arange(GRID_W, dtype=F32), rows)
    n_freq = MLA_ROPE // 4
    inv = ROPE_THETA ** (-jnp.arange(n_freq, dtype=F32) / n_freq)
    ang = jnp.concatenate([row[:, None] * inv, col[:, None] * inv], axis=-1)
    cos, sin = jnp.cos(ang), jnp.sin(ang)
    pad = jnp.zeros((t, LANES - MLA_ROPE), F32)
    return (jnp.concatenate([cos, cos, pad], axis=-1), jnp.concatenate([-sin, sin, pad], axis=-1))


def _pad_lanes(g, n):
    return jnp.pad(g.reshape(1, -1), ((0, 0), (0, n - g.shape[-1])))


def _s5_group_layout(h2, nb, t):
    u = h2[:, COL_U:COL_U + 1024].reshape(nb, t // S5_CHUNK, S5_CHUNK, 1024 // S5_GROUP, S5_GROUP)
    return u.transpose(3, 0, 1, 2, 4).reshape(1024 // S5_GROUP, nb * (t // S5_CHUNK), S5_CHUNK * S5_GROUP)


def _s5_token_layout(yg, nb, t):
    ng = yg.shape[0]
    y = yg.reshape(ng, nb, t // S5_CHUNK, S5_CHUNK, S5_GROUP).transpose(1, 2, 3, 0, 4)
    return y.reshape(nb * t, ng * S5_GROUP)


def kernel(x, c, ctx, c_ctx, ada_w, ada_b, norm1_g, norm2_g, w_in, mla_q_lora_g, mla_kv_lora_g, mla_w_uq, mla_w_ukv, mla_qn_nope_g, mla_qn_rope_g, mla_kn_nope_g, mla_kn_rope_g, rwkv_conv, rwkv_w0, rwkv_w2, rwkv_a0, rwkv_a2, rwkv_g2, rwkv_k_k, rwkv_k_a, rwkv_r_k, rwkv_ln_g, rwkv_ln_b, s5_lam_re, s5_lam_im, s5_log_dt, s5_b_re, s5_b_im, s5_c_re, s5_c_im, s5_d, s5_glu_w, s5_glu_b, w_branch, w_out, w_mlp1, w_mlp2):
    nb, t, d = x.shape
    tc = ctx.shape[1]
    depth = ada_w.shape[0]
    ctx_row = nb

    cc = jnp.concatenate([c, c_ctx[None, :], jnp.zeros((8 - nb - 1, d), F32)], axis=0)
    mod_all = _ada(cc, ada_w, ada_b)

    bt = lambda z: jnp.swapaxes(z, -1, -2)
    mw, wot, a16 = _s5_setup(s5_lam_re, s5_lam_im, s5_log_dt, bt(s5_b_re), bt(s5_b_im), s5_c_re, s5_c_im)

    cos_t, sin_t = _rope_tables(t)
    cos_c = jnp.concatenate([jnp.ones((tc, MLA_ROPE), F32), jnp.zeros((tc, LANES - MLA_ROPE), F32)], axis=-1)
    sin_c = jnp.zeros((tc, LANES), F32)
    hi = jnp.arange(HEAD_GROUP) // RWKV_HEAD
    ones_bd = (hi[:, None] == hi[None, :]).astype(BF)
    ones_bd128 = ones_bd[:LANES, :LANES]

    tm_t = min(1024, t)
    tm_c = min(1024, nb * tc)
    lat_row = lambda tm: (lambda i: i // (t // tm))
    ctx_rowf = lambda i: ctx_row

    x2 = x.reshape(nb * t, d)
    xc2 = ctx.reshape(nb * tc, d)
    s_zero = jnp.zeros((nb, 2, 1024 // HEAD_GROUP, HEAD_GROUP, HEAD_GROUP), F32)
    h_zero = jnp.zeros((1024 // S5_GROUP, 2, nb, 2 * S5_STATE), F32)

    for l in range(depth):
        last = l == depth - 1
        mod = mod_all[l].reshape(8, 6, 1, d)
        w = w_in[l]
        w_perm = jnp.concatenate(
            [w[:, 1088:4160], w[:, 4576:5600], w[:, 5600:11744], w[:, 0:1024], w[:, 1024:1088],
             w[:, 4160:4576], jnp.zeros((d, N_IN_PAD - 11744), F32)], axis=1).astype(BF)
        g1 = norm1_g[l].reshape(1, d)
        g2n = norm2_g[l].reshape(1, d)
        w_uq = jnp.pad(mla_w_uq[l].reshape(-1, N_HEADS, MLA_NOPE + MLA_ROPE),
                       ((0, 0), (0, 0), (0, 256 - MLA_NOPE - MLA_ROPE))).reshape(-1, N_HEADS * 256).astype(BF)
        w_ukv = mla_w_ukv[l].astype(BF)
        gq = mla_q_lora_g[l].reshape(1, -1)
        gkv = mla_kv_lora_g[l].reshape(1, -1)
        gqn, gqr = mla_qn_nope_g[l].reshape(1, -1), _pad_lanes(mla_qn_rope_g[l], LANES)
        gkn, gkr = mla_kn_nope_g[l].reshape(1, -1), _pad_lanes(mla_kn_rope_g[l], LANES)
        rp = dict(w0=rwkv_w0[l], w2=rwkv_w2[l].astype(BF), a0=rwkv_a0[l], a2=rwkv_a2[l].astype(BF),
                  k_a=rwkv_k_a[l].reshape(1, -1), r_k=rwkv_r_k[l].reshape(1, -1),
                  ln_g=rwkv_ln_g[l].reshape(1, -1), ln_b=rwkv_ln_b[l].reshape(1, -1),
                  g2_pad=jnp.pad(rwkv_g2[l], ((SM_GD, 512 - SM_GD - rwkv_g2.shape[1]), (0, 0))).astype(BF))
        conv_w = rwkv_conv[l]
        k_k = rwkv_k_k[l].reshape(1, -1)
        glu_w = s5_glu_w[l].astype(BF)
        glu_b = s5_glu_b[l].reshape(1, -1)
        d_skip = s5_d[l].reshape(1, -1)
        wb = w_branch[l].astype(BF)
        wo = w_out[l].astype(BF)
        w1 = w_mlp1[l].astype(BF)
        w2 = w_mlp2[l].astype(BF)

        hc = _inproj(xc2, g1, mod, w_perm, ctx_rowf, tm_c)
        k_c, v_c = _kvprep(hc, nb, tc, gkv, w_ukv, gkn, gkr, cos_c, sin_c, tc)
        rc_c, kc_c, vc_c, kk_c = _rwkv_conv(hc.reshape(nb, tc, -1), conv_w, k_k, ones_bd128)
        yf_c, yb_c, s_ctx = _rwkv_scan(rc_c, kc_c, vc_c, kk_c, hc, rp, s_zero)
        ys_c, h_ctx = _s5_scan(_s5_group_layout(hc, nb, tc), mw, wot, a16, h_zero, l, nb)

        ht = _inproj(x2, g1, mod, w_perm, lat_row(tm_t), tm_t)
        tq = min(512, t)
        q_t = _qprep(ht, nb, t, gq, w_uq, gqn, gqr, cos_t, sin_t, tq)
        k_t, v_t = _kvprep(ht, nb, t, gkv, w_ukv, gkn, gkr, cos_t, sin_t, tq)
        o_a = _attention(q_t, k_t, v_t, (k_c, v_c), min(512, t), min(1024, t)).reshape(nb * t, -1)

        rc_t, kc_t, vc_t, kk_t = _rwkv_conv(ht.reshape(nb, t, -1), conv_w, k_k, ones_bd128)
        yf_t, yb_t, _ = _rwkv_scan(rc_t, kc_t, vc_t, kk_t, ht, rp, s_ctx)
        flat = lambda z: z.reshape(-1, z.shape[-1])
        o_b = _rwkv_post(flat(yf_t), flat(yb_t), flat(rc_t), flat(kc_t), flat(vc_t), ht, rp, ones_bd,
                         min(512, nb * t))

        ys_t, _ = _s5_scan(_s5_group_layout(ht, nb, t), mw, wot, a16, h_ctx, l, nb)
        o_c = _s5_post(_s5_token_layout(ys_t, nb, t), ht, d_skip, glu_w, glu_b, min(512, nb * t))

        mg = _merge(o_a, o_b, o_c, ht, wb, tm_t)
        x2 = _outproj(mg, wo, x2, mod, lat_row(tm_t), tm_t)
        tm_m = min(512, t)
        x2 = _mlp(x2, g2n, mod, w1, w2, lat_row(tm_m), tm_m)

        if not last:
            q_c = _qprep(hc, nb, tc, gq, w_uq, gqn, gqr, cos_c, sin_c, tc)
            o_a_c = _attention(q_c, k_c, v_c, None, tc, tc).reshape(nb * tc, -1)
            o_b_c = _rwkv_post(flat(yf_c), flat(yb_c), flat(rc_c), flat(kc_c), flat(vc_c), hc, rp, ones_bd,
                               min(512, nb * tc))
            o_c_c = _s5_post(_s5_token_layout(ys_c, nb, tc), hc, d_skip, glu_w, glu_b, min(512, nb * tc))
            mg_c = _merge(o_a_c, o_b_c, o_c_c, hc, wb, tm_c)
            xc2 = _outproj(mg_c, wo, xc2, mod, ctx_rowf, tm_c)
            tm_mc = min(512, nb * tc)
            xc2 = _mlp(xc2, g2n, mod, w1, w2, ctx_rowf, tm_mc)

    return x2.reshape(nb, t, d)
```

```python
import functools
import math

import jax
import jax.numpy as jnp
from jax import lax
from jax.experimental import pallas as pl
from jax.experimental.pallas import tpu as pltpu

F32 = jnp.float32
BF = jnp.bfloat16

GRID_W = 64
N_HEADS = 8
MLA_NOPE = 128
MLA_ROPE = 64
MLA_V = 128
ROPE_THETA = 10000.0
RWKV_HEAD = 64
RWKV_GN_EPS = 64e-5
L2_EPS = 1e-12
S5_GROUP = 16
S5_STATE = 64
NORM_EPS = 1e-6

LANES = 128
VMEM_LIMIT = 48 * 1024 * 1024

COL_R, COL_K, COL_V, COL_U, COL_GATE, COL_CQ, COL_CKV, COL_SMALL = (
    0, 1024, 2048, 3072, 4096, 10240, 10752, 11264)
N_IN_PAD = 11776
SM_KR, SM_WD, SM_AD, SM_GD = 0, 64, 192, 320

RWKV_CHUNK = 64
HEAD_GROUP = 256
S5_CHUNK = 16
ATTN_HEADS_PER_STEP = 4


def _cp(sem, vmem=VMEM_LIMIT):
    return pltpu.CompilerParams(dimension_semantics=sem, vmem_limit_bytes=vmem)


def _mm(a, b):
    return jnp.dot(a.astype(BF), b.astype(BF), preferred_element_type=F32)


def _mm_nt(a, b):
    return lax.dot_general(a.astype(BF), b.astype(BF), (((1,), (1,)), ((), ())),
                           preferred_element_type=F32)


def _mm_tn(a, b):
    return lax.dot_general(a.astype(BF), b.astype(BF), (((0,), (0,)), ((), ())),
                           preferred_element_type=F32)


def _idiv(x, n):
    assert n & (n - 1) == 0, n
    return lax.shift_right_logical(x, n.bit_length() - 1)


def _imod(x, n):
    assert n & (n - 1) == 0, n
    return lax.bitwise_and(x, n - 1)


def _rms(x, g, eps=NORM_EPS):
    return x * lax.rsqrt(jnp.mean(x * x, axis=-1, keepdims=True) + eps) * g


def _softplus(x):
    return jnp.maximum(x, 0.0) + jnp.log(1.0 + jnp.exp(-jnp.abs(x)))


def _ada_kernel(c_ref, w_ref, b_ref, o_ref):
    c = c_ref[...]
    s = c * jax.nn.sigmoid(c)
    o_ref[0] = _mm(s, w_ref[0]) + b_ref[0]


def _ada(cc, ada_w, ada_b):
    n_layer, d, n = ada_w.shape
    tn = 1024
    return pl.pallas_call(
        _ada_kernel,
        grid=(n_layer, n // tn),
        in_specs=[pl.BlockSpec((8, d), lambda l, j: (0, 0)),
                  pl.BlockSpec((1, d, tn), lambda l, j: (l, 0, j)),
                  pl.BlockSpec((1, 1, tn), lambda l, j: (l, 0, j))],
        out_specs=pl.BlockSpec((1, 8, tn), lambda l, j: (l, 0, j)),
        out_shape=jax.ShapeDtypeStruct((n_layer, 8, n), F32),
        compiler_params=_cp(("arbitrary", "arbitrary")),
    )(cc, ada_w, ada_b.reshape(n_layer, 1, n))


def _inproj_kernel(x_ref, g_ref, sh_ref, sc_ref, w_ref, o_ref, xn_ref):
    @pl.when(pl.program_id(1) == 0)
    def _():
        y = _rms(x_ref[...], g_ref[...])
        xn_ref[...] = (y * (1.0 + sc_ref[0, 0]) + sh_ref[0, 0]).astype(BF)

    o_ref[...] = jnp.dot(xn_ref[...], w_ref[...], preferred_element_type=F32).astype(o_ref.dtype)


def _inproj(x2, g, mod, w, row_fn, tm):
    m, d = x2.shape
    n = w.shape[1]
    tn = 512
    return pl.pallas_call(
        _inproj_kernel,
        grid=(m // tm, n // tn),
        in_specs=[pl.BlockSpec((tm, d), lambda i, j: (i, 0)),
                  pl.BlockSpec((1, d), lambda i, j: (0, 0)),
                  pl.BlockSpec((1, 1, 1, d), lambda i, j: (row_fn(i), 0, 0, 0)),
                  pl.BlockSpec((1, 1, 1, d), lambda i, j: (row_fn(i), 1, 0, 0)),
                  pl.BlockSpec((d, tn), lambda i, j: (0, j))],
        out_specs=pl.BlockSpec((tm, tn), lambda i, j: (i, j)),
        out_shape=jax.ShapeDtypeStruct((m, n), BF),
        scratch_shapes=[pltpu.VMEM((tm, d), BF)],
        compiler_params=_cp(("arbitrary", "arbitrary")),
    )(x2, g, mod, mod, w)


def _rope_rotate(x, cos_t, sin_t):
    lane = lax.broadcasted_iota(jnp.int32, x.shape, 1)
    sw = jnp.where(lane < 32, pltpu.roll(x, 96, 1), pltpu.roll(x, 32, 1))
    return x * cos_t + sw * sin_t


def _qprep_kernel(scale, cq_ref, g_ref, w_ref, gn_ref, gr_ref, cos_ref, sin_ref, q_ref):
    xn = _rms(cq_ref[...].astype(F32), g_ref[...])
    q = _mm(xn, w_ref[...])
    cos_t, sin_t = cos_ref[...], sin_ref[...]
    for h in range(N_HEADS):
        nope = q[:, h * 256:h * 256 + 128]
        rp = q[:, h * 256 + 128:h * 256 + 256]
        nope = _rms(nope, gn_ref[...])
        rp = rp * lax.rsqrt(jnp.sum(rp * rp, axis=-1, keepdims=True) * (1.0 / MLA_ROPE) + NORM_EPS) * gr_ref[...]
        rp = _rope_rotate(rp, cos_t, sin_t)
        q_ref[0, h, :, 0:128] = (nope * scale).astype(BF)
        q_ref[0, h, :, 128:256] = (rp * scale).astype(BF)


def _qprep(h2, nb, t, g, w, gn, gr, cos_t, sin_t, tm):
    nt = t // tm
    scale = 1.0 / math.sqrt(MLA_NOPE + MLA_ROPE)
    return pl.pallas_call(
        functools.partial(_qprep_kernel, scale),
        grid=(nb * nt,),
        in_specs=[pl.BlockSpec((tm, 512), lambda i: (i, COL_CQ // 512)),
                  pl.BlockSpec((1, 512), lambda i: (0, 0)),
                  pl.BlockSpec((512, N_HEADS * 256), lambda i: (0, 0)),
                  pl.BlockSpec((1, 128), lambda i: (0, 0)),
                  pl.BlockSpec((1, 128), lambda i: (0, 0)),
                  pl.BlockSpec((tm, 128), lambda i: (i % nt, 0)),
                  pl.BlockSpec((tm, 128), lambda i: (i % nt, 0))],
        out_specs=pl.BlockSpec((1, N_HEADS, tm, 256), lambda i: (i // nt, 0, i % nt, 0)),
        out_shape=jax.ShapeDtypeStruct((nb, N_HEADS, t, 256), BF),
        compiler_params=_cp(("arbitrary",)),
    )(h2, g, w, gn, gr, cos_t, sin_t)


def _kvprep_kernel(ckv_ref, sm_ref, g_ref, w_ref, gn_ref, gr_ref, cos_ref, sin_ref, k_ref, v_ref):
    xn = _rms(ckv_ref[...].astype(F32), g_ref[...])
    kv = _mm(xn, w_ref[...])
    sm = sm_ref[...].astype(F32)
    lane = lax.broadcasted_iota(jnp.int32, sm.shape, 1)
    kr = jnp.where(lane < MLA_ROPE, sm, 0.0)
    kr = kr * lax.rsqrt(jnp.sum(kr * kr, axis=-1, keepdims=True) * (1.0 / MLA_ROPE) + NORM_EPS) * gr_ref[...]
    kr = _rope_rotate(kr, cos_ref[...], sin_ref[...]).astype(BF)
    for h in range(N_HEADS):
        kn = _rms(kv[:, h * 256:h * 256 + 128], gn_ref[...])
        k_ref[0, h, :, 0:128] = kn.astype(BF)
        k_ref[0, h, :, 128:256] = kr
        v_ref[0, h] = kv[:, h * 256 + 128:h * 256 + 256].astype(BF)


def _kvprep(h2, nb, t, g, w, gn, gr, cos_t, sin_t, tm):
    nt = t // tm
    return pl.pallas_call(
        _kvprep_kernel,
        grid=(nb * nt,),
        in_specs=[pl.BlockSpec((tm, 512), lambda i: (i, COL_CKV // 512)),
                  pl.BlockSpec((tm, 128), lambda i: (i, COL_SMALL // 128)),
                  pl.BlockSpec((1, 512), lambda i: (0, 0)),
                  pl.BlockSpec((512, N_HEADS * 256), lambda i: (0, 0)),
                  pl.BlockSpec((1, 128), lambda i: (0, 0)),
                  pl.BlockSpec((1, 128), lambda i: (0, 0)),
                  pl.BlockSpec((tm, 128), lambda i: (i % nt, 0)),
                  pl.BlockSpec((tm, 128), lambda i: (i % nt, 0))],
        out_specs=[pl.BlockSpec((1, N_HEADS, tm, 256), lambda i: (i // nt, 0, i % nt, 0)),
                   pl.BlockSpec((1, N_HEADS, tm, 128), lambda i: (i // nt, 0, i % nt, 0))],
        out_shape=[jax.ShapeDtypeStruct((nb, N_HEADS, t, 256), BF),
                   jax.ShapeDtypeStruct((nb, N_HEADS, t, 128), BF)],
        compiler_params=_cp(("arbitrary",)),
    )(h2, h2, g, w, gn, gr, cos_t, sin_t)


def _attn_kernel(has_extra, nk, *refs):
    if has_extra:
        q_ref, k_ref, v_ref, kc_ref, vc_ref, o_ref, m_ref, l_ref, acc_ref = refs
    else:
        q_ref, k_ref, v_ref, o_ref, m_ref, l_ref, acc_ref = refs
    j = pl.program_id(3)

    @pl.when(j == 0)
    def _():
        m_ref[...] = jnp.full(m_ref.shape, -1e30, F32)
        l_ref[...] = jnp.zeros(l_ref.shape, F32)
        acc_ref[...] = jnp.zeros(acc_ref.shape, F32)

    heads = range(ATTN_HEADS_PER_STEP)

    def step(k_r, v_r):
        s = [lax.dot_general(q_ref[0, h], k_r[0, h], (((1,), (1,)), ((), ())),
                             preferred_element_type=F32) for h in heads]
        m_prev = [m_ref[h] for h in heads]
        m_new = [jnp.maximum(m_prev[h], jnp.max(s[h], axis=1, keepdims=True)) for h in heads]
        alpha = [jnp.exp(m_prev[h] - m_new[h]) for h in heads]
        p = [jnp.exp(s[h] - m_new[h]) for h in heads]
        pv = [jnp.dot(p[h].astype(BF), v_r[0, h], preferred_element_type=F32) for h in heads]
        for h in heads:
            l_ref[h] = alpha[h] * l_ref[h] + jnp.sum(p[h], axis=1, keepdims=True)
            acc_ref[h] = alpha[h] * acc_ref[h] + pv[h]
            m_ref[h] = m_new[h]

    step(k_ref, v_ref)

    @pl.when(j == nk - 1)
    def _():
        if has_extra:
            step(kc_ref, vc_ref)
        for h in heads:
            o_ref[0, :, h * MLA_V:(h + 1) * MLA_V] = (acc_ref[h] / l_ref[h]).astype(o_ref.dtype)


def _attention(q, k, v, extra, bq, bk):
    nb, nh, t, _ = q.shape
    tk = k.shape[2]
    nk = tk // bk
    hb = ATTN_HEADS_PER_STEP
    in_specs = [pl.BlockSpec((1, hb, bq, 256), lambda b, h, i, j: (b, h, i, 0)),
                pl.BlockSpec((1, hb, bk, 256), lambda b, h, i, j: (b, h, j, 0)),
                pl.BlockSpec((1, hb, bk, 128), lambda b, h, i, j: (b, h, j, 0))]
    args = [q, k, v]
    if extra is not None:
        kc, vc = extra
        tc = kc.shape[2]
        in_specs += [pl.BlockSpec((1, hb, tc, 256), lambda b, h, i, j: (b, h, 0, 0)),
                     pl.BlockSpec((1, hb, tc, 128), lambda b, h, i, j: (b, h, 0, 0))]
        args += [kc, vc]
    return pl.pallas_call(
        functools.partial(_attn_kernel, extra is not None, nk),
        grid=(nb, nh // hb, t // bq, nk),
        in_specs=in_specs,
        out_specs=pl.BlockSpec((1, bq, hb * MLA_V), lambda b, h, i, j: (b, i, h)),
        out_shape=jax.ShapeDtypeStruct((nb, t, nh * MLA_V), BF),
        scratch_shapes=[pltpu.VMEM((hb, bq, 1), F32), pltpu.VMEM((hb, bq, 1), F32),
                        pltpu.VMEM((hb, bq, MLA_V), F32)],
        compiler_params=_cp(("arbitrary",) * 4),
    )(*args)


def _conv_kernel(r_ref, k_ref, v_ref, wr_ref, wk_ref, wv_ref, kk_w_ref, ones_ref,
                 ro_ref, ko_ref, vo_ref, kko_ref):
    t = r_ref.shape[1]
    row = lax.broadcasted_iota(jnp.int32, (t, LANES), 0)

    def conv(x_ref, w_ref):
        x = x_ref[0].astype(F32)
        xm = jnp.where(row == 0, 0.0, pltpu.roll(x, 1, 0))
        xp = jnp.where(row == t - 1, 0.0, pltpu.roll(x, t - 1, 0))
        w = w_ref[...]
        return xm * w[0:1] + x * w[1:2] + xp * w[2:3]

    ro_ref[0] = conv(r_ref, wr_ref).astype(BF)
    vo_ref[0] = conv(v_ref, wv_ref).astype(BF)
    k = conv(k_ref, wk_ref)
    ko_ref[0] = k.astype(BF)
    kk = k * kk_w_ref[...]
    ss = _mm(kk * kk, ones_ref[...])
    kko_ref[0] = (kk * lax.rsqrt(ss + L2_EPS)).astype(BF)


def _rwkv_conv(h3, conv_w, k_k, ones_bd):
    nb, t, _ = h3.shape
    nj = 1024 // LANES
    blk = lambda off: pl.BlockSpec((1, t, LANES), lambda b, j: (b, 0, off // LANES + j))
    wblk = lambda off: pl.BlockSpec((3, LANES), lambda b, j: (0, off // LANES + j))
    oblk = pl.BlockSpec((1, t, LANES), lambda b, j: (b, 0, j))
    osh = jax.ShapeDtypeStruct((nb, t, 1024), BF)
    return pl.pallas_call(
        _conv_kernel,
        grid=(nb, nj),
        in_specs=[blk(COL_R), blk(COL_K), blk(COL_V), wblk(0), wblk(1024), wblk(2048),
                  pl.BlockSpec((1, LANES), lambda b, j: (0, j)),
                  pl.BlockSpec((LANES, LANES), lambda b, j: (0, 0))],
        out_specs=[oblk, oblk, oblk, oblk],
        out_shape=[osh, osh, osh, osh],
        compiler_params=_cp(("arbitrary", "arbitrary")),
    )(h3, h3, h3, conv_w, conv_w, conv_w, k_k, ones_bd)


def _bd_rows(x, bdmask):
    reps = HEAD_GROUP // x.shape[0]
    return jnp.where(bdmask, jnp.concatenate([x] * reps, axis=0), 0.0)


def _scan_prep(d, r_ref, k_ref, v_ref, kk_ref, sm_ref, w0_ref, w2_ref, a0_ref, a2_ref, ka_ref, tri):
    L = RWKV_CHUNK
    r = r_ref[0].astype(F32)
    k = k_ref[0].astype(F32)
    v = v_ref[0].astype(F32)
    kk = kk_ref[0].astype(F32)
    sm = sm_ref[...].astype(F32)
    wd = sm[:, SM_WD + 64 * d:SM_WD + 64 * d + 64]
    ad = sm[:, SM_AD + 64 * d:SM_AD + 64 * d + 64]
    w_log = -_softplus(-(w0_ref[d:d + 1] + _mm(jnp.tanh(wd), w2_ref[d]))) - 0.5
    lw = -jnp.exp(w_log)
    a = jax.nn.sigmoid(a0_ref[d:d + 1] + _mm(ad, a2_ref[d]))
    b = kk * a
    krep = k * (1.0 + (a - 1.0) * ka_ref[...])
    cl = jnp.dot(tri, lw, precision=lax.Precision.HIGHEST, preferred_element_type=F32)
    cl_last = cl[0:1] if d == 1 else cl[L - 1:L]
    e_neg = jnp.exp(-cl)
    e_last = jnp.exp(cl_last - cl)
    at = -(kk * jnp.exp(cl - lw))
    rt = r * jnp.exp(cl)
    bt = b * e_neg
    kt = krep * e_neg
    bh = b * e_last
    kh = krep * e_last
    return dict(at=at, rt=rt, bt=bt, kt=kt, bh=bh, kh=kh, v=v, p_last=jnp.exp(cl_last))


def _scan_kernel(nc, rf, kf, vf, kkf, smf, rb, kb, vb, kkb, smb, w0, w2, a0, a2, ka, s_in,
                 yf, yb, s_out, s_ref):
    c = pl.program_id(1)

    @pl.when(c == 0)
    def _():
        s_ref[...] = s_in[0]

    L = RWKV_CHUNK
    row2 = lax.broadcasted_iota(jnp.int32, (HEAD_GROUP, HEAD_GROUP), 0)
    col2 = lax.broadcasted_iota(jnp.int32, (HEAD_GROUP, HEAD_GROUP), 1)
    bdmask = _idiv(row2, RWKV_HEAD) == _idiv(col2, RWKV_HEAD)
    tt = lax.broadcasted_iota(jnp.int32, (L, HEAD_GROUP), 0)
    ss = _imod(lax.broadcasted_iota(jnp.int32, (L, HEAD_GROUP), 1), L)
    rl = lax.broadcasted_iota(jnp.int32, (L, L), 0)
    cl = lax.broadcasted_iota(jnp.int32, (L, L), 1)
    blk = tuple(_idiv(ss, n) == _idiv(tt, n) for n in (8, 16, 32, 64))
    eye = jnp.where(ss == tt, 1.0, 0.0)
    strict = (ss < tt, ss > tt)
    incl = (ss <= tt, ss >= tt)
    prep = (_scan_prep(0, rf, kf, vf, kkf, smf, w0, w2, a0, a2, ka, (rl >= cl).astype(F32)),
            _scan_prep(1, rb, kb, vb, kkb, smb, w0, w2, a0, a2, ka, (rl <= cl).astype(F32)))
    y_refs = (yf, yb)

    chains = [(d, gi) for d in range(2) for gi in range(1024 // HEAD_GROUP)]
    n = len(chains)
    bd = lambda x: _bd_rows(x, bdmask)
    cat = jnp.concatenate
    g_ = lambda name: [prep[d][name][:, gi * HEAD_GROUP:(gi + 1) * HEAD_GROUP] for d, gi in chains]
    at, rt, bt, kt, bh, kh, v = (g_(nm) for nm in ("at", "rt", "bt", "kt", "bh", "kh", "v"))
    ac = [_mm_nt(cat([at[i], rt[i]], axis=0), cat([bd(bt[i]), bd(kt[i])], axis=0)) for i in range(n)]
    a_ab = [jnp.where(strict[chains[i][0]], ac[i][:L, :HEAD_GROUP], 0.0) for i in range(n)]
    a_ak = [jnp.where(strict[chains[i][0]], ac[i][:L, HEAD_GROUP:], 0.0) for i in range(n)]
    c_b = [jnp.where(incl[chains[i][0]], ac[i][L:, :HEAD_GROUP], 0.0) for i in range(n)]
    c_k = [jnp.where(incl[chains[i][0]], ac[i][L:, HEAD_GROUP:], 0.0) for i in range(n)]
    av = [_mm(a_ak[i], bd(v[i])) for i in range(n)]
    a8 = [jnp.where(blk[0], a_ab[i], 0.0) for i in range(n)]
    p2 = [_mm(a8[i], bd(a8[i])) for i in range(n)]
    tm = [eye + a8[i] for i in range(n)]
    pt = [_mm(cat([p2[i], tm[i]], axis=0), bd(p2[i])) for i in range(n)]
    tm = [tm[i] + pt[i][L:] for i in range(n)]
    tm = [tm[i] + _mm(tm[i], bd(pt[i][:L])) for i in range(n)]
    for lvl in range(3):
        off_mask = blk[lvl + 1] & ~blk[lvl]
        z = [_mm(jnp.where(off_mask, a_ab[i], 0.0), bd(tm[i])) for i in range(n)]
        tm = [tm[i] + _mm(tm[i], bd(z[i])) for i in range(n)]
    wu = [_mm(tm[i], cat([bd(at[i]), bd(av[i])], axis=1)) for i in range(n)]
    w = [wu[i][:, :HEAD_GROUP] for i in range(n)]
    uv = [wu[i][:, HEAD_GROUP:] for i in range(n)]
    q = [rt[i] + _mm(c_b[i], bd(w[i])) for i in range(n)]
    y0 = [_mm(cat([c_b[i], c_k[i]], axis=1), cat([bd(uv[i]), bd(v[i])], axis=0)) for i in range(n)]
    zeros = jnp.zeros((L, HEAD_GROUP), F32)
    gh = [_mm_tn(cat([cat([w[i], uv[i]], axis=1), cat([zeros, v[i]], axis=1)], axis=0),
                 cat([bh[i], kh[i]], axis=0)) for i in range(n)]
    for i, (d, gi) in enumerate(chains):
        sl = slice(gi * HEAD_GROUP, (gi + 1) * HEAD_GROUP)
        s0 = s_ref[d, gi]
        y_refs[d][0, :, sl] = y0[i] + _mm_nt(q[i], s0)
        g = jnp.where(bdmask, gh[i][:HEAD_GROUP], 0.0)
        hh = jnp.where(bdmask, gh[i][HEAD_GROUP:], 0.0)
        s_ref[d, gi] = s0 * prep[d]["p_last"][:, sl] + _mm(s0, g) + hh

    @pl.when(c == nc - 1)
    def _():
        s_out[0] = s_ref[...]


def _rwkv_scan(rc, kc, vc, kk, h2, p, s_in):
    nb, t, _ = rc.shape
    L = RWKV_CHUNK
    nc = t // L
    fw = lambda b, c: (b, c, 0)
    bw = lambda b, c: (b, nc - 1 - c, 0)
    seq = lambda im: pl.BlockSpec((1, L, 1024), im)
    smf = pl.BlockSpec((L, 512), lambda b, c: (b * nc + c, COL_SMALL // 512))
    smb = pl.BlockSpec((L, 512), lambda b, c: (b * nc + nc - 1 - c, COL_SMALL // 512))
    full = lambda shape: pl.BlockSpec(shape, lambda b, c: (0,) * len(shape))
    st = pl.BlockSpec((1, 2, 4, HEAD_GROUP, HEAD_GROUP), lambda b, c: (b, 0, 0, 0, 0))
    ysh = jax.ShapeDtypeStruct((nb, t, 1024), F32)
    return pl.pallas_call(
        functools.partial(_scan_kernel, nc),
        grid=(nb, nc),
        in_specs=[seq(fw), seq(fw), seq(fw), seq(fw), smf, seq(bw), seq(bw), seq(bw), seq(bw), smb,
                  full((2, 1024)), full((2, 64, 1024)), full((2, 1024)), full((2, 64, 1024)),
                  full((1, 1024)), st],
        out_specs=[seq(fw), seq(bw), st],
        out_shape=[ysh, ysh, jax.ShapeDtypeStruct(s_in.shape, F32)],
        scratch_shapes=[pltpu.VMEM((2, 4, HEAD_GROUP, HEAD_GROUP), F32)],
        compiler_params=_cp(("arbitrary", "arbitrary")),
    )(rc, kc, vc, kk, h2, rc, kc, vc, kk, h2, p["w0"], p["w2"], p["a0"], p["a2"], p["k_a"], s_in)


def _rpost_kernel(yf_ref, yb_ref, r_ref, k_ref, v_ref, sm_ref, a0_ref, a2_ref, ka_ref, rk_ref,
                  lng_ref, lnb_ref, g2_ref, ones_ref, o_ref):
    y = yf_ref[...] + yb_ref[...]
    r = r_ref[...].astype(F32)
    k = k_ref[...].astype(F32)
    v = v_ref[...].astype(F32)
    sm = sm_ref[...].astype(F32)
    a_sum = 0.0
    for d in range(2):
        ad = sm[:, SM_AD + 64 * d:SM_AD + 64 * d + 64]
        a_sum = a_sum + jax.nn.sigmoid(a0_ref[d:d + 1] + _mm(ad, a2_ref[d]))
    kb = k * (1.0 + (0.5 * a_sum - 1.0) * ka_ref[...])
    rkb = r * kb * rk_ref[...]
    gate = _mm(jax.nn.sigmoid(sm), g2_ref[...])
    ones = ones_ref[...]
    inv_n = 1.0 / RWKV_HEAD
    for gi in range(1024 // HEAD_GROUP):
        sl = slice(gi * HEAD_GROUP, (gi + 1) * HEAD_GROUP)
        yg = y[:, sl]
        mu = _mm(yg, ones) * inv_n
        dlt = yg - mu
        var = _mm(dlt * dlt, ones) * inv_n
        yn = dlt * lax.rsqrt(var + RWKV_GN_EPS) * lng_ref[:, sl] + lnb_ref[:, sl]
        bonus = _mm(rkb[:, sl], ones) * v[:, sl]
        o_ref[:, sl] = ((yn + bonus) * gate[:, sl]).astype(o_ref.dtype)


def _rwkv_post(yf, yb, rc, kc, vc, h2, p, ones_bd, tm):
    m = yf.shape[0]
    row = lambda i: (i, 0)
    full = lambda shape: pl.BlockSpec(shape, lambda i: (0,) * len(shape))
    seq = pl.BlockSpec((tm, 1024), row)
    return pl.pallas_call(
        _rpost_kernel,
        grid=(m // tm,),
        in_specs=[seq, seq, seq, seq, seq,
                  pl.BlockSpec((tm, 512), lambda i: (i, COL_SMALL // 512)),
                  full((2, 1024)), full((2, 64, 1024)), full((1, 1024)), full((1, 1024)),
                  full((1, 1024)), full((1, 1024)), full((512, 1024)),
                  full((HEAD_GROUP, HEAD_GROUP))],
        out_specs=seq,
        out_shape=jax.ShapeDtypeStruct((m, 1024), BF),
        compiler_params=_cp(("arbitrary",)),
    )(yf, yb, rc, kc, vc, h2, p["a0"], p["a2"], p["k_a"], p["r_k"], p["ln_g"], p["ln_b"],
      p["g2_pad"], ones_bd)


def _cmul(x, y):
    return x[0] * y[0] - x[1] * y[1], x[0] * y[1] + x[1] * y[0]


def _s5setup_kernel(lre_ref, lim_ref, ldt_ref, btr_ref, bti_ref, cr_ref, ci_ref,
                    mw_ref, wot_ref, a16_ref):
    n = S5_CHUNK
    half = n // 2
    rt = _idiv(lax.broadcasted_iota(jnp.int32, (n * S5_GROUP, n * S5_GROUP), 0), S5_GROUP)
    ct = _idiv(lax.broadcasted_iota(jnp.int32, (n * S5_GROUP, n * S5_GROUP), 1), S5_GROUP)
    for d in range(2):
        lr = lre_ref[0, d, 0]
        li = lim_ref[0, d, 0]
        dt = jnp.exp(ldt_ref[0, d, 0])

        def cexp(mult):
            mag = jnp.exp(mult * lr * dt)
            ang = mult * li * dt
            return mag * jnp.cos(ang), mag * jnp.sin(ang)

        a1 = cexp(1.0)
        den = lr * lr + li * li
        q_re = ((a1[0] - 1.0) * lr + a1[1] * li) / den
        q_im = (a1[1] * lr - (a1[0] - 1.0) * li) / den
        bb = (q_re * btr_ref[0, d, 0] - q_im * bti_ref[0, d, 0],
              q_re * bti_ref[0, d, 0] + q_im * btr_ref[0, d, 0])
        cc = (cr_ref[0, d, 0], ci_ref[0, d, 0])
        pw = {0: (jnp.ones_like(lr), jnp.zeros_like(lr)), 1: a1, -1: cexp(-1.0)}
        for m in range(2, n + 1):
            pw[m] = _cmul(pw[m - 1], pw[1])
        for m in range(2, half + 1):
            pw[-m] = _cmul(pw[-m + 1], pw[-1])

        def rows(base, efn):
            parts = [_cmul(base, pw[efn(t)]) for t in range(n)]
            return (jnp.concatenate([z[0] for z in parts], axis=0),
                    jnp.concatenate([z[1] for z in parts], axis=0))

        if d == 0:
            e_pow, f_pow = (lambda t: half - t), (lambda t: t - half)
            wi_pow, wo_pow = (lambda t: n - 1 - t), (lambda t: t + 1)
            mask = ct >= rt
        else:
            e_pow, f_pow = (lambda t: t - half), (lambda t: half - t)
            wi_pow, wo_pow = (lambda t: t), (lambda t: n - t)
            mask = ct <= rt
        e = rows(bb, e_pow)
        f = rows(cc, f_pow)
        mmat = lax.dot_general(jnp.concatenate([e[0], -e[1]], axis=1),
                               jnp.concatenate([f[0], f[1]], axis=1),
                               (((1,), (1,)), ((), ())),
                               precision=lax.Precision.HIGHEST, preferred_element_type=F32)
        mmat = jnp.where(mask, mmat, 0.0)
        wi = rows(bb, wi_pow)
        wo = rows(cc, wo_pow)
        mw_ref[0, d, 0] = jnp.concatenate([mmat, wi[0], wi[1]], axis=1).astype(BF)
        wot_ref[0, d, 0] = jnp.concatenate([wo[0], -wo[1]], axis=1).astype(BF)
        a16_ref[0, d, 0] = jnp.concatenate([pw[n][0], pw[n][1]], axis=1)


def _s5_setup(lam_re, lam_im, log_dt, bt_re, bt_im, c_re, c_im):
    nl, _, ng, _ = lam_re.shape
    n2 = S5_CHUNK * S5_GROUP
    vec = pl.BlockSpec((1, 2, 1, 1, S5_STATE), lambda l, g: (l, 0, g, 0, 0))
    sc = pl.BlockSpec((1, 2, 1, 1, 1), lambda l, g: (l, 0, g, 0, 0))
    mat = pl.BlockSpec((1, 2, 1, S5_GROUP, S5_STATE), lambda l, g: (l, 0, g, 0, 0))
    return pl.pallas_call(
        _s5setup_kernel,
        grid=(nl, ng),
        in_specs=[vec, vec, sc, mat, mat, mat, mat],
        out_specs=[pl.BlockSpec((1, 2, 1, n2, n2 + 2 * S5_STATE), lambda l, g: (l, 0, g, 0, 0)),
                   pl.BlockSpec((1, 2, 1, n2, 2 * S5_STATE), lambda l, g: (l, 0, g, 0, 0)),
                   pl.BlockSpec((1, 2, 1, 1, 2 * S5_STATE), lambda l, g: (l, 0, g, 0, 0))],
        out_shape=[jax.ShapeDtypeStruct((nl, 2, ng, n2, n2 + 2 * S5_STATE), BF),
                   jax.ShapeDtypeStruct((nl, 2, ng, n2, 2 * S5_STATE), BF),
                   jax.ShapeDtypeStruct((nl, 2, ng, 1, 2 * S5_STATE), F32)],
        compiler_params=_cp(("arbitrary", "arbitrary")),
    )(lam_re.reshape(nl, 2, ng, 1, S5_STATE), lam_im.reshape(nl, 2, ng, 1, S5_STATE),
      log_dt.reshape(nl, 2, ng, 1, 1), bt_re, bt_im, c_re, c_im)


S5_GROUPS_PER_STEP = LANES // S5_GROUP


def _s5_kernel(nc, *refs):
    n = S5_CHUNK
    u_refs = refs[:n]
    perm_ref, mw_ref, wot_ref, a16_ref, h0_ref = refs[n:n + 5]
    y_refs = refs[n + 5:2 * n + 5]
    hfin_ref = refs[2 * n + 5]
    n2 = S5_CHUNK * S5_GROUP
    ns = S5_STATE
    perm = perm_ref[...]
    ucat = jnp.concatenate([r[...] for r in u_refs], axis=1)
    xcat = jnp.dot(ucat, perm, preferred_element_type=F32).astype(BF)
    cpos = lax.broadcasted_iota(jnp.int32, (nc, 2 * ns), 0)
    lane = lax.broadcasted_iota(jnp.int32, (1, 2 * ns), 1)

    def coef_pair(coef):
        sw = pltpu.roll(coef, ns, 1)
        return jnp.where(lane < ns, coef, sw), jnp.where(lane < ns, -sw, coef)

    def cmul_rows(coef, val):
        c_re, c_im = coef_pair(coef)
        return val * c_re + pltpu.roll(val, ns, 1) * c_im

    chains = [(g, d) for g in range(S5_GROUPS_PER_STEP) for d in range(2)]
    rng = range(len(chains))
    first = (0, nc - 1)
    last = (nc - 1, 0)
    yz = [jnp.dot(xcat[:, g * n2:(g + 1) * n2], mw_ref[0, d, g], preferred_element_type=F32)
          for g, d in chains]
    h0r = [jnp.where(cpos == first[d], h0_ref[0, g, d:d + 1], 0.0) for g, d in chains]
    ap = [a16_ref[0, d, g] for g, d in chains]
    e = [yz[i][:, n2:] + cmul_rows(ap[i], h0r[i]) for i in rng]
    sh = 1
    while sh < nc:
        shifted = [jnp.where(cpos >= sh, pltpu.roll(e[i], sh, 0), 0.0) if chains[i][1] == 0 else
                   jnp.where(cpos < nc - sh, pltpu.roll(e[i], nc - sh, 0), 0.0) for i in rng]
        e = [e[i] + cmul_rows(ap[i], shifted[i]) for i in rng]
        ap = [cmul_rows(ap[i], ap[i]) for i in rng]
        sh *= 2
    hs = [jnp.where(cpos >= 1, pltpu.roll(e[i], 1, 0), h0r[i]) if chains[i][1] == 0 else
          jnp.where(cpos < nc - 1, pltpu.roll(e[i], nc - 1, 0), h0r[i]) for i in rng]
    yd = [yz[i][:, :n2] + _mm_nt(hs[i], wot_ref[0, d, g]) for i, (g, d) in enumerate(chains)]
    ycat = jnp.concatenate([yd[2 * g] + yd[2 * g + 1] for g in range(S5_GROUPS_PER_STEP)], axis=1)
    ytok = lax.dot_general(ycat.astype(BF), perm, (((1,), (1,)), ((), ())), preferred_element_type=F32)
    for tau in range(n):
        y_refs[tau][...] = ytok[:, tau * LANES:(tau + 1) * LANES].astype(y_refs[tau].dtype)
    for i, (g, d) in enumerate(chains):
        hfin_ref[0, g, d:d + 1] = e[i][last[d]:last[d] + 1]


def _s5_scan(h2, nb, t, perm, mw, wot, a16, h0, layer):
    n = S5_CHUNK
    nc = t // n
    gs = S5_GROUPS_PER_STEP
    ng = 1024 // S5_GROUP
    n2 = S5_CHUNK * S5_GROUP
    ncol = h2.shape[1] // LANES
    hv = h2.reshape(nb * nc, n * h2.shape[1])
    u_spec = lambda tau: pl.BlockSpec((nc, LANES), lambda s, b: (b, tau * ncol + COL_U // LANES + s))
    y_spec = pl.BlockSpec((nc, LANES), lambda s, b: (b, s))
    par = lambda shape: pl.BlockSpec((1, 2, gs) + shape, lambda s, b: (layer, 0, s, 0, 0))
    st = pl.BlockSpec((1, gs, 2, 2 * S5_STATE), lambda s, b: (b, s, 0, 0))
    outs = pl.pallas_call(
        functools.partial(_s5_kernel, nc),
        grid=(ng // gs, nb),
        in_specs=[u_spec(tau) for tau in range(n)] + [
            pl.BlockSpec(perm.shape, lambda s, b: (0, 0)),
            par((n2, n2 + 2 * S5_STATE)), par((n2, 2 * S5_STATE)), par((1, 2 * S5_STATE)), st],
        out_specs=[y_spec] * n + [st],
        out_shape=[jax.ShapeDtypeStruct((nb * nc, 1024), BF) for _ in range(n)] + [
            jax.ShapeDtypeStruct((nb, ng, 2, 2 * S5_STATE), F32)],
        compiler_params=_cp(("arbitrary", "arbitrary")),
    )(*([hv] * n), perm, mw, wot, a16, h0)
    return jnp.stack(outs[:n], axis=1).reshape(nb * t, 1024), outs[n]


def _s5post_kernel(y_ref, u_ref, d_ref, w_ref, b_ref, o_ref):
    yy = y_ref[...].astype(F32) + d_ref[...] * u_ref[...].astype(F32)
    z = jax.nn.gelu(yy)
    o_ref[...] = (z * jax.nn.sigmoid(_mm(z, w_ref[...]) + b_ref[...])).astype(o_ref.dtype)


def _s5_post(y2, h2, d_skip, glu_w, glu_b, tm):
    m = y2.shape[0]
    full = lambda shape: pl.BlockSpec(shape, lambda i: (0,) * len(shape))
    return pl.pallas_call(
        _s5post_kernel,
        grid=(m // tm,),
        in_specs=[pl.BlockSpec((tm, 1024), lambda i: (i, 0)),
                  pl.BlockSpec((tm, 1024), lambda i: (i, COL_U // 1024)),
                  full((1, 1024)), full((1024, 1024)), full((1, 1024))],
        out_specs=pl.BlockSpec((tm, 1024), lambda i: (i, 0)),
        out_shape=jax.ShapeDtypeStruct((m, 1024), BF),
        compiler_params=_cp(("arbitrary",)),
    )(y2, h2, d_skip, glu_w, glu_b)


def _merge_kernel(oa_ref, ob_ref, oc_ref, g0_ref, g1_ref, g2_ref, w_ref, o_ref):
    acc = None
    for o_r, g_r, n in ((oa_ref, g0_ref, 0), (ob_ref, g1_ref, 1), (oc_ref, g2_ref, 2)):
        pr = jnp.dot(o_r[...], w_ref[n], preferred_element_type=F32)
        term = jax.nn.sigmoid(g_r[...].astype(F32)) * pr
        acc = term if acc is None else acc + term
    o_ref[...] = acc.astype(o_ref.dtype)


def _merge(oa, ob, oc, h2, w_branch, tm):
    m = oa.shape[0]
    d = w_branch.shape[2]
    tn = 512
    seq = pl.BlockSpec((tm, 1024), lambda i, j: (i, 0))
    gate = lambda n: pl.BlockSpec((tm, tn), lambda i, j: (i, (COL_GATE + n * d) // tn + j))
    return pl.pallas_call(
        _merge_kernel,
        grid=(m // tm, d // tn),
        in_specs=[seq, seq, seq, gate(0), gate(1), gate(2),
                  pl.BlockSpec((3, 1024, tn), lambda i, j: (0, 0, j))],
        out_specs=pl.BlockSpec((tm, tn), lambda i, j: (i, j)),
        out_shape=jax.ShapeDtypeStruct((m, d), BF),
        compiler_params=_cp(("arbitrary", "arbitrary")),
    )(oa, ob, oc, h2, h2, h2, w_branch)


def _outproj_kernel(m_ref, w_ref, x_ref, gt_ref, o_ref):
    o_ref[...] = x_ref[...] + gt_ref[0, 0] * jnp.dot(m_ref[...], w_ref[...], preferred_element_type=F32)


def _outproj(mg, w_out, x2, mod, row_fn, tm):
    m, d = x2.shape
    tn = 512
    return pl.pallas_call(
        _outproj_kernel,
        grid=(m // tm, d // tn),
        in_specs=[pl.BlockSpec((tm, d), lambda i, j: (i, 0)),
                  pl.BlockSpec((d, tn), lambda i, j: (0, j)),
                  pl.BlockSpec((tm, tn), lambda i, j: (i, j)),
                  pl.BlockSpec((1, 1, 1, tn), lambda i, j: (row_fn(i), 2, 0, j))],
        out_specs=pl.BlockSpec((tm, tn), lambda i, j: (i, j)),
        out_shape=jax.ShapeDtypeStruct((m, d), F32),
        compiler_params=_cp(("arbitrary", "arbitrary")),
    )(mg, w_out, x2, mod)


def _mlp_kernel(nf, x_ref, g_ref, sh_ref, sc_ref, gt_ref, w1_ref, w2_ref, o_ref, xn_ref, acc_ref):
    f = pl.program_id(1)

    @pl.when(f == 0)
    def _():
        y = _rms(x_ref[...], g_ref[...])
        xn_ref[...] = (y * (1.0 + sc_ref[0, 0]) + sh_ref[0, 0]).astype(BF)
        acc_ref[...] = jnp.zeros(acc_ref.shape, F32)

    hmid = jnp.maximum(jnp.dot(xn_ref[...], w1_ref[...], preferred_element_type=F32), 0.0)
    acc_ref[...] += jnp.dot((hmid * hmid).astype(BF), w2_ref[...], preferred_element_type=F32)

    @pl.when(f == nf - 1)
    def _():
        o_ref[...] = x_ref[...] + gt_ref[0, 0] * acc_ref[...]


def _mlp(x2, g, mod, w1, w2, row_fn, tm):
    m, d = x2.shape
    dff = w1.shape[1]
    tf = 512
    nf = dff // tf
    modspec = lambda k: pl.BlockSpec((1, 1, 1, d), lambda i, f: (row_fn(i), k, 0, 0))
    return pl.pallas_call(
        functools.partial(_mlp_kernel, nf),
        grid=(m // tm, nf),
        in_specs=[pl.BlockSpec((tm, d), lambda i, f: (i, 0)),
                  pl.BlockSpec((1, d), lambda i, f: (0, 0)),
                  modspec(3), modspec(4), modspec(5),
                  pl.BlockSpec((d, tf), lambda i, f: (0, f)),
                  pl.BlockSpec((tf, d), lambda i, f: (f, 0))],
        out_specs=pl.BlockSpec((tm, d), lambda i, f: (i, 0)),
        out_shape=jax.ShapeDtypeStruct((m, d), F32),
        scratch_shapes=[pltpu.VMEM((tm, d), BF), pltpu.VMEM((tm, d), F32)],
        compiler_params=_cp(("arbitrary", "arbitrary")),
    )(x2, g, mod, mod, mod, w1, w2)


def _rope_tables(t):
    rows = t // GRID_W
    row = jnp.repeat(jnp.arange(rows, dtype=F32), GRID_W)
    col = jnp.tile(jnp.arange(GRID_W, dtype=F32), rows)
    n_freq = MLA_ROPE // 4
    inv = ROPE_THETA ** (-jnp.arange(n_freq, dtype=F32) / n_freq)
    ang = jnp.concatenate([row[:, None] * inv, col[:, None] * inv], axis=-1)
    cos, sin = jnp.cos(ang), jnp.sin(ang)
    pad = jnp.zeros((t, LANES - MLA_ROPE), F32)
    return (jnp.concatenate([cos, cos, pad], axis=-1), jnp.concatenate([-sin, sin, pad], axis=-1))


def _pad_lanes(g, n):
    return jnp.pad(g.reshape(1, -1), ((0, 0), (0, n - g.shape[-1])))


def _s5_perm():
    p = jnp.arange(S5_CHUNK * LANES)
    tau, g, i = p // LANES, (p % LANES) // S5_GROUP, p % S5_GROUP
    q = g * (S5_CHUNK * S5_GROUP) + tau * S5_GROUP + i
    return (q[:, None] == p[None, :]).astype(BF)


def kernel(x, c, ctx, c_ctx, ada_w, ada_b, norm1_g, norm2_g, w_in, mla_q_lora_g, mla_kv_lora_g, mla_w_uq, mla_w_ukv, mla_qn_nope_g, mla_qn_rope_g, mla_kn_nope_g, mla_kn_rope_g, rwkv_conv, rwkv_w0, rwkv_w2, rwkv_a0, rwkv_a2, rwkv_g2, rwkv_k_k, rwkv_k_a, rwkv_r_k, rwkv_ln_g, rwkv_ln_b, s5_lam_re, s5_lam_im, s5_log_dt, s5_b_re, s5_b_im, s5_c_re, s5_c_im, s5_d, s5_glu_w, s5_glu_b, w_branch, w_out, w_mlp1, w_mlp2):
    nb, t, d = x.shape
    tc = ctx.shape[1]
    depth = ada_w.shape[0]
    ctx_row = nb

    cc = jnp.concatenate([c, c_ctx[None, :], jnp.zeros((8 - nb - 1, d), F32)], axis=0)
    mod_all = _ada(cc, ada_w, ada_b)

    bt = lambda z: jnp.swapaxes(z, -1, -2)
    mw, wot, a16 = _s5_setup(s5_lam_re, s5_lam_im, s5_log_dt, bt(s5_b_re), bt(s5_b_im), s5_c_re, s5_c_im)

    cos_t, sin_t = _rope_tables(t)
    cos_c = jnp.concatenate([jnp.ones((tc, MLA_ROPE), F32), jnp.zeros((tc, LANES - MLA_ROPE), F32)], axis=-1)
    sin_c = jnp.zeros((tc, LANES), F32)
    hi = jnp.arange(HEAD_GROUP) // RWKV_HEAD
    ones_bd = (hi[:, None] == hi[None, :]).astype(BF)
    ones_bd128 = ones_bd[:LANES, :LANES]

    tm_t = min(1024, t)
    tm_c = min(1024, nb * tc)
    lat_row = lambda tm: (lambda i: i // (t // tm))
    ctx_rowf = lambda i: ctx_row

    x2 = x.reshape(nb * t, d)
    xc2 = ctx.reshape(nb * tc, d)
    s_zero = jnp.zeros((nb, 2, 1024 // HEAD_GROUP, HEAD_GROUP, HEAD_GROUP), F32)
    h_zero = jnp.zeros((nb, 1024 // S5_GROUP, 2, 2 * S5_STATE), F32)
    perm = _s5_perm()

    for l in range(depth):
        last = l == depth - 1
        mod = mod_all[l].reshape(8, 6, 1, d)
        w = w_in[l]
        w_perm = jnp.concatenate(
            [w[:, 1088:4160], w[:, 4576:5600], w[:, 5600:11744], w[:, 0:1024], w[:, 1024:1088],
             w[:, 4160:4576], jnp.zeros((d, N_IN_PAD - 11744), F32)], axis=1).astype(BF)
        g1 = norm1_g[l].reshape(1, d)
        g2n = norm2_g[l].reshape(1, d)
        w_uq = jnp.pad(mla_w_uq[l].reshape(-1, N_HEADS, MLA_NOPE + MLA_ROPE),
                       ((0, 0), (0, 0), (0, 256 - MLA_NOPE - MLA_ROPE))).reshape(-1, N_HEADS * 256).astype(BF)
        w_ukv = mla_w_ukv[l].astype(BF)
        gq = mla_q_lora_g[l].reshape(1, -1)
        gkv = mla_kv_lora_g[l].reshape(1, -1)
        gqn, gqr = mla_qn_nope_g[l].reshape(1, -1), _pad_lanes(mla_qn_rope_g[l], LANES)
        gkn, gkr = mla_kn_nope_g[l].reshape(1, -1), _pad_lanes(mla_kn_rope_g[l], LANES)
        rp = dict(w0=rwkv_w0[l], w2=rwkv_w2[l].astype(BF), a0=rwkv_a0[l], a2=rwkv_a2[l].astype(BF),
                  k_a=rwkv_k_a[l].reshape(1, -1), r_k=rwkv_r_k[l].reshape(1, -1),
                  ln_g=rwkv_ln_g[l].reshape(1, -1), ln_b=rwkv_ln_b[l].reshape(1, -1),
                  g2_pad=jnp.pad(rwkv_g2[l], ((SM_GD, 512 - SM_GD - rwkv_g2.shape[1]), (0, 0))).astype(BF))
        conv_w = rwkv_conv[l]
        k_k = rwkv_k_k[l].reshape(1, -1)
        glu_w = s5_glu_w[l].astype(BF)
        glu_b = s5_glu_b[l].reshape(1, -1)
        d_skip = s5_d[l].reshape(1, -1)
        wb = w_branch[l].astype(BF)
        wo = w_out[l].astype(BF)
        w1 = w_mlp1[l].astype(BF)
        w2 = w_mlp2[l].astype(BF)

        hc = _inproj(xc2, g1, mod, w_perm, ctx_rowf, tm_c)
        k_c, v_c = _kvprep(hc, nb, tc, gkv, w_ukv, gkn, gkr, cos_c, sin_c, tc)
        rc_c, kc_c, vc_c, kk_c = _rwkv_conv(hc.reshape(nb, tc, -1), conv_w, k_k, ones_bd128)
        yf_c, yb_c, s_ctx = _rwkv_scan(rc_c, kc_c, vc_c, kk_c, hc, rp, s_zero)
        ys_c, h_ctx = _s5_scan(hc, nb, tc, perm, mw, wot, a16, h_zero, l)

        ht = _inproj(x2, g1, mod, w_perm, lat_row(tm_t), tm_t)
        tq = min(512, t)
        q_t = _qprep(ht, nb, t, gq, w_uq, gqn, gqr, cos_t, sin_t, tq)
        k_t, v_t = _kvprep(ht, nb, t, gkv, w_ukv, gkn, gkr, cos_t, sin_t, tq)
        o_a = _attention(q_t, k_t, v_t, (k_c, v_c), min(512, t), min(1024, t)).reshape(nb * t, -1)

        rc_t, kc_t, vc_t, kk_t = _rwkv_conv(ht.reshape(nb, t, -1), conv_w, k_k, ones_bd128)
        yf_t, yb_t, _ = _rwkv_scan(rc_t, kc_t, vc_t, kk_t, ht, rp, s_ctx)
        flat = lambda z: z.reshape(-1, z.shape[-1])
        o_b = _rwkv_post(flat(yf_t), flat(yb_t), flat(rc_t), flat(kc_t), flat(vc_t), ht, rp, ones_bd,
                         min(512, nb * t))

        ys_t, _ = _s5_scan(ht, nb, t, perm, mw, wot, a16, h_ctx, l)
        o_c = _s5_post(ys_t, ht, d_skip, glu_w, glu_b, min(512, nb * t))

        mg = _merge(o_a, o_b, o_c, ht, wb, tm_t)
        x2 = _outproj(mg, wo, x2, mod, lat_row(tm_t), tm_t)
        tm_m = min(512, t)
        x2 = _mlp(x2, g2n, mod, w1, w2, lat_row(tm_m), tm_m)

        if not last:
            q_c = _qprep(hc, nb, tc, gq, w_uq, gqn, gqr, cos_c, sin_c, tc)
            o_a_c = _attention(q_c, k_c, v_c, None, tc, tc).reshape(nb * tc, -1)
            o_b_c = _rwkv_post(flat(yf_c), flat(yb_c), flat(rc_c), flat(kc_c), flat(vc_c), hc, rp, ones_bd,
                               min(512, nb * tc))
            o_c_c = _s5_post(ys_c, hc, d_skip, glu_w, glu_b, min(512, nb * tc))
            mg_c = _merge(o_a_c, o_b_c, o_c_c, hc, wb, tm_c)
            xc2 = _outproj(mg_c, wo, xc2, mod, ctx_rowf, tm_c)
            tm_mc = min(512, nb * tc)
            xc2 = _mlp(xc2, g2n, mod, w1, w2, ctx_rowf, tm_mc)

    return x2.reshape(nb, t, d)
```

```python
import functools
import math

import jax
import jax.numpy as jnp
from jax import lax
from jax.experimental import pallas as pl
from jax.experimental.pallas import tpu as pltpu

F32 = jnp.float32
BF = jnp.bfloat16

GRID_W = 64
N_HEADS = 8
MLA_NOPE = 128
MLA_ROPE = 64
MLA_V = 128
ROPE_THETA = 10000.0
RWKV_HEAD = 64
RWKV_GN_EPS = 64e-5
L2_EPS = 1e-12
S5_GROUP = 16
S5_STATE = 64
NORM_EPS = 1e-6

LANES = 128
VMEM_LIMIT = 48 * 1024 * 1024

COL_R, COL_K, COL_V, COL_U, COL_GATE, COL_CQ, COL_CKV, COL_SMALL = (
    0, 1024, 2048, 3072, 4096, 10240, 10752, 11264)
N_IN_PAD = 11776
SM_KR, SM_WD, SM_AD, SM_GD = 0, 64, 192, 320

RWKV_CHUNK = 64
HEAD_GROUP = 256
S5_CHUNK = 16
ATTN_HEADS_PER_STEP = 4


def _cp(sem, vmem=VMEM_LIMIT):
    return pltpu.CompilerParams(dimension_semantics=sem, vmem_limit_bytes=vmem)


def _mm(a, b):
    return jnp.dot(a.astype(BF), b.astype(BF), preferred_element_type=F32)


def _mm_nt(a, b):
    return lax.dot_general(a.astype(BF), b.astype(BF), (((1,), (1,)), ((), ())),
                           preferred_element_type=F32)


def _mm_tn(a, b):
    return lax.dot_general(a.astype(BF), b.astype(BF), (((0,), (0,)), ((), ())),
                           preferred_element_type=F32)


def _idiv(x, n):
    assert n & (n - 1) == 0, n
    return lax.shift_right_logical(x, n.bit_length() - 1)


def _imod(x, n):
    assert n & (n - 1) == 0, n
    return lax.bitwise_and(x, n - 1)


def _rms(x, g, eps=NORM_EPS):
    return x * lax.rsqrt(jnp.mean(x * x, axis=-1, keepdims=True) + eps) * g


def _softplus(x):
    return jnp.maximum(x, 0.0) + jnp.log(1.0 + jnp.exp(-jnp.abs(x)))


def _ada_kernel(c_ref, w_ref, b_ref, o_ref):
    c = c_ref[...]
    s = c * jax.nn.sigmoid(c)
    o_ref[0] = _mm(s, w_ref[0]) + b_ref[0]


def _ada(cc, ada_w, ada_b):
    n_layer, d, n = ada_w.shape
    tn = 1024
    return pl.pallas_call(
        _ada_kernel,
        grid=(n_layer, n // tn),
        in_specs=[pl.BlockSpec((8, d), lambda l, j: (0, 0)),
                  pl.BlockSpec((1, d, tn), lambda l, j: (l, 0, j)),
                  pl.BlockSpec((1, 1, tn), lambda l, j: (l, 0, j))],
        out_specs=pl.BlockSpec((1, 8, tn), lambda l, j: (l, 0, j)),
        out_shape=jax.ShapeDtypeStruct((n_layer, 8, n), F32),
        compiler_params=_cp(("arbitrary", "arbitrary")),
    )(cc, ada_w, ada_b.reshape(n_layer, 1, n))


def _inproj_kernel(x_ref, g_ref, sh_ref, sc_ref, w_ref, o_ref, xn_ref):
    @pl.when(pl.program_id(1) == 0)
    def _():
        y = _rms(x_ref[...], g_ref[...])
        xn_ref[...] = (y * (1.0 + sc_ref[0, 0]) + sh_ref[0, 0]).astype(BF)

    o_ref[...] = jnp.dot(xn_ref[...], w_ref[...], preferred_element_type=F32).astype(o_ref.dtype)


def _inproj(x2, g, mod, w, row_fn, tm):
    m, d = x2.shape
    n = w.shape[1]
    tn = 512
    return pl.pallas_call(
        _inproj_kernel,
        grid=(m // tm, n // tn),
        in_specs=[pl.BlockSpec((tm, d), lambda i, j: (i, 0)),
                  pl.BlockSpec((1, d), lambda i, j: (0, 0)),
                  pl.BlockSpec((1, 1, 1, d), lambda i, j: (row_fn(i), 0, 0, 0)),
                  pl.BlockSpec((1, 1, 1, d), lambda i, j: (row_fn(i), 1, 0, 0)),
                  pl.BlockSpec((d, tn), lambda i, j: (0, j))],
        out_specs=pl.BlockSpec((tm, tn), lambda i, j: (i, j)),
        out_shape=jax.ShapeDtypeStruct((m, n), BF),
        scratch_shapes=[pltpu.VMEM((tm, d), BF)],
        compiler_params=_cp(("arbitrary", "arbitrary")),
    )(x2, g, mod, mod, w)


def _rope_rotate(x, cos_t, sin_t):
    lane = lax.broadcasted_iota(jnp.int32, x.shape, 1)
    sw = jnp.where(lane < 32, pltpu.roll(x, 96, 1), pltpu.roll(x, 32, 1))
    return x * cos_t + sw * sin_t


def _qprep_kernel(scale, cq_ref, g_ref, w_ref, gn_ref, gr_ref, cos_ref, sin_ref, q_ref):
    xn = _rms(cq_ref[...].astype(F32), g_ref[...])
    q = _mm(xn, w_ref[...])
    cos_t, sin_t = cos_ref[...], sin_ref[...]
    for h in range(N_HEADS):
        nope = q[:, h * 256:h * 256 + 128]
        rp = q[:, h * 256 + 128:h * 256 + 256]
        nope = _rms(nope, gn_ref[...])
        rp = rp * lax.rsqrt(jnp.sum(rp * rp, axis=-1, keepdims=True) * (1.0 / MLA_ROPE) + NORM_EPS) * gr_ref[...]
        rp = _rope_rotate(rp, cos_t, sin_t)
        q_ref[0, h, :, 0:128] = (nope * scale).astype(BF)
        q_ref[0, h, :, 128:256] = (rp * scale).astype(BF)


def _qprep(h2, nb, t, g, w, gn, gr, cos_t, sin_t, tm):
    nt = t // tm
    scale = math.log2(math.e) / math.sqrt(MLA_NOPE + MLA_ROPE)
    return pl.pallas_call(
        functools.partial(_qprep_kernel, scale),
        grid=(nb * nt,),
        in_specs=[pl.BlockSpec((tm, 512), lambda i: (i, COL_CQ // 512)),
                  pl.BlockSpec((1, 512), lambda i: (0, 0)),
                  pl.BlockSpec((512, N_HEADS * 256), lambda i: (0, 0)),
                  pl.BlockSpec((1, 128), lambda i: (0, 0)),
                  pl.BlockSpec((1, 128), lambda i: (0, 0)),
                  pl.BlockSpec((tm, 128), lambda i: (i % nt, 0)),
                  pl.BlockSpec((tm, 128), lambda i: (i % nt, 0))],
        out_specs=pl.BlockSpec((1, N_HEADS, tm, 256), lambda i: (i // nt, 0, i % nt, 0)),
        out_shape=jax.ShapeDtypeStruct((nb, N_HEADS, t, 256), BF),
        compiler_params=_cp(("arbitrary",)),
    )(h2, g, w, gn, gr, cos_t, sin_t)


def _kvprep_kernel(ckv_ref, sm_ref, g_ref, w_ref, gn_ref, gr_ref, cos_ref, sin_ref, k_ref, v_ref):
    xn = _rms(ckv_ref[...].astype(F32), g_ref[...])
    kv = _mm(xn, w_ref[...])
    sm = sm_ref[...].astype(F32)
    lane = lax.broadcasted_iota(jnp.int32, sm.shape, 1)
    kr = jnp.where(lane < MLA_ROPE, sm, 0.0)
    kr = kr * lax.rsqrt(jnp.sum(kr * kr, axis=-1, keepdims=True) * (1.0 / MLA_ROPE) + NORM_EPS) * gr_ref[...]
    kr = _rope_rotate(kr, cos_ref[...], sin_ref[...]).astype(BF)
    ones_col = jnp.where(lane == 0, 1.0, 0.0).astype(BF)
    for h in range(N_HEADS):
        kn = _rms(kv[:, h * 256:h * 256 + 128], gn_ref[...])
        k_ref[0, h, :, 0:128] = kn.astype(BF)
        k_ref[0, h, :, 128:256] = kr
        v_ref[0, h, :, 0:128] = kv[:, h * 256 + 128:h * 256 + 256].astype(BF)
        v_ref[0, h, :, 128:256] = ones_col


def _kvprep(h2, nb, t, g, w, gn, gr, cos_t, sin_t, tm):
    nt = t // tm
    return pl.pallas_call(
        _kvprep_kernel,
        grid=(nb * nt,),
        in_specs=[pl.BlockSpec((tm, 512), lambda i: (i, COL_CKV // 512)),
                  pl.BlockSpec((tm, 128), lambda i: (i, COL_SMALL // 128)),
                  pl.BlockSpec((1, 512), lambda i: (0, 0)),
                  pl.BlockSpec((512, N_HEADS * 256), lambda i: (0, 0)),
                  pl.BlockSpec((1, 128), lambda i: (0, 0)),
                  pl.BlockSpec((1, 128), lambda i: (0, 0)),
                  pl.BlockSpec((tm, 128), lambda i: (i % nt, 0)),
                  pl.BlockSpec((tm, 128), lambda i: (i % nt, 0))],
        out_specs=[pl.BlockSpec((1, N_HEADS, tm, 256), lambda i: (i // nt, 0, i % nt, 0)),
                   pl.BlockSpec((1, N_HEADS, tm, 256), lambda i: (i // nt, 0, i % nt, 0))],
        out_shape=[jax.ShapeDtypeStruct((nb, N_HEADS, t, 256), BF),
                   jax.ShapeDtypeStruct((nb, N_HEADS, t, 256), BF)],
        compiler_params=_cp(("arbitrary",)),
    )(h2, h2, g, w, gn, gr, cos_t, sin_t)


def _attn_kernel(has_extra, nk, *refs):
    if has_extra:
        q_ref, k_ref, v_ref, kc_ref, vc_ref, o_ref, m_ref, acc_ref = refs
    else:
        q_ref, k_ref, v_ref, o_ref, m_ref, acc_ref = refs
    j = pl.program_id(3)

    @pl.when(j == 0)
    def _():
        m_ref[...] = jnp.full(m_ref.shape, -1e30, F32)
        acc_ref[...] = jnp.zeros(acc_ref.shape, F32)

    heads = range(ATTN_HEADS_PER_STEP)

    def step(k_r, v_r):
        s = [lax.dot_general(q_ref[0, h], k_r[0, h], (((1,), (1,)), ((), ())),
                             preferred_element_type=F32) for h in heads]
        m_prev = [m_ref[h] for h in heads]
        m_new = [jnp.maximum(m_prev[h], jnp.max(s[h], axis=1, keepdims=True)) for h in heads]
        alpha = [jnp.exp2(m_prev[h] - m_new[h]) for h in heads]
        p = [jnp.exp2((s[h] - m_new[h]).astype(BF)) for h in heads]
        pv = [jnp.dot(p[h], v_r[0, h], preferred_element_type=F32) for h in heads]
        for h in heads:
            acc_ref[h] = alpha[h] * acc_ref[h] + pv[h]
            m_ref[h] = m_new[h]

    step(k_ref, v_ref)

    @pl.when(j == nk - 1)
    def _():
        if has_extra:
            step(kc_ref, vc_ref)
        for h in heads:
            acc = acc_ref[h]
            o_ref[0, :, h * MLA_V:(h + 1) * MLA_V] = (
                acc[:, :MLA_V] / acc[:, MLA_V:MLA_V + 1]).astype(o_ref.dtype)


def _attention(q, k, v, extra, bq, bk):
    nb, nh, t, _ = q.shape
    tk = k.shape[2]
    nk = tk // bk
    hb = ATTN_HEADS_PER_STEP
    in_specs = [pl.BlockSpec((1, hb, bq, 256), lambda b, h, i, j: (b, h, i, 0)),
                pl.BlockSpec((1, hb, bk, 256), lambda b, h, i, j: (b, h, j, 0)),
                pl.BlockSpec((1, hb, bk, 256), lambda b, h, i, j: (b, h, j, 0))]
    args = [q, k, v]
    if extra is not None:
        kc, vc = extra
        tc = kc.shape[2]
        in_specs += [pl.BlockSpec((1, hb, tc, 256), lambda b, h, i, j: (b, h, 0, 0)),
                     pl.BlockSpec((1, hb, tc, 256), lambda b, h, i, j: (b, h, 0, 0))]
        args += [kc, vc]
    return pl.pallas_call(
        functools.partial(_attn_kernel, extra is not None, nk),
        grid=(nb, nh // hb, t // bq, nk),
        in_specs=in_specs,
        out_specs=pl.BlockSpec((1, bq, hb * MLA_V), lambda b, h, i, j: (b, i, h)),
        out_shape=jax.ShapeDtypeStruct((nb, t, nh * MLA_V), BF),
        scratch_shapes=[pltpu.VMEM((hb, bq, 1), F32), pltpu.VMEM((hb, bq, 2 * MLA_V), F32)],
        compiler_params=_cp(("arbitrary",) * 4),
    )(*args)


def _conv_kernel(r_ref, k_ref, v_ref, wr_ref, wk_ref, wv_ref, kk_w_ref, ones_ref,
                 ro_ref, ko_ref, vo_ref, kko_ref):
    t = r_ref.shape[1]
    row = lax.broadcasted_iota(jnp.int32, (t, LANES), 0)

    def conv(x_ref, w_ref):
        x = x_ref[0].astype(F32)
        xm = jnp.where(row == 0, 0.0, pltpu.roll(x, 1, 0))
        xp = jnp.where(row == t - 1, 0.0, pltpu.roll(x, t - 1, 0))
        w = w_ref[...]
        return xm * w[0:1] + x * w[1:2] + xp * w[2:3]

    ro_ref[0] = conv(r_ref, wr_ref).astype(BF)
    vo_ref[0] = conv(v_ref, wv_ref).astype(BF)
    k = conv(k_ref, wk_ref)
    ko_ref[0] = k.astype(BF)
    kk = k * kk_w_ref[...]
    ss = _mm(kk * kk, ones_ref[...])
    kko_ref[0] = (kk * lax.rsqrt(ss + L2_EPS)).astype(BF)


def _rwkv_conv(h3, conv_w, k_k, ones_bd):
    nb, t, _ = h3.shape
    nj = 1024 // LANES
    blk = lambda off: pl.BlockSpec((1, t, LANES), lambda b, j: (b, 0, off // LANES + j))
    wblk = lambda off: pl.BlockSpec((3, LANES), lambda b, j: (0, off // LANES + j))
    oblk = pl.BlockSpec((1, t, LANES), lambda b, j: (b, 0, j))
    osh = jax.ShapeDtypeStruct((nb, t, 1024), BF)
    return pl.pallas_call(
        _conv_kernel,
        grid=(nb, nj),
        in_specs=[blk(COL_R), blk(COL_K), blk(COL_V), wblk(0), wblk(1024), wblk(2048),
                  pl.BlockSpec((1, LANES), lambda b, j: (0, j)),
                  pl.BlockSpec((LANES, LANES), lambda b, j: (0, 0))],
        out_specs=[oblk, oblk, oblk, oblk],
        out_shape=[osh, osh, osh, osh],
        compiler_params=_cp(("arbitrary", "arbitrary")),
    )(h3, h3, h3, conv_w, conv_w, conv_w, k_k, ones_bd)


def _bd_rows(x, bdmask):
    reps = HEAD_GROUP // x.shape[0]
    return jnp.where(bdmask, jnp.concatenate([x] * reps, axis=0), 0.0)


def _scan_prep(d, r_ref, k_ref, v_ref, kk_ref, sm_ref, w0_ref, w2_ref, a0_ref, a2_ref, ka_ref, tri):
    L = RWKV_CHUNK
    r = r_ref[0].astype(F32)
    k = k_ref[0].astype(F32)
    v = v_ref[0].astype(F32)
    kk = kk_ref[0].astype(F32)
    sm = sm_ref[...].astype(F32)
    wd = sm[:, SM_WD + 64 * d:SM_WD + 64 * d + 64]
    ad = sm[:, SM_AD + 64 * d:SM_AD + 64 * d + 64]
    w_log = -_softplus(-(w0_ref[d:d + 1] + _mm(jnp.tanh(wd), w2_ref[d]))) - 0.5
    lw = -jnp.exp(w_log)
    a = jax.nn.sigmoid(a0_ref[d:d + 1] + _mm(ad, a2_ref[d]))
    b = kk * a
    krep = k * (1.0 + (a - 1.0) * ka_ref[...])
    cl = jnp.dot(tri, lw, precision=lax.Precision.HIGHEST, preferred_element_type=F32)
    cl_last = cl[0:1] if d == 1 else cl[L - 1:L]
    e_neg = jnp.exp(-cl)
    e_last = jnp.exp(cl_last - cl)
    at = -(kk * jnp.exp(cl - lw))
    rt = r * jnp.exp(cl)
    bt = b * e_neg
    kt = krep * e_neg
    bh = b * e_last
    kh = krep * e_last
    return dict(at=at, rt=rt, bt=bt, kt=kt, bh=bh, kh=kh, v=v, p_last=jnp.exp(cl_last))


def _scan_kernel(nc, rf, kf, vf, kkf, smf, rb, kb, vb, kkb, smb, w0, w2, a0, a2, ka, s_in,
                 yf, yb, s_out, s_ref):
    c = pl.program_id(1)

    @pl.when(c == 0)
    def _():
        s_ref[...] = s_in[0]

    L = RWKV_CHUNK
    row2 = lax.broadcasted_iota(jnp.int32, (HEAD_GROUP, HEAD_GROUP), 0)
    col2 = lax.broadcasted_iota(jnp.int32, (HEAD_GROUP, HEAD_GROUP), 1)
    bdmask = _idiv(row2, RWKV_HEAD) == _idiv(col2, RWKV_HEAD)
    tt = lax.broadcasted_iota(jnp.int32, (L, HEAD_GROUP), 0)
    ss = _imod(lax.broadcasted_iota(jnp.int32, (L, HEAD_GROUP), 1), L)
    rl = lax.broadcasted_iota(jnp.int32, (L, L), 0)
    cl = lax.broadcasted_iota(jnp.int32, (L, L), 1)
    blk = tuple(_idiv(ss, n) == _idiv(tt, n) for n in (8, 16, 32, 64))
    eye = jnp.where(ss == tt, 1.0, 0.0)
    strict = (ss < tt, ss > tt)
    incl = (ss <= tt, ss >= tt)
    prep = (_scan_prep(0, rf, kf, vf, kkf, smf, w0, w2, a0, a2, ka, (rl >= cl).astype(F32)),
            _scan_prep(1, rb, kb, vb, kkb, smb, w0, w2, a0, a2, ka, (rl <= cl).astype(F32)))
    y_refs = (yf, yb)

    chains = [(d, gi) for d in range(2) for gi in range(1024 // HEAD_GROUP)]
    n = len(chains)
    bd = lambda x: _bd_rows(x, bdmask)
    cat = jnp.concatenate
    g_ = lambda name: [prep[d][name][:, gi * HEAD_GROUP:(gi + 1) * HEAD_GROUP] for d, gi in chains]
    at, rt, bt, kt, bh, kh, v = (g_(nm) for nm in ("at", "rt", "bt", "kt", "bh", "kh", "v"))
    ac = [_mm_nt(cat([at[i], rt[i]], axis=0), cat([bd(bt[i]), bd(kt[i])], axis=0)) for i in range(n)]
    a_ab = [jnp.where(strict[chains[i][0]], ac[i][:L, :HEAD_GROUP], 0.0) for i in range(n)]
    a_ak = [jnp.where(strict[chains[i][0]], ac[i][:L, HEAD_GROUP:], 0.0) for i in range(n)]
    c_b = [jnp.where(incl[chains[i][0]], ac[i][L:, :HEAD_GROUP], 0.0) for i in range(n)]
    c_k = [jnp.where(incl[chains[i][0]], ac[i][L:, HEAD_GROUP:], 0.0) for i in range(n)]
    av = [_mm(a_ak[i], bd(v[i])) for i in range(n)]
    a8 = [jnp.where(blk[0], a_ab[i], 0.0) for i in range(n)]
    p2 = [_mm(a8[i], bd(a8[i])) for i in range(n)]
    tm = [eye + a8[i] for i in range(n)]
    pt = [_mm(cat([p2[i], tm[i]], axis=0), bd(p2[i])) for i in range(n)]
    tm = [tm[i] + pt[i][L:] for i in range(n)]
    tm = [tm[i] + _mm(tm[i], bd(pt[i][:L])) for i in range(n)]
    for lvl in range(3):
        off_mask = blk[lvl + 1] & ~blk[lvl]
        z = [_mm(jnp.where(off_mask, a_ab[i], 0.0), bd(tm[i])) for i in range(n)]
        tm = [tm[i] + _mm(tm[i], bd(z[i])) for i in range(n)]
    wu = [_mm(tm[i], cat([bd(at[i]), bd(av[i])], axis=1)) for i in range(n)]
    w = [wu[i][:, :HEAD_GROUP] for i in range(n)]
    uv = [wu[i][:, HEAD_GROUP:] for i in range(n)]
    q = [rt[i] + _mm(c_b[i], bd(w[i])) for i in range(n)]
    y0 = [_mm(cat([c_b[i], c_k[i]], axis=1), cat([bd(uv[i]), bd(v[i])], axis=0)) for i in range(n)]
    zeros = jnp.zeros((L, HEAD_GROUP), F32)
    gh = [_mm_tn(cat([cat([w[i], uv[i]], axis=1), cat([zeros, v[i]], axis=1)], axis=0),
                 cat([bh[i], kh[i]], axis=0)) for i in range(n)]
    for i, (d, gi) in enumerate(chains):
        sl = slice(gi * HEAD_GROUP, (gi + 1) * HEAD_GROUP)
        s0 = s_ref[d, gi]
        y_refs[d][0, :, sl] = y0[i] + _mm_nt(q[i], s0)
        g = jnp.where(bdmask, gh[i][:HEAD_GROUP], 0.0)
        hh = jnp.where(bdmask, gh[i][HEAD_GROUP:], 0.0)
        s_ref[d, gi] = s0 * prep[d]["p_last"][:, sl] + _mm(s0, g) + hh

    @pl.when(c == nc - 1)
    def _():
        s_out[0] = s_ref[...]


def _rwkv_scan(rc, kc, vc, kk, h2, p, s_in):
    nb, t, _ = rc.shape
    L = RWKV_CHUNK
    nc = t // L
    fw = lambda b, c: (b, c, 0)
    bw = lambda b, c: (b, nc - 1 - c, 0)
    seq = lambda im: pl.BlockSpec((1, L, 1024), im)
    smf = pl.BlockSpec((L, 512), lambda b, c: (b * nc + c, COL_SMALL // 512))
    smb = pl.BlockSpec((L, 512), lambda b, c: (b * nc + nc - 1 - c, COL_SMALL // 512))
    full = lambda shape: pl.BlockSpec(shape, lambda b, c: (0,) * len(shape))
    st = pl.BlockSpec((1, 2, 4, HEAD_GROUP, HEAD_GROUP), lambda b, c: (b, 0, 0, 0, 0))
    ysh = jax.ShapeDtypeStruct((nb, t, 1024), F32)
    return pl.pallas_call(
        functools.partial(_scan_kernel, nc),
        grid=(nb, nc),
        in_specs=[seq(fw), seq(fw), seq(fw), seq(fw), smf, seq(bw), seq(bw), seq(bw), seq(bw), smb,
                  full((2, 1024)), full((2, 64, 1024)), full((2, 1024)), full((2, 64, 1024)),
                  full((1, 1024)), st],
        out_specs=[seq(fw), seq(bw), st],
        out_shape=[ysh, ysh, jax.ShapeDtypeStruct(s_in.shape, F32)],
        scratch_shapes=[pltpu.VMEM((2, 4, HEAD_GROUP, HEAD_GROUP), F32)],
        compiler_params=_cp(("arbitrary", "arbitrary")),
    )(rc, kc, vc, kk, h2, rc, kc, vc, kk, h2, p["w0"], p["w2"], p["a0"], p["a2"], p["k_a"], s_in)


def _rpost_kernel(yf_ref, yb_ref, r_ref, k_ref, v_ref, sm_ref, a0_ref, a2_ref, ka_ref, rk_ref,
                  lng_ref, lnb_ref, g2_ref, ones_ref, o_ref):
    y = yf_ref[...] + yb_ref[...]
    r = r_ref[...].astype(F32)
    k = k_ref[...].astype(F32)
    v = v_ref[...].astype(F32)
    sm = sm_ref[...].astype(F32)
    a_sum = 0.0
    for d in range(2):
        ad = sm[:, SM_AD + 64 * d:SM_AD + 64 * d + 64]
        a_sum = a_sum + jax.nn.sigmoid(a0_ref[d:d + 1] + _mm(ad, a2_ref[d]))
    kb = k * (1.0 + (0.5 * a_sum - 1.0) * ka_ref[...])
    rkb = r * kb * rk_ref[...]
    gate = _mm(jax.nn.sigmoid(sm), g2_ref[...])
    ones = ones_ref[...]
    inv_n = 1.0 / RWKV_HEAD
    for gi in range(1024 // HEAD_GROUP):
        sl = slice(gi * HEAD_GROUP, (gi + 1) * HEAD_GROUP)
        yg = y[:, sl]
        mu = _mm(yg, ones) * inv_n
        dlt = yg - mu
        var = _mm(dlt * dlt, ones) * inv_n
        yn = dlt * lax.rsqrt(var + RWKV_GN_EPS) * lng_ref[:, sl] + lnb_ref[:, sl]
        bonus = _mm(rkb[:, sl], ones) * v[:, sl]
        o_ref[:, sl] = ((yn + bonus) * gate[:, sl]).astype(o_ref.dtype)


def _rwkv_post(yf, yb, rc, kc, vc, h2, p, ones_bd, tm):
    m = yf.shape[0]
    row = lambda i: (i, 0)
    full = lambda shape: pl.BlockSpec(shape, lambda i: (0,) * len(shape))
    seq = pl.BlockSpec((tm, 1024), row)
    return pl.pallas_call(
        _rpost_kernel,
        grid=(m // tm,),
        in_specs=[seq, seq, seq, seq, seq,
                  pl.BlockSpec((tm, 512), lambda i: (i, COL_SMALL // 512)),
                  full((2, 1024)), full((2, 64, 1024)), full((1, 1024)), full((1, 1024)),
                  full((1, 1024)), full((1, 1024)), full((512, 1024)),
                  full((HEAD_GROUP, HEAD_GROUP))],
        out_specs=seq,
        out_shape=jax.ShapeDtypeStruct((m, 1024), BF),
        compiler_params=_cp(("arbitrary",)),
    )(yf, yb, rc, kc, vc, h2, p["a0"], p["a2"], p["k_a"], p["r_k"], p["ln_g"], p["ln_b"],
      p["g2_pad"], ones_bd)


def _cmul(x, y):
    return x[0] * y[0] - x[1] * y[1], x[0] * y[1] + x[1] * y[0]


def _s5setup_kernel(lre_ref, lim_ref, ldt_ref, btr_ref, bti_ref, cr_ref, ci_ref,
                    mw_ref, wot_ref, a16_ref):
    n = S5_CHUNK
    half = n // 2
    rt = _idiv(lax.broadcasted_iota(jnp.int32, (n * S5_GROUP, n * S5_GROUP), 0), S5_GROUP)
    ct = _idiv(lax.broadcasted_iota(jnp.int32, (n * S5_GROUP, n * S5_GROUP), 1), S5_GROUP)
    for d in range(2):
        lr = lre_ref[0, d, 0]
        li = lim_ref[0, d, 0]
        dt = jnp.exp(ldt_ref[0, d, 0])

        def cexp(mult):
            mag = jnp.exp(mult * lr * dt)
            ang = mult * li * dt
            return mag * jnp.cos(ang), mag * jnp.sin(ang)

        a1 = cexp(1.0)
        den = lr * lr + li * li
        q_re = ((a1[0] - 1.0) * lr + a1[1] * li) / den
        q_im = (a1[1] * lr - (a1[0] - 1.0) * li) / den
        bb = (q_re * btr_ref[0, d, 0] - q_im * bti_ref[0, d, 0],
              q_re * bti_ref[0, d, 0] + q_im * btr_ref[0, d, 0])
        cc = (cr_ref[0, d, 0], ci_ref[0, d, 0])
        pw = {0: (jnp.ones_like(lr), jnp.zeros_like(lr)), 1: a1, -1: cexp(-1.0)}
        for m in range(2, n + 1):
            pw[m] = _cmul(pw[m - 1], pw[1])
        for m in range(2, half + 1):
            pw[-m] = _cmul(pw[-m + 1], pw[-1])

        def rows(base, efn):
            parts = [_cmul(base, pw[efn(t)]) for t in range(n)]
            return (jnp.concatenate([z[0] for z in parts], axis=0),
                    jnp.concatenate([z[1] for z in parts], axis=0))

        if d == 0:
            e_pow, f_pow = (lambda t: half - t), (lambda t: t - half)
            wi_pow, wo_pow = (lambda t: n - 1 - t), (lambda t: t + 1)
            mask = ct >= rt
        else:
            e_pow, f_pow = (lambda t: t - half), (lambda t: half - t)
            wi_pow, wo_pow = (lambda t: t), (lambda t: n - t)
            mask = ct <= rt
        e = rows(bb, e_pow)
        f = rows(cc, f_pow)
        mmat = lax.dot_general(jnp.concatenate([e[0], -e[1]], axis=1),
                               jnp.concatenate([f[0], f[1]], axis=1),
                               (((1,), (1,)), ((), ())),
                               precision=lax.Precision.HIGHEST, preferred_element_type=F32)
        mmat = jnp.where(mask, mmat, 0.0)
        wi = rows(bb, wi_pow)
        wo = rows(cc, wo_pow)
        mw_ref[0, d, 0] = jnp.concatenate([mmat, wi[0], wi[1]], axis=1).astype(BF)
        wot_ref[0, d, 0] = jnp.concatenate([wo[0], -wo[1]], axis=1).astype(BF)
        a16_ref[0, d, 0] = jnp.concatenate([pw[n][0], pw[n][1]], axis=1)


def _s5_setup(lam_re, lam_im, log_dt, bt_re, bt_im, c_re, c_im):
    nl, _, ng, _ = lam_re.shape
    n2 = S5_CHUNK * S5_GROUP
    vec = pl.BlockSpec((1, 2, 1, 1, S5_STATE), lambda l, g: (l, 0, g, 0, 0))
    sc = pl.BlockSpec((1, 2, 1, 1, 1), lambda l, g: (l, 0, g, 0, 0))
    mat = pl.BlockSpec((1, 2, 1, S5_GROUP, S5_STATE), lambda l, g: (l, 0, g, 0, 0))
    return pl.pallas_call(
        _s5setup_kernel,
        grid=(nl, ng),
        in_specs=[vec, vec, sc, mat, mat, mat, mat],
        out_specs=[pl.BlockSpec((1, 2, 1, n2, n2 + 2 * S5_STATE), lambda l, g: (l, 0, g, 0, 0)),
                   pl.BlockSpec((1, 2, 1, n2, 2 * S5_STATE), lambda l, g: (l, 0, g, 0, 0)),
                   pl.BlockSpec((1, 2, 1, 1, 2 * S5_STATE), lambda l, g: (l, 0, g, 0, 0))],
        out_shape=[jax.ShapeDtypeStruct((nl, 2, ng, n2, n2 + 2 * S5_STATE), BF),
                   jax.ShapeDtypeStruct((nl, 2, ng, n2, 2 * S5_STATE), BF),
                   jax.ShapeDtypeStruct((nl, 2, ng, 1, 2 * S5_STATE), F32)],
        compiler_params=_cp(("arbitrary", "arbitrary")),
    )(lam_re.reshape(nl, 2, ng, 1, S5_STATE), lam_im.reshape(nl, 2, ng, 1, S5_STATE),
      log_dt.reshape(nl, 2, ng, 1, 1), bt_re, bt_im, c_re, c_im)


S5_GROUPS_PER_STEP = LANES // S5_GROUP


def _s5_kernel(bb, nc, u_ref, perm_ref, mw_ref, wot_ref, a16_ref, h0_ref, y_ref, hfin_ref,
               uf_ref, yf_ref):
    n = S5_CHUNK
    n2 = S5_CHUNK * S5_GROUP
    ns = S5_STATE
    rows = bb * nc
    perm = perm_ref[...]
    uf_ref[...] = u_ref[...].astype(F32)
    ucat = jnp.concatenate([uf_ref[pl.ds(tau, rows, stride=n), :] for tau in range(n)],
                           axis=1).astype(BF)
    xcat = jnp.dot(ucat, perm, preferred_element_type=F32).astype(BF)
    ridx = lax.broadcasted_iota(jnp.int32, (rows, 2 * ns), 0)
    cpos = _imod(ridx, nc)
    lane = lax.broadcasted_iota(jnp.int32, (1, 2 * ns), 1)

    def coef_pair(coef):
        sw = pltpu.roll(coef, ns, 1)
        return jnp.where(lane < ns, coef, sw), jnp.where(lane < ns, -sw, coef)

    def cmul_rows(coef, val):
        c_re, c_im = coef_pair(coef)
        return val * c_re + pltpu.roll(val, ns, 1) * c_im

    chains = [(g, d) for g in range(S5_GROUPS_PER_STEP) for d in range(2)]
    rng = range(len(chains))
    first = (0, nc - 1)
    last = (nc - 1, 0)
    yz = [jnp.dot(xcat[:, g * n2:(g + 1) * n2], mw_ref[0, d, g], preferred_element_type=F32)
          for g, d in chains]
    h0r = []
    for g, d in chains:
        z = jnp.zeros((rows, 2 * ns), F32)
        for bi in range(bb):
            z = jnp.where(ridx == bi * nc + first[d], h0_ref[bi, g, d:d + 1], z)
        h0r.append(z)
    ap = [a16_ref[0, d, g] for g, d in chains]
    e = [yz[i][:, n2:] + cmul_rows(ap[i], h0r[i]) for i in rng]
    sh = 1
    while sh < nc:
        shifted = [jnp.where(cpos >= sh, pltpu.roll(e[i], sh, 0), 0.0) if chains[i][1] == 0 else
                   jnp.where(cpos < nc - sh, pltpu.roll(e[i], rows - sh, 0), 0.0) for i in rng]
        e = [e[i] + cmul_rows(ap[i], shifted[i]) for i in rng]
        ap = [cmul_rows(ap[i], ap[i]) for i in rng]
        sh *= 2
    hs = [jnp.where(cpos >= 1, pltpu.roll(e[i], 1, 0), h0r[i]) if chains[i][1] == 0 else
          jnp.where(cpos < nc - 1, pltpu.roll(e[i], rows - 1, 0), h0r[i]) for i in rng]
    yd = [yz[i][:, :n2] + _mm_nt(hs[i], wot_ref[0, d, g]) for i, (g, d) in enumerate(chains)]
    ycat = jnp.concatenate([yd[2 * g] + yd[2 * g + 1] for g in range(S5_GROUPS_PER_STEP)], axis=1)
    ytok = lax.dot_general(ycat.astype(BF), perm, (((1,), (1,)), ((), ())), preferred_element_type=F32)
    for tau in range(n):
        yf_ref[pl.ds(tau, rows, stride=n), :] = ytok[:, tau * LANES:(tau + 1) * LANES]
    y_ref[...] = yf_ref[...].astype(y_ref.dtype)
    for i, (g, d) in enumerate(chains):
        for bi in range(bb):
            hfin_ref[bi, g, d:d + 1] = e[i][bi * nc + last[d]:bi * nc + last[d] + 1]


def _s5_scan(h2, nb, t, bb, perm, mw, wot, a16, h0, layer):
    n = S5_CHUNK
    nc = t // n
    gs = S5_GROUPS_PER_STEP
    ng = 1024 // S5_GROUP
    n2 = S5_CHUNK * S5_GROUP
    par = lambda shape: pl.BlockSpec((1, 2, gs) + shape, lambda s, b: (layer, 0, s, 0, 0))
    st = pl.BlockSpec((bb, gs, 2, 2 * S5_STATE), lambda s, b: (b, s, 0, 0))
    return pl.pallas_call(
        functools.partial(_s5_kernel, bb, nc),
        grid=(ng // gs, nb // bb),
        in_specs=[pl.BlockSpec((bb * t, LANES), lambda s, b: (b, COL_U // LANES + s)),
                  pl.BlockSpec(perm.shape, lambda s, b: (0, 0)),
                  par((n2, n2 + 2 * S5_STATE)), par((n2, 2 * S5_STATE)), par((1, 2 * S5_STATE)), st],
        out_specs=[pl.BlockSpec((bb * t, LANES), lambda s, b: (b, s)), st],
        out_shape=[jax.ShapeDtypeStruct((nb * t, 1024), BF),
                   jax.ShapeDtypeStruct((nb, ng, 2, 2 * S5_STATE), F32)],
        scratch_shapes=[pltpu.VMEM((bb * t, LANES), F32), pltpu.VMEM((bb * t, LANES), F32)],
        compiler_params=_cp(("arbitrary", "arbitrary")),
    )(h2, perm, mw, wot, a16, h0)


def _s5post_kernel(y_ref, u_ref, d_ref, w_ref, b_ref, o_ref):
    yy = y_ref[...].astype(F32) + d_ref[...] * u_ref[...].astype(F32)
    z = jax.nn.gelu(yy)
    o_ref[...] = (z * jax.nn.sigmoid(_mm(z, w_ref[...]) + b_ref[...])).astype(o_ref.dtype)


def _s5_post(y2, h2, d_skip, glu_w, glu_b, tm):
    m = y2.shape[0]
    full = lambda shape: pl.BlockSpec(shape, lambda i: (0,) * len(shape))
    return pl.pallas_call(
        _s5post_kernel,
        grid=(m // tm,),
        in_specs=[pl.BlockSpec((tm, 1024), lambda i: (i, 0)),
                  pl.BlockSpec((tm, 1024), lambda i: (i, COL_U // 1024)),
                  full((1, 1024)), full((1024, 1024)), full((1, 1024))],
        out_specs=pl.BlockSpec((tm, 1024), lambda i: (i, 0)),
        out_shape=jax.ShapeDtypeStruct((m, 1024), BF),
        compiler_params=_cp(("arbitrary",)),
    )(y2, h2, d_skip, glu_w, glu_b)


def _merge_kernel(oa_ref, ob_ref, oc_ref, g0_ref, g1_ref, g2_ref, w_ref, o_ref):
    acc = None
    for o_r, g_r, n in ((oa_ref, g0_ref, 0), (ob_ref, g1_ref, 1), (oc_ref, g2_ref, 2)):
        pr = jnp.dot(o_r[...], w_ref[n], preferred_element_type=F32)
        term = jax.nn.sigmoid(g_r[...].astype(F32)) * pr
        acc = term if acc is None else acc + term
    o_ref[...] = acc.astype(o_ref.dtype)


def _merge(oa, ob, oc, h2, w_branch, tm):
    m = oa.shape[0]
    d = w_branch.shape[2]
    tn = 512
    seq = pl.BlockSpec((tm, 1024), lambda i, j: (i, 0))
    gate = lambda n: pl.BlockSpec((tm, tn), lambda i, j: (i, (COL_GATE + n * d) // tn + j))
    return pl.pallas_call(
        _merge_kernel,
        grid=(m // tm, d // tn),
        in_specs=[seq, seq, seq, gate(0), gate(1), gate(2),
                  pl.BlockSpec((3, 1024, tn), lambda i, j: (0, 0, j))],
        out_specs=pl.BlockSpec((tm, tn), lambda i, j: (i, j)),
        out_shape=jax.ShapeDtypeStruct((m, d), BF),
        compiler_params=_cp(("arbitrary", "arbitrary")),
    )(oa, ob, oc, h2, h2, h2, w_branch)


def _outproj_kernel(m_ref, w_ref, x_ref, gt_ref, o_ref):
    o_ref[...] = x_ref[...] + gt_ref[0, 0] * jnp.dot(m_ref[...], w_ref[...], preferred_element_type=F32)


def _outproj(mg, w_out, x2, mod, row_fn, tm):
    m, d = x2.shape
    tn = 512
    return pl.pallas_call(
        _outproj_kernel,
        grid=(m // tm, d // tn),
        in_specs=[pl.BlockSpec((tm, d), lambda i, j: (i, 0)),
                  pl.BlockSpec((d, tn), lambda i, j: (0, j)),
                  pl.BlockSpec((tm, tn), lambda i, j: (i, j)),
                  pl.BlockSpec((1, 1, 1, tn), lambda i, j: (row_fn(i), 2, 0, j))],
        out_specs=pl.BlockSpec((tm, tn), lambda i, j: (i, j)),
        out_shape=jax.ShapeDtypeStruct((m, d), F32),
        compiler_params=_cp(("arbitrary", "arbitrary")),
    )(mg, w_out, x2, mod)


def _mlp_kernel(nf, x_ref, g_ref, sh_ref, sc_ref, gt_ref, w1_ref, w2_ref, o_ref, xn_ref, acc_ref):
    f = pl.program_id(1)

    @pl.when(f == 0)
    def _():
        y = _rms(x_ref[...], g_ref[...])
        xn_ref[...] = (y * (1.0 + sc_ref[0, 0]) + sh_ref[0, 0]).astype(BF)
        acc_ref[...] = jnp.zeros(acc_ref.shape, F32)

    hmid = jnp.maximum(jnp.dot(xn_ref[...], w1_ref[...], preferred_element_type=F32), 0.0)
    acc_ref[...] += jnp.dot((hmid * hmid).astype(BF), w2_ref[...], preferred_element_type=F32)

    @pl.when(f == nf - 1)
    def _():
        o_ref[...] = x_ref[...] + gt_ref[0, 0] * acc_ref[...]


def _mlp(x2, g, mod, w1, w2, row_fn, tm):
    m, d = x2.shape
    dff = w1.shape[1]
    tf = 512
    nf = dff // tf
    modspec = lambda k: pl.BlockSpec((1, 1, 1, d), lambda i, f: (row_fn(i), k, 0, 0))
    return pl.pallas_call(
        functools.partial(_mlp_kernel, nf),
        grid=(m // tm, nf),
        in_specs=[pl.BlockSpec((tm, d), lambda i, f: (i, 0)),
                  pl.BlockSpec((1, d), lambda i, f: (0, 0)),
                  modspec(3), modspec(4), modspec(5),
                  pl.BlockSpec((d, tf), lambda i, f: (0, f)),
                  pl.BlockSpec((tf, d), lambda i, f: (f, 0))],
        out_specs=pl.BlockSpec((tm, d), lambda i, f: (i, 0)),
        out_shape=jax.ShapeDtypeStruct((m, d), F32),
        scratch_shapes=[pltpu.VMEM((tm, d), BF), pltpu.VMEM((tm, d), F32)],
        compiler_params=_cp(("arbitrary", "arbitrary")),
    )(x2, g, mod, mod, mod, w1, w2)


def _rope_tables(t):
    rows = t // GRID_W
    row = jnp.repeat(jnp.arange(rows, dtype=F32), GRID_W)
    col = jnp.tile(jnp.arange(GRID_W, dtype=F32), rows)
    n_freq = MLA_ROPE // 4
    inv = ROPE_THETA ** (-jnp.arange(n_freq, dtype=F32) / n_freq)
    ang = jnp.concatenate([row[:, None] * inv, col[:, None] * inv], axis=-1)
    cos, sin = jnp.cos(ang), jnp.sin(ang)
    pad = jnp.zeros((t, LANES - MLA_ROPE), F32)
    return (jnp.concatenate([cos, cos, pad], axis=-1), jnp.concatenate([-sin, sin, pad], axis=-1))


def _pad_lanes(g, n):
    return jnp.pad(g.reshape(1, -1), ((0, 0), (0, n - g.shape[-1])))


def _s5_perm():
    p = jnp.arange(S5_CHUNK * LANES)
    tau, g, i = p // LANES, (p % LANES) // S5_GROUP, p % S5_GROUP
    q = g * (S5_CHUNK * S5_GROUP) + tau * S5_GROUP + i
    return (q[:, None] == p[None, :]).astype(BF)


def kernel(x, c, ctx, c_ctx, ada_w, ada_b, norm1_g, norm2_g, w_in, mla_q_lora_g, mla_kv_lora_g, mla_w_uq, mla_w_ukv, mla_qn_nope_g, mla_qn_rope_g, mla_kn_nope_g, mla_kn_rope_g, rwkv_conv, rwkv_w0, rwkv_w2, rwkv_a0, rwkv_a2, rwkv_g2, rwkv_k_k, rwkv_k_a, rwkv_r_k, rwkv_ln_g, rwkv_ln_b, s5_lam_re, s5_lam_im, s5_log_dt, s5_b_re, s5_b_im, s5_c_re, s5_c_im, s5_d, s5_glu_w, s5_glu_b, w_branch, w_out, w_mlp1, w_mlp2):
    nb, t, d = x.shape
    tc = ctx.shape[1]
    depth = ada_w.shape[0]
    ctx_row = nb

    cc = jnp.concatenate([c, c_ctx[None, :], jnp.zeros((8 - nb - 1, d), F32)], axis=0)
    mod_all = _ada(cc, ada_w, ada_b)

    bt = lambda z: jnp.swapaxes(z, -1, -2)
    mw, wot, a16 = _s5_setup(s5_lam_re, s5_lam_im, s5_log_dt, bt(s5_b_re), bt(s5_b_im), s5_c_re, s5_c_im)

    cos_t, sin_t = _rope_tables(t)
    cos_c = jnp.concatenate([jnp.ones((tc, MLA_ROPE), F32), jnp.zeros((tc, LANES - MLA_ROPE), F32)], axis=-1)
    sin_c = jnp.zeros((tc, LANES), F32)
    hi = jnp.arange(HEAD_GROUP) // RWKV_HEAD
    ones_bd = (hi[:, None] == hi[None, :]).astype(BF)
    ones_bd128 = ones_bd[:LANES, :LANES]

    tm_t = min(1024, t)
    tm_c = min(1024, nb * tc)
    lat_row = lambda tm: (lambda i: i // (t // tm))
    ctx_rowf = lambda i: ctx_row

    x2 = x.reshape(nb * t, d)
    xc2 = ctx.reshape(nb * tc, d)
    s_zero = jnp.zeros((nb, 2, 1024 // HEAD_GROUP, HEAD_GROUP, HEAD_GROUP), F32)
    h_zero = jnp.zeros((nb, 1024 // S5_GROUP, 2, 2 * S5_STATE), F32)
    perm = _s5_perm()

    for l in range(depth):
        last = l == depth - 1
        mod = mod_all[l].reshape(8, 6, 1, d)
        w = w_in[l].astype(BF)
        w_perm = jnp.concatenate(
            [w[:, 1088:4160], w[:, 4576:5600], w[:, 5600:11744], w[:, 0:1024], w[:, 1024:1088],
             w[:, 4160:4576], jnp.zeros((d, N_IN_PAD - 11744), BF)], axis=1)
        g1 = norm1_g[l].reshape(1, d)
        g2n = norm2_g[l].reshape(1, d)
        w_uq = jnp.pad(mla_w_uq[l].reshape(-1, N_HEADS, MLA_NOPE + MLA_ROPE),
                       ((0, 0), (0, 0), (0, 256 - MLA_NOPE - MLA_ROPE))).reshape(-1, N_HEADS * 256).astype(BF)
        w_ukv = mla_w_ukv[l].astype(BF)
        gq = mla_q_lora_g[l].reshape(1, -1)
        gkv = mla_kv_lora_g[l].reshape(1, -1)
        gqn, gqr = mla_qn_nope_g[l].reshape(1, -1), _pad_lanes(mla_qn_rope_g[l], LANES)
        gkn, gkr = mla_kn_nope_g[l].reshape(1, -1), _pad_lanes(mla_kn_rope_g[l], LANES)
        rp = dict(w0=rwkv_w0[l], w2=rwkv_w2[l].astype(BF), a0=rwkv_a0[l], a2=rwkv_a2[l].astype(BF),
                  k_a=rwkv_k_a[l].reshape(1, -1), r_k=rwkv_r_k[l].reshape(1, -1),
                  ln_g=rwkv_ln_g[l].reshape(1, -1), ln_b=rwkv_ln_b[l].reshape(1, -1),
                  g2_pad=jnp.pad(rwkv_g2[l], ((SM_GD, 512 - SM_GD - rwkv_g2.shape[1]), (0, 0))).astype(BF))
        conv_w = rwkv_conv[l]
        k_k = rwkv_k_k[l].reshape(1, -1)
        glu_w = s5_glu_w[l].astype(BF)
        glu_b = s5_glu_b[l].reshape(1, -1)
        d_skip = s5_d[l].reshape(1, -1)
        wb = w_branch[l].astype(BF)
        wo = w_out[l].astype(BF)
        w1 = w_mlp1[l].astype(BF)
        w2 = w_mlp2[l].astype(BF)

        hc = _inproj(xc2, g1, mod, w_perm, ctx_rowf, tm_c)
        k_c, v_c = _kvprep(hc, nb, tc, gkv, w_ukv, gkn, gkr, cos_c, sin_c, tc)
        rc_c, kc_c, vc_c, kk_c = _rwkv_conv(hc.reshape(nb, tc, -1), conv_w, k_k, ones_bd128)
        yf_c, yb_c, s_ctx = _rwkv_scan(rc_c, kc_c, vc_c, kk_c, hc, rp, s_zero)
        ys_c, h_ctx = _s5_scan(hc, nb, tc, nb, perm, mw, wot, a16, h_zero, l)

        ht = _inproj(x2, g1, mod, w_perm, lat_row(tm_t), tm_t)
        tq = min(512, t)
        q_t = _qprep(ht, nb, t, gq, w_uq, gqn, gqr, cos_t, sin_t, tq)
        k_t, v_t = _kvprep(ht, nb, t, gkv, w_ukv, gkn, gkr, cos_t, sin_t, tq)
        o_a = _attention(q_t, k_t, v_t, (k_c, v_c), min(512, t), min(1024, t)).reshape(nb * t, -1)

        rc_t, kc_t, vc_t, kk_t = _rwkv_conv(ht.reshape(nb, t, -1), conv_w, k_k, ones_bd128)
        yf_t, yb_t, _ = _rwkv_scan(rc_t, kc_t, vc_t, kk_t, ht, rp, s_ctx)
        flat = lambda z: z.reshape(-1, z.shape[-1])
        o_b = _rwkv_post(flat(yf_t), flat(yb_t), flat(rc_t), flat(kc_t), flat(vc_t), ht, rp, ones_bd,
                         min(512, nb * t))

        ys_t, _ = _s5_scan(ht, nb, t, 1, perm, mw, wot, a16, h_ctx, l)
        o_c = _s5_post(ys_t, ht, d_skip, glu_w, glu_b, min(512, nb * t))

        mg = _merge(o_a, o_b, o_c, ht, wb, tm_t)
        x2 = _outproj(mg, wo, x2, mod, lat_row(tm_t), tm_t)
        tm_m = min(512, t)
        x2 = _mlp(x2, g2n, mod, w1, w2, lat_row(tm_m), tm_m)

        if not last:
            q_c = _qprep(hc, nb, tc, gq, w_uq, gqn, gqr, cos_c, sin_c, tc)
            o_a_c = _attention(q_c, k_c, v_c, None, tc, tc).reshape(nb * tc, -1)
            o_b_c = _rwkv_post(flat(yf_c), flat(yb_c), flat(rc_c), flat(kc_c), flat(vc_c), hc, rp, ones_bd,
                               min(512, nb * tc))
            o_c_c = _s5_post(ys_c, hc, d_skip, glu_w, glu_b, min(512, nb * tc))
            mg_c = _merge(o_a_c, o_b_c, o_c_c, hc, wb, tm_c)
            xc2 = _outproj(mg_c, wo, xc2, mod, ctx_rowf, tm_c)
            tm_mc = min(512, nb * tc)
            xc2 = _mlp(xc2, g2n, mod, w1, w2, ctx_rowf, tm_mc)

    return x2.reshape(nb, t, d)
```

```python
import functools
import math

import jax
import jax.numpy as jnp
from jax import lax
from jax.experimental import pallas as pl
from jax.experimental.pallas import tpu as pltpu

F32 = jnp.float32
BF = jnp.bfloat16

GRID_W = 64
N_HEADS = 8
MLA_NOPE = 128
MLA_ROPE = 64
MLA_V = 128
ROPE_THETA = 10000.0
RWKV_HEAD = 64
RWKV_GN_EPS = 64e-5
L2_EPS = 1e-12
S5_GROUP = 16
S5_STATE = 64
NORM_EPS = 1e-6

LANES = 128
VMEM_LIMIT = 48 * 1024 * 1024

COL_R, COL_K, COL_V, COL_U, COL_GATE, COL_CQ, COL_CKV, COL_SMALL = (
    0, 1024, 2048, 3072, 4096, 10240, 10752, 11264)
N_IN_PAD = 11776
SM_KR, SM_WD, SM_AD, SM_GD = 0, 64, 192, 320

RWKV_CHUNK = 64
HEAD_GROUP = 256
S5_CHUNK = 16
S5_SETUP_GROUPS = 8
ATTN_HEADS_PER_STEP = 4


def _cp(sem, vmem=VMEM_LIMIT):
    return pltpu.CompilerParams(dimension_semantics=sem, vmem_limit_bytes=vmem)


def _mm(a, b):
    return jnp.dot(a.astype(BF), b.astype(BF), preferred_element_type=F32)


def _mm_nt(a, b):
    return lax.dot_general(a.astype(BF), b.astype(BF), (((1,), (1,)), ((), ())),
                           preferred_element_type=F32)


def _mm_tn(a, b):
    return lax.dot_general(a.astype(BF), b.astype(BF), (((0,), (0,)), ((), ())),
                           preferred_element_type=F32)


def _idiv(x, n):
    assert n & (n - 1) == 0, n
    return lax.shift_right_logical(x, n.bit_length() - 1)


def _imod(x, n):
    assert n & (n - 1) == 0, n
    return lax.bitwise_and(x, n - 1)


def _rms(x, g, eps=NORM_EPS):
    return x * lax.rsqrt(jnp.mean(x * x, axis=-1, keepdims=True) + eps) * g


def _softplus(x):
    return jnp.maximum(x, 0.0) + jnp.log(1.0 + jnp.exp(-jnp.abs(x)))


def _ada_kernel(c_ref, w_ref, b_ref, o_ref):
    c = c_ref[...]
    s = c * jax.nn.sigmoid(c)
    o_ref[0] = _mm(s, w_ref[0]) + b_ref[0]


def _ada(cc, ada_w, ada_b):
    n_layer, d, n = ada_w.shape
    tn = 1024
    return pl.pallas_call(
        _ada_kernel,
        grid=(n_layer, n // tn),
        in_specs=[pl.BlockSpec((8, d), lambda l, j: (0, 0)),
                  pl.BlockSpec((1, d, tn), lambda l, j: (l, 0, j)),
                  pl.BlockSpec((1, 1, tn), lambda l, j: (l, 0, j))],
        out_specs=pl.BlockSpec((1, 8, tn), lambda l, j: (l, 0, j)),
        out_shape=jax.ShapeDtypeStruct((n_layer, 8, n), F32),
        compiler_params=_cp(("arbitrary", "arbitrary")),
    )(cc, ada_w, ada_b.reshape(n_layer, 1, n))


W_IN_SEGMENTS = ((1088, 3072), (4576, 1024), (5600, 6144), (0, 1024), (1024, 64), (4160, 416))


def _wperm_kernel(w_ref, o_ref):
    off = 0
    for src, n in W_IN_SEGMENTS:
        o_ref[0, :, off:off + n] = w_ref[0, :, src:src + n].astype(BF)
        off += n
    o_ref[0, :, off:] = jnp.zeros((o_ref.shape[1], o_ref.shape[2] - off), BF)


def _permute_w_in(w_in):
    nl, d, n_in = w_in.shape
    tr = 256
    return pl.pallas_call(
        _wperm_kernel,
        grid=(nl, d // tr),
        in_specs=[pl.BlockSpec((1, tr, n_in), lambda l, i: (l, i, 0))],
        out_specs=pl.BlockSpec((1, tr, N_IN_PAD), lambda l, i: (l, i, 0)),
        out_shape=jax.ShapeDtypeStruct((nl, d, N_IN_PAD), BF),
        compiler_params=_cp(("arbitrary", "arbitrary")),
    )(w_in)


def _inproj_kernel(x_ref, g_ref, sh_ref, sc_ref, w_ref, o_ref, xn_ref):
    @pl.when(pl.program_id(1) == 0)
    def _():
        y = _rms(x_ref[...], g_ref[...])
        xn_ref[...] = (y * (1.0 + sc_ref[0, 0]) + sh_ref[0, 0]).astype(BF)

    o_ref[...] = jnp.dot(xn_ref[...], w_ref[...], preferred_element_type=F32).astype(o_ref.dtype)


def _inproj(x2, g, mod, w, row_fn, tm):
    m, d = x2.shape
    n = w.shape[1]
    tn = 512
    return pl.pallas_call(
        _inproj_kernel,
        grid=(m // tm, n // tn),
        in_specs=[pl.BlockSpec((tm, d), lambda i, j: (i, 0)),
                  pl.BlockSpec((1, d), lambda i, j: (0, 0)),
                  pl.BlockSpec((1, 1, 1, d), lambda i, j: (row_fn(i), 0, 0, 0)),
                  pl.BlockSpec((1, 1, 1, d), lambda i, j: (row_fn(i), 1, 0, 0)),
                  pl.BlockSpec((d, tn), lambda i, j: (0, j))],
        out_specs=pl.BlockSpec((tm, tn), lambda i, j: (i, j)),
        out_shape=jax.ShapeDtypeStruct((m, n), BF),
        scratch_shapes=[pltpu.VMEM((tm, d), BF)],
        compiler_params=_cp(("arbitrary", "arbitrary")),
    )(x2, g, mod, mod, w)


def _rope_rotate(x, cos_t, sin_t):
    lane = lax.broadcasted_iota(jnp.int32, x.shape, 1)
    sw = jnp.where(lane < 32, pltpu.roll(x, 96, 1), pltpu.roll(x, 32, 1))
    return x * cos_t + sw * sin_t


def _qprep_kernel(scale, cq_ref, g_ref, w_ref, gn_ref, gr_ref, cos_ref, sin_ref, q_ref):
    xn = _rms(cq_ref[...].astype(F32), g_ref[...])
    q = _mm(xn, w_ref[...])
    cos_t, sin_t = cos_ref[...], sin_ref[...]
    for h in range(N_HEADS):
        nope = q[:, h * 256:h * 256 + 128]
        rp = q[:, h * 256 + 128:h * 256 + 256]
        nope = _rms(nope, gn_ref[...])
        rp = rp * lax.rsqrt(jnp.sum(rp * rp, axis=-1, keepdims=True) * (1.0 / MLA_ROPE) + NORM_EPS) * gr_ref[...]
        rp = _rope_rotate(rp, cos_t, sin_t)
        q_ref[0, h, :, 0:128] = (nope * scale).astype(BF)
        q_ref[0, h, :, 128:256] = (rp * scale).astype(BF)


def _qprep(h2, nb, t, g, w, gn, gr, cos_t, sin_t, tm):
    nt = t // tm
    scale = math.log2(math.e) / math.sqrt(MLA_NOPE + MLA_ROPE)
    return pl.pallas_call(
        functools.partial(_qprep_kernel, scale),
        grid=(nb * nt,),
        in_specs=[pl.BlockSpec((tm, 512), lambda i: (i, COL_CQ // 512)),
                  pl.BlockSpec((1, 512), lambda i: (0, 0)),
                  pl.BlockSpec((512, N_HEADS * 256), lambda i: (0, 0)),
                  pl.BlockSpec((1, 128), lambda i: (0, 0)),
                  pl.BlockSpec((1, 128), lambda i: (0, 0)),
                  pl.BlockSpec((tm, 128), lambda i: (i % nt, 0)),
                  pl.BlockSpec((tm, 128), lambda i: (i % nt, 0))],
        out_specs=pl.BlockSpec((1, N_HEADS, tm, 256), lambda i: (i // nt, 0, i % nt, 0)),
        out_shape=jax.ShapeDtypeStruct((nb, N_HEADS, t, 256), BF),
        compiler_params=_cp(("arbitrary",)),
    )(h2, g, w, gn, gr, cos_t, sin_t)


def _kvprep_kernel(ckv_ref, sm_ref, g_ref, w_ref, gn_ref, gr_ref, cos_ref, sin_ref, k_ref, v_ref):
    xn = _rms(ckv_ref[...].astype(F32), g_ref[...])
    kv = _mm(xn, w_ref[...])
    sm = sm_ref[...].astype(F32)
    lane = lax.broadcasted_iota(jnp.int32, sm.shape, 1)
    kr = jnp.where(lane < MLA_ROPE, sm, 0.0)
    kr = kr * lax.rsqrt(jnp.sum(kr * kr, axis=-1, keepdims=True) * (1.0 / MLA_ROPE) + NORM_EPS) * gr_ref[...]
    kr = _rope_rotate(kr, cos_ref[...], sin_ref[...]).astype(BF)
    ones_col = jnp.where(lane == 0, 1.0, 0.0).astype(BF)
    for h in range(N_HEADS):
        kn = _rms(kv[:, h * 256:h * 256 + 128], gn_ref[...])
        k_ref[0, h, :, 0:128] = kn.astype(BF)
        k_ref[0, h, :, 128:256] = kr
        v_ref[0, h, :, 0:128] = kv[:, h * 256 + 128:h * 256 + 256].astype(BF)
        v_ref[0, h, :, 128:256] = ones_col


def _kvprep(h2, nb, t, g, w, gn, gr, cos_t, sin_t, tm):
    nt = t // tm
    return pl.pallas_call(
        _kvprep_kernel,
        grid=(nb * nt,),
        in_specs=[pl.BlockSpec((tm, 512), lambda i: (i, COL_CKV // 512)),
                  pl.BlockSpec((tm, 128), lambda i: (i, COL_SMALL // 128)),
                  pl.BlockSpec((1, 512), lambda i: (0, 0)),
                  pl.BlockSpec((512, N_HEADS * 256), lambda i: (0, 0)),
                  pl.BlockSpec((1, 128), lambda i: (0, 0)),
                  pl.BlockSpec((1, 128), lambda i: (0, 0)),
                  pl.BlockSpec((tm, 128), lambda i: (i % nt, 0)),
                  pl.BlockSpec((tm, 128), lambda i: (i % nt, 0))],
        out_specs=[pl.BlockSpec((1, N_HEADS, tm, 256), lambda i: (i // nt, 0, i % nt, 0)),
                   pl.BlockSpec((1, N_HEADS, tm, 256), lambda i: (i // nt, 0, i % nt, 0))],
        out_shape=[jax.ShapeDtypeStruct((nb, N_HEADS, t, 256), BF),
                   jax.ShapeDtypeStruct((nb, N_HEADS, t, 256), BF)],
        compiler_params=_cp(("arbitrary",)),
    )(h2, h2, g, w, gn, gr, cos_t, sin_t)


def _attn_kernel(has_extra, nk, *refs):
    if has_extra:
        q_ref, k_ref, v_ref, kc_ref, vc_ref, o_ref, m_ref, acc_ref = refs
    else:
        q_ref, k_ref, v_ref, o_ref, m_ref, acc_ref = refs
    j = pl.program_id(3)

    @pl.when(j == 0)
    def _():
        m_ref[...] = jnp.full(m_ref.shape, -1e30, F32)
        acc_ref[...] = jnp.zeros(acc_ref.shape, F32)

    heads = range(ATTN_HEADS_PER_STEP)

    def step(k_r, v_r):
        s = [lax.dot_general(q_ref[0, h], k_r[0, h], (((1,), (1,)), ((), ())),
                             preferred_element_type=F32) for h in heads]
        m_prev = [m_ref[h] for h in heads]
        m_new = [jnp.maximum(m_prev[h], jnp.max(s[h], axis=1, keepdims=True)) for h in heads]
        alpha = [jnp.exp2(m_prev[h] - m_new[h]) for h in heads]
        p = [jnp.exp2((s[h] - m_new[h]).astype(BF)) for h in heads]
        pv = [jnp.dot(p[h], v_r[0, h], preferred_element_type=F32) for h in heads]
        for h in heads:
            acc_ref[h] = alpha[h] * acc_ref[h] + pv[h]
            m_ref[h] = m_new[h]

    step(k_ref, v_ref)

    @pl.when(j == nk - 1)
    def _():
        if has_extra:
            step(kc_ref, vc_ref)
        for h in heads:
            acc = acc_ref[h]
            o_ref[0, :, h * MLA_V:(h + 1) * MLA_V] = (
                acc[:, :MLA_V] / acc[:, MLA_V:MLA_V + 1]).astype(o_ref.dtype)


def _attention(q, k, v, extra, bq, bk):
    nb, nh, t, _ = q.shape
    tk = k.shape[2]
    nk = tk // bk
    hb = ATTN_HEADS_PER_STEP
    in_specs = [pl.BlockSpec((1, hb, bq, 256), lambda b, h, i, j: (b, h, i, 0)),
                pl.BlockSpec((1, hb, bk, 256), lambda b, h, i, j: (b, h, j, 0)),
                pl.BlockSpec((1, hb, bk, 256), lambda b, h, i, j: (b, h, j, 0))]
    args = [q, k, v]
    if extra is not None:
        kc, vc = extra
        tc = kc.shape[2]
        in_specs += [pl.BlockSpec((1, hb, tc, 256), lambda b, h, i, j: (b, h, 0, 0)),
                     pl.BlockSpec((1, hb, tc, 256), lambda b, h, i, j: (b, h, 0, 0))]
        args += [kc, vc]
    return pl.pallas_call(
        functools.partial(_attn_kernel, extra is not None, nk),
        grid=(nb, nh // hb, t // bq, nk),
        in_specs=in_specs,
        out_specs=pl.BlockSpec((1, bq, hb * MLA_V), lambda b, h, i, j: (b, i, h)),
        out_shape=jax.ShapeDtypeStruct((nb, t, nh * MLA_V), BF),
        scratch_shapes=[pltpu.VMEM((hb, bq, 1), F32), pltpu.VMEM((hb, bq, 2 * MLA_V), F32)],
        compiler_params=_cp(("arbitrary",) * 4),
    )(*args)


def _conv_kernel(r_ref, k_ref, v_ref, wr_ref, wk_ref, wv_ref, kk_w_ref, ones_ref,
                 ro_ref, ko_ref, vo_ref, kko_ref):
    t = r_ref.shape[1]
    row = lax.broadcasted_iota(jnp.int32, (t, LANES), 0)

    def conv(x_ref, w_ref):
        x = x_ref[0].astype(F32)
        xm = jnp.where(row == 0, 0.0, pltpu.roll(x, 1, 0))
        xp = jnp.where(row == t - 1, 0.0, pltpu.roll(x, t - 1, 0))
        w = w_ref[...]
        return xm * w[0:1] + x * w[1:2] + xp * w[2:3]

    ro_ref[0] = conv(r_ref, wr_ref).astype(BF)
    vo_ref[0] = conv(v_ref, wv_ref).astype(BF)
    k = conv(k_ref, wk_ref)
    ko_ref[0] = k.astype(BF)
    kk = k * kk_w_ref[...]
    ss = _mm(kk * kk, ones_ref[...])
    kko_ref[0] = (kk * lax.rsqrt(ss + L2_EPS)).astype(BF)


def _rwkv_conv(h3, conv_w, k_k, ones_bd):
    nb, t, _ = h3.shape
    nj = 1024 // LANES
    blk = lambda off: pl.BlockSpec((1, t, LANES), lambda b, j: (b, 0, off // LANES + j))
    wblk = lambda off: pl.BlockSpec((3, LANES), lambda b, j: (0, off // LANES + j))
    oblk = pl.BlockSpec((1, t, LANES), lambda b, j: (b, 0, j))
    osh = jax.ShapeDtypeStruct((nb, t, 1024), BF)
    return pl.pallas_call(
        _conv_kernel,
        grid=(nb, nj),
        in_specs=[blk(COL_R), blk(COL_K), blk(COL_V), wblk(0), wblk(1024), wblk(2048),
                  pl.BlockSpec((1, LANES), lambda b, j: (0, j)),
                  pl.BlockSpec((LANES, LANES), lambda b, j: (0, 0))],
        out_specs=[oblk, oblk, oblk, oblk],
        out_shape=[osh, osh, osh, osh],
        compiler_params=_cp(("arbitrary", "arbitrary")),
    )(h3, h3, h3, conv_w, conv_w, conv_w, k_k, ones_bd)


def _bd_rows(x, bdmask):
    reps = HEAD_GROUP // x.shape[0]
    return jnp.where(bdmask, jnp.concatenate([x] * reps, axis=0), 0.0)


def _scan_prep(d, r_ref, k_ref, v_ref, kk_ref, sm_ref, w0_ref, w2_ref, a0_ref, a2_ref, ka_ref, tri):
    L = RWKV_CHUNK
    r = r_ref[0].astype(F32)
    k = k_ref[0].astype(F32)
    v = v_ref[0].astype(F32)
    kk = kk_ref[0].astype(F32)
    sm = sm_ref[...].astype(F32)
    wd = sm[:, SM_WD + 64 * d:SM_WD + 64 * d + 64]
    ad = sm[:, SM_AD + 64 * d:SM_AD + 64 * d + 64]
    w_log = -_softplus(-(w0_ref[d:d + 1] + _mm(jnp.tanh(wd), w2_ref[d]))) - 0.5
    lw = -jnp.exp(w_log)
    a = jax.nn.sigmoid(a0_ref[d:d + 1] + _mm(ad, a2_ref[d]))
    b = kk * a
    krep = k * (1.0 + (a - 1.0) * ka_ref[...])
    cl = jnp.dot(tri, lw, precision=lax.Precision.HIGHEST, preferred_element_type=F32)
    cl_last = cl[0:1] if d == 1 else cl[L - 1:L]
    e_neg = jnp.exp(-cl)
    e_last = jnp.exp(cl_last - cl)
    at = -(kk * jnp.exp(cl - lw))
    rt = r * jnp.exp(cl)
    bt = b * e_neg
    kt = krep * e_neg
    bh = b * e_last
    kh = krep * e_last
    return dict(at=at, rt=rt, bt=bt, kt=kt, bh=bh, kh=kh, v=v, p_last=jnp.exp(cl_last))


def _scan_kernel(nc, rf, kf, vf, kkf, smf, rb, kb, vb, kkb, smb, w0, w2, a0, a2, ka, s_in,
                 yf, yb, s_out, s_ref):
    c = pl.program_id(1)

    @pl.when(c == 0)
    def _():
        s_ref[...] = s_in[0]

    L = RWKV_CHUNK
    row2 = lax.broadcasted_iota(jnp.int32, (HEAD_GROUP, HEAD_GROUP), 0)
    col2 = lax.broadcasted_iota(jnp.int32, (HEAD_GROUP, HEAD_GROUP), 1)
    bdmask = _idiv(row2, RWKV_HEAD) == _idiv(col2, RWKV_HEAD)
    tt = lax.broadcasted_iota(jnp.int32, (L, HEAD_GROUP), 0)
    ss = _imod(lax.broadcasted_iota(jnp.int32, (L, HEAD_GROUP), 1), L)
    rl = lax.broadcasted_iota(jnp.int32, (L, L), 0)
    cl = lax.broadcasted_iota(jnp.int32, (L, L), 1)
    blk = tuple(_idiv(ss, n) == _idiv(tt, n) for n in (8, 16, 32, 64))
    eye = jnp.where(ss == tt, 1.0, 0.0)
    strict = (ss < tt, ss > tt)
    incl = (ss <= tt, ss >= tt)
    prep = (_scan_prep(0, rf, kf, vf, kkf, smf, w0, w2, a0, a2, ka, (rl >= cl).astype(F32)),
            _scan_prep(1, rb, kb, vb, kkb, smb, w0, w2, a0, a2, ka, (rl <= cl).astype(F32)))
    y_refs = (yf, yb)

    chains = [(d, gi) for d in range(2) for gi in range(1024 // HEAD_GROUP)]
    n = len(chains)
    bd = lambda x: _bd_rows(x, bdmask)
    cat = jnp.concatenate
    g_ = lambda name: [prep[d][name][:, gi * HEAD_GROUP:(gi + 1) * HEAD_GROUP] for d, gi in chains]
    at, rt, bt, kt, bh, kh, v = (g_(nm) for nm in ("at", "rt", "bt", "kt", "bh", "kh", "v"))
    ac = [_mm_nt(cat([at[i], rt[i]], axis=0), cat([bd(bt[i]), bd(kt[i])], axis=0)) for i in range(n)]
    a_ab = [jnp.where(strict[chains[i][0]], ac[i][:L, :HEAD_GROUP], 0.0) for i in range(n)]
    a_ak = [jnp.where(strict[chains[i][0]], ac[i][:L, HEAD_GROUP:], 0.0) for i in range(n)]
    c_b = [jnp.where(incl[chains[i][0]], ac[i][L:, :HEAD_GROUP], 0.0) for i in range(n)]
    c_k = [jnp.where(incl[chains[i][0]], ac[i][L:, HEAD_GROUP:], 0.0) for i in range(n)]
    av = [_mm(a_ak[i], bd(v[i])) for i in range(n)]
    a8 = [jnp.where(blk[0], a_ab[i], 0.0) for i in range(n)]
    p2 = [_mm(a8[i], bd(a8[i])) for i in range(n)]
    tm = [eye + a8[i] for i in range(n)]
    pt = [_mm(cat([p2[i], tm[i]], axis=0), bd(p2[i])) for i in range(n)]
    tm = [tm[i] + pt[i][L:] for i in range(n)]
    tm = [tm[i] + _mm(tm[i], bd(pt[i][:L])) for i in range(n)]
    for lvl in range(3):
        off_mask = blk[lvl + 1] & ~blk[lvl]
        z = [_mm(jnp.where(off_mask, a_ab[i], 0.0), bd(tm[i])) for i in range(n)]
        tm = [tm[i] + _mm(tm[i], bd(z[i])) for i in range(n)]
    wu = [_mm(tm[i], cat([bd(at[i]), bd(av[i])], axis=1)) for i in range(n)]
    w = [wu[i][:, :HEAD_GROUP] for i in range(n)]
    uv = [wu[i][:, HEAD_GROUP:] for i in range(n)]
    q = [rt[i] + _mm(c_b[i], bd(w[i])) for i in range(n)]
    y0 = [_mm(cat([c_b[i], c_k[i]], axis=1), cat([bd(uv[i]), bd(v[i])], axis=0)) for i in range(n)]
    zeros = jnp.zeros((L, HEAD_GROUP), F32)
    gh = [_mm_tn(cat([cat([w[i], uv[i]], axis=1), cat([zeros, v[i]], axis=1)], axis=0),
                 cat([bh[i], kh[i]], axis=0)) for i in range(n)]
    for i, (d, gi) in enumerate(chains):
        sl = slice(gi * HEAD_GROUP, (gi + 1) * HEAD_GROUP)
        s0 = s_ref[d, gi]
        y_refs[d][0, :, sl] = y0[i] + _mm_nt(q[i], s0)
        g = jnp.where(bdmask, gh[i][:HEAD_GROUP], 0.0)
        hh = jnp.where(bdmask, gh[i][HEAD_GROUP:], 0.0)
        s_ref[d, gi] = s0 * prep[d]["p_last"][:, sl] + _mm(s0, g) + hh

    @pl.when(c == nc - 1)
    def _():
        s_out[0] = s_ref[...]


def _rwkv_scan(rc, kc, vc, kk, h2, p, s_in):
    nb, t, _ = rc.shape
    L = RWKV_CHUNK
    nc = t // L
    fw = lambda b, c: (b, c, 0)
    bw = lambda b, c: (b, nc - 1 - c, 0)
    seq = lambda im: pl.BlockSpec((1, L, 1024), im)
    smf = pl.BlockSpec((L, 512), lambda b, c: (b * nc + c, COL_SMALL // 512))
    smb = pl.BlockSpec((L, 512), lambda b, c: (b * nc + nc - 1 - c, COL_SMALL // 512))
    full = lambda shape: pl.BlockSpec(shape, lambda b, c: (0,) * len(shape))
    st = pl.BlockSpec((1, 2, 4, HEAD_GROUP, HEAD_GROUP), lambda b, c: (b, 0, 0, 0, 0))
    ysh = jax.ShapeDtypeStruct((nb, t, 1024), F32)
    return pl.pallas_call(
        functools.partial(_scan_kernel, nc),
        grid=(nb, nc),
        in_specs=[seq(fw), seq(fw), seq(fw), seq(fw), smf, seq(bw), seq(bw), seq(bw), seq(bw), smb,
                  full((2, 1024)), full((2, 64, 1024)), full((2, 1024)), full((2, 64, 1024)),
                  full((1, 1024)), st],
        out_specs=[seq(fw), seq(bw), st],
        out_shape=[ysh, ysh, jax.ShapeDtypeStruct(s_in.shape, F32)],
        scratch_shapes=[pltpu.VMEM((2, 4, HEAD_GROUP, HEAD_GROUP), F32)],
        compiler_params=_cp(("arbitrary", "arbitrary")),
    )(rc, kc, vc, kk, h2, rc, kc, vc, kk, h2, p["w0"], p["w2"], p["a0"], p["a2"], p["k_a"], s_in)


def _rpost_kernel(yf_ref, yb_ref, r_ref, k_ref, v_ref, sm_ref, a0_ref, a2_ref, ka_ref, rk_ref,
                  lng_ref, lnb_ref, g2_ref, ones_ref, o_ref):
    y = yf_ref[...] + yb_ref[...]
    r = r_ref[...].astype(F32)
    k = k_ref[...].astype(F32)
    v = v_ref[...].astype(F32)
    sm = sm_ref[...].astype(F32)
    a_sum = 0.0
    for d in range(2):
        ad = sm[:, SM_AD + 64 * d:SM_AD + 64 * d + 64]
        a_sum = a_sum + jax.nn.sigmoid(a0_ref[d:d + 1] + _mm(ad, a2_ref[d]))
    kb = k * (1.0 + (0.5 * a_sum - 1.0) * ka_ref[...])
    rkb = r * kb * rk_ref[...]
    gate = _mm(jax.nn.sigmoid(sm), g2_ref[...])
    ones = ones_ref[...]
    inv_n = 1.0 / RWKV_HEAD
    for gi in range(1024 // HEAD_GROUP):
        sl = slice(gi * HEAD_GROUP, (gi + 1) * HEAD_GROUP)
        yg = y[:, sl]
        mu = _mm(yg, ones) * inv_n
        dlt = yg - mu
        var = _mm(dlt * dlt, ones) * inv_n
        yn = dlt * lax.rsqrt(var + RWKV_GN_EPS) * lng_ref[:, sl] + lnb_ref[:, sl]
        bonus = _mm(rkb[:, sl], ones) * v[:, sl]
        o_ref[:, sl] = ((yn + bonus) * gate[:, sl]).astype(o_ref.dtype)


def _rwkv_post(yf, yb, rc, kc, vc, h2, p, ones_bd, tm):
    m = yf.shape[0]
    row = lambda i: (i, 0)
    full = lambda shape: pl.BlockSpec(shape, lambda i: (0,) * len(shape))
    seq = pl.BlockSpec((tm, 1024), row)
    return pl.pallas_call(
        _rpost_kernel,
        grid=(m // tm,),
        in_specs=[seq, seq, seq, seq, seq,
                  pl.BlockSpec((tm, 512), lambda i: (i, COL_SMALL // 512)),
                  full((2, 1024)), full((2, 64, 1024)), full((1, 1024)), full((1, 1024)),
                  full((1, 1024)), full((1, 1024)), full((512, 1024)),
                  full((HEAD_GROUP, HEAD_GROUP))],
        out_specs=seq,
        out_shape=jax.ShapeDtypeStruct((m, 1024), BF),
        compiler_params=_cp(("arbitrary",)),
    )(yf, yb, rc, kc, vc, h2, p["a0"], p["a2"], p["k_a"], p["r_k"], p["ln_g"], p["ln_b"],
      p["g2_pad"], ones_bd)


def _cmul(x, y):
    return x[0] * y[0] - x[1] * y[1], x[0] * y[1] + x[1] * y[0]


def _s5setup_kernel(lre_ref, lim_ref, ldt_ref, btr_ref, bti_ref, cr_ref, ci_ref,
                    mw_ref, wot_ref, a16_ref):
    n = S5_CHUNK
    half = n // 2
    rt = _idiv(lax.broadcasted_iota(jnp.int32, (n * S5_GROUP, n * S5_GROUP), 0), S5_GROUP)
    ct = _idiv(lax.broadcasted_iota(jnp.int32, (n * S5_GROUP, n * S5_GROUP), 1), S5_GROUP)
    for gi, d in [(gi, d) for gi in range(S5_SETUP_GROUPS) for d in range(2)]:
        lr = lre_ref[0, d, gi]
        li = lim_ref[0, d, gi]
        dt = jnp.exp(ldt_ref[0, d, gi])

        def cexp(mult):
            mag = jnp.exp(mult * lr * dt)
            ang = mult * li * dt
            return mag * jnp.cos(ang), mag * jnp.sin(ang)

        a1 = cexp(1.0)
        den = lr * lr + li * li
        q_re = ((a1[0] - 1.0) * lr + a1[1] * li) / den
        q_im = (a1[1] * lr - (a1[0] - 1.0) * li) / den
        bb = (q_re * btr_ref[0, d, gi] - q_im * bti_ref[0, d, gi],
              q_re * bti_ref[0, d, gi] + q_im * btr_ref[0, d, gi])
        cc = (cr_ref[0, d, gi], ci_ref[0, d, gi])
        pw = {0: (jnp.ones_like(lr), jnp.zeros_like(lr)), 1: a1, -1: cexp(-1.0)}
        for m in range(2, n + 1):
            pw[m] = _cmul(pw[m - 1], pw[1])
        for m in range(2, half + 1):
            pw[-m] = _cmul(pw[-m + 1], pw[-1])

        def rows(base, efn):
            parts = [_cmul(base, pw[efn(t)]) for t in range(n)]
            return (jnp.concatenate([z[0] for z in parts], axis=0),
                    jnp.concatenate([z[1] for z in parts], axis=0))

        if d == 0:
            e_pow, f_pow = (lambda t: half - t), (lambda t: t - half)
            wi_pow, wo_pow = (lambda t: n - 1 - t), (lambda t: t + 1)
            mask = ct >= rt
        else:
            e_pow, f_pow = (lambda t: t - half), (lambda t: half - t)
            wi_pow, wo_pow = (lambda t: t), (lambda t: n - t)
            mask = ct <= rt
        e = rows(bb, e_pow)
        f = rows(cc, f_pow)
        mmat = lax.dot_general(jnp.concatenate([e[0], -e[1]], axis=1),
                               jnp.concatenate([f[0], f[1]], axis=1),
                               (((1,), (1,)), ((), ())),
                               precision=lax.Precision.HIGHEST, preferred_element_type=F32)
        mmat = jnp.where(mask, mmat, 0.0)
        wi = rows(bb, wi_pow)
        wo = rows(cc, wo_pow)
        mw_ref[0, d, gi] = jnp.concatenate([mmat, wi[0], wi[1]], axis=1).astype(BF)
        wot_ref[0, d, gi] = jnp.concatenate([wo[0], -wo[1]], axis=1).astype(BF)
        a16_ref[0, d, gi] = jnp.concatenate([pw[n][0], pw[n][1]], axis=1)


def _s5_setup(lam_re, lam_im, log_dt, bt_re, bt_im, c_re, c_im):
    nl, _, ng, _ = lam_re.shape
    n2 = S5_CHUNK * S5_GROUP
    gs = S5_SETUP_GROUPS
    vec = pl.BlockSpec((1, 2, gs,1, S5_STATE), lambda l, g: (l, 0, g, 0, 0))
    sc = pl.BlockSpec((1, 2, gs,1, 1), lambda l, g: (l, 0, g, 0, 0))
    mat = pl.BlockSpec((1, 2, gs,S5_GROUP, S5_STATE), lambda l, g: (l, 0, g, 0, 0))
    return pl.pallas_call(
        _s5setup_kernel,
        grid=(nl, ng // gs),
        in_specs=[vec, vec, sc, mat, mat, mat, mat],
        out_specs=[pl.BlockSpec((1, 2, gs,n2, n2 + 2 * S5_STATE), lambda l, g: (l, 0, g, 0, 0)),
                   pl.BlockSpec((1, 2, gs,n2, 2 * S5_STATE), lambda l, g: (l, 0, g, 0, 0)),
                   pl.BlockSpec((1, 2, gs,1, 2 * S5_STATE), lambda l, g: (l, 0, g, 0, 0))],
        out_shape=[jax.ShapeDtypeStruct((nl, 2, ng, n2, n2 + 2 * S5_STATE), BF),
                   jax.ShapeDtypeStruct((nl, 2, ng, n2, 2 * S5_STATE), BF),
                   jax.ShapeDtypeStruct((nl, 2, ng, 1, 2 * S5_STATE), F32)],
        compiler_params=_cp(("arbitrary", "arbitrary")),
    )(lam_re.reshape(nl, 2, ng, 1, S5_STATE), lam_im.reshape(nl, 2, ng, 1, S5_STATE),
      log_dt.reshape(nl, 2, ng, 1, 1), bt_re, bt_im, c_re, c_im)


S5_GROUPS_PER_STEP = LANES // S5_GROUP


def _s5_kernel(bb, nc, u_ref, perm_ref, mw_ref, wot_ref, a16_ref, h0_ref, y_ref, hfin_ref,
               uf_ref, yf_ref):
    n = S5_CHUNK
    n2 = S5_CHUNK * S5_GROUP
    ns = S5_STATE
    rows = bb * nc
    perm = perm_ref[...]
    uf_ref[...] = u_ref[...].astype(F32)
    ucat = jnp.concatenate([uf_ref[pl.ds(tau, rows, stride=n), :] for tau in range(n)],
                           axis=1).astype(BF)
    xcat = jnp.dot(ucat, perm, preferred_element_type=F32).astype(BF)
    ridx = lax.broadcasted_iota(jnp.int32, (rows, 2 * ns), 0)
    cpos = _imod(ridx, nc)
    lane = lax.broadcasted_iota(jnp.int32, (1, 2 * ns), 1)

    def coef_pair(coef):
        sw = pltpu.roll(coef, ns, 1)
        return jnp.where(lane < ns, coef, sw), jnp.where(lane < ns, -sw, coef)

    def cmul_rows(coef, val):
        c_re, c_im = coef_pair(coef)
        return val * c_re + pltpu.roll(val, ns, 1) * c_im

    chains = [(g, d) for g in range(S5_GROUPS_PER_STEP) for d in range(2)]
    rng = range(len(chains))
    first = (0, nc - 1)
    last = (nc - 1, 0)
    yz = [jnp.dot(xcat[:, g * n2:(g + 1) * n2], mw_ref[0, d, g], preferred_element_type=F32)
          for g, d in chains]
    h0r = []
    for g, d in chains:
        z = jnp.zeros((rows, 2 * ns), F32)
        for bi in range(bb):
            z = jnp.where(ridx == bi * nc + first[d], h0_ref[bi, g, d:d + 1], z)
        h0r.append(z)
    ap = [a16_ref[0, d, g] for g, d in chains]
    e = [yz[i][:, n2:] + cmul_rows(ap[i], h0r[i]) for i in rng]
    sh = 1
    while sh < nc:
        shifted = [jnp.where(cpos >= sh, pltpu.roll(e[i], sh, 0), 0.0) if chains[i][1] == 0 else
                   jnp.where(cpos < nc - sh, pltpu.roll(e[i], rows - sh, 0), 0.0) for i in rng]
        e = [e[i] + cmul_rows(ap[i], shifted[i]) for i in rng]
        ap = [cmul_rows(ap[i], ap[i]) for i in rng]
        sh *= 2
    hs = [jnp.where(cpos >= 1, pltpu.roll(e[i], 1, 0), h0r[i]) if chains[i][1] == 0 else
          jnp.where(cpos < nc - 1, pltpu.roll(e[i], rows - 1, 0), h0r[i]) for i in rng]
    yd = [yz[i][:, :n2] + _mm_nt(hs[i], wot_ref[0, d, g]) for i, (g, d) in enumerate(chains)]
    ycat = jnp.concatenate([yd[2 * g] + yd[2 * g + 1] for g in range(S5_GROUPS_PER_STEP)], axis=1)
    ytok = lax.dot_general(ycat.astype(BF), perm, (((1,), (1,)), ((), ())), preferred_element_type=F32)
    for tau in range(n):
        yf_ref[pl.ds(tau, rows, stride=n), :] = ytok[:, tau * LANES:(tau + 1) * LANES]
    y_ref[...] = yf_ref[...].astype(y_ref.dtype)
    for i, (g, d) in enumerate(chains):
        for bi in range(bb):
            hfin_ref[bi, g, d:d + 1] = e[i][bi * nc + last[d]:bi * nc + last[d] + 1]


def _s5_scan(h2, nb, t, bb, perm, mw, wot, a16, h0, layer):
    n = S5_CHUNK
    nc = t // n
    gs = S5_GROUPS_PER_STEP
    ng = 1024 // S5_GROUP
    n2 = S5_CHUNK * S5_GROUP
    par = lambda shape: pl.BlockSpec((1, 2, gs) + shape, lambda s, b: (layer, 0, s, 0, 0))
    st = pl.BlockSpec((bb, gs, 2, 2 * S5_STATE), lambda s, b: (b, s, 0, 0))
    return pl.pallas_call(
        functools.partial(_s5_kernel, bb, nc),
        grid=(ng // gs, nb // bb),
        in_specs=[pl.BlockSpec((bb * t, LANES), lambda s, b: (b, COL_U // LANES + s)),
                  pl.BlockSpec(perm.shape, lambda s, b: (0, 0)),
                  par((n2, n2 + 2 * S5_STATE)), par((n2, 2 * S5_STATE)), par((1, 2 * S5_STATE)), st],
        out_specs=[pl.BlockSpec((bb * t, LANES), lambda s, b: (b, s)), st],
        out_shape=[jax.ShapeDtypeStruct((nb * t, 1024), BF),
                   jax.ShapeDtypeStruct((nb, ng, 2, 2 * S5_STATE), F32)],
        scratch_shapes=[pltpu.VMEM((bb * t, LANES), F32), pltpu.VMEM((bb * t, LANES), F32)],
        compiler_params=_cp(("arbitrary", "arbitrary")),
    )(h2, perm, mw, wot, a16, h0)


def _s5post_kernel(y_ref, u_ref, d_ref, w_ref, b_ref, o_ref):
    yy = y_ref[...].astype(F32) + d_ref[...] * u_ref[...].astype(F32)
    z = jax.nn.gelu(yy)
    o_ref[...] = (z * jax.nn.sigmoid(_mm(z, w_ref[...]) + b_ref[...])).astype(o_ref.dtype)


def _s5_post(y2, h2, d_skip, glu_w, glu_b, tm):
    m = y2.shape[0]
    full = lambda shape: pl.BlockSpec(shape, lambda i: (0,) * len(shape))
    return pl.pallas_call(
        _s5post_kernel,
        grid=(m // tm,),
        in_specs=[pl.BlockSpec((tm, 1024), lambda i: (i, 0)),
                  pl.BlockSpec((tm, 1024), lambda i: (i, COL_U // 1024)),
                  full((1, 1024)), full((1024, 1024)), full((1, 1024))],
        out_specs=pl.BlockSpec((tm, 1024), lambda i: (i, 0)),
        out_shape=jax.ShapeDtypeStruct((m, 1024), BF),
        compiler_params=_cp(("arbitrary",)),
    )(y2, h2, d_skip, glu_w, glu_b)


def _merge_kernel(oa_ref, ob_ref, oc_ref, g0_ref, g1_ref, g2_ref, w_ref, o_ref):
    acc = None
    for o_r, g_r, n in ((oa_ref, g0_ref, 0), (ob_ref, g1_ref, 1), (oc_ref, g2_ref, 2)):
        pr = jnp.dot(o_r[...], w_ref[n], preferred_element_type=F32)
        term = jax.nn.sigmoid(g_r[...].astype(F32)) * pr
        acc = term if acc is None else acc + term
    o_ref[...] = acc.astype(o_ref.dtype)


def _merge(oa, ob, oc, h2, w_branch, tm):
    m = oa.shape[0]
    d = w_branch.shape[2]
    tn = 512
    seq = pl.BlockSpec((tm, 1024), lambda i, j: (i, 0))
    gate = lambda n: pl.BlockSpec((tm, tn), lambda i, j: (i, (COL_GATE + n * d) // tn + j))
    return pl.pallas_call(
        _merge_kernel,
        grid=(m // tm, d // tn),
        in_specs=[seq, seq, seq, gate(0), gate(1), gate(2),
                  pl.BlockSpec((3, 1024, tn), lambda i, j: (0, 0, j))],
        out_specs=pl.BlockSpec((tm, tn), lambda i, j: (i, j)),
        out_shape=jax.ShapeDtypeStruct((m, d), BF),
        compiler_params=_cp(("arbitrary", "arbitrary")),
    )(oa, ob, oc, h2, h2, h2, w_branch)


def _outproj_kernel(m_ref, w_ref, x_ref, gt_ref, o_ref):
    o_ref[...] = x_ref[...] + gt_ref[0, 0] * jnp.dot(m_ref[...], w_ref[...], preferred_element_type=F32)


def _outproj(mg, w_out, x2, mod, row_fn, tm):
    m, d = x2.shape
    tn = d
    return pl.pallas_call(
        _outproj_kernel,
        grid=(m // tm, d // tn),
        in_specs=[pl.BlockSpec((tm, d), lambda i, j: (i, 0)),
                  pl.BlockSpec((d, tn), lambda i, j: (0, j)),
                  pl.BlockSpec((tm, tn), lambda i, j: (i, j)),
                  pl.BlockSpec((1, 1, 1, tn), lambda i, j: (row_fn(i), 2, 0, j))],
        out_specs=pl.BlockSpec((tm, tn), lambda i, j: (i, j)),
        out_shape=jax.ShapeDtypeStruct((m, d), F32),
        compiler_params=_cp(("arbitrary", "arbitrary")),
    )(mg, w_out, x2, mod)


def _mlp_kernel(nf, x_ref, g_ref, sh_ref, sc_ref, gt_ref, w1_ref, w2_ref, o_ref, xn_ref, acc_ref):
    f = pl.program_id(1)

    @pl.when(f == 0)
    def _():
        y = _rms(x_ref[...], g_ref[...])
        xn_ref[...] = (y * (1.0 + sc_ref[0, 0]) + sh_ref[0, 0]).astype(BF)
        acc_ref[...] = jnp.zeros(acc_ref.shape, F32)

    hmid = jnp.maximum(jnp.dot(xn_ref[...], w1_ref[...], preferred_element_type=F32), 0.0)
    acc_ref[...] += jnp.dot((hmid * hmid).astype(BF), w2_ref[...], preferred_element_type=F32)

    @pl.when(f == nf - 1)
    def _():
        o_ref[...] = x_ref[...] + gt_ref[0, 0] * acc_ref[...]


def _mlp(x2, g, mod, w1, w2, row_fn, tm):
    m, d = x2.shape
    dff = w1.shape[1]
    tf = 1024
    nf = dff // tf
    modspec = lambda k: pl.BlockSpec((1, 1, 1, d), lambda i, f: (row_fn(i), k, 0, 0))
    return pl.pallas_call(
        functools.partial(_mlp_kernel, nf),
        grid=(m // tm, nf),
        in_specs=[pl.BlockSpec((tm, d), lambda i, f: (i, 0)),
                  pl.BlockSpec((1, d), lambda i, f: (0, 0)),
                  modspec(3), modspec(4), modspec(5),
                  pl.BlockSpec((d, tf), lambda i, f: (0, f)),
                  pl.BlockSpec((tf, d), lambda i, f: (f, 0))],
        out_specs=pl.BlockSpec((tm, d), lambda i, f: (i, 0)),
        out_shape=jax.ShapeDtypeStruct((m, d), F32),
        scratch_shapes=[pltpu.VMEM((tm, d), BF), pltpu.VMEM((tm, d), F32)],
        compiler_params=_cp(("arbitrary", "arbitrary")),
    )(x2, g, mod, mod, mod, w1, w2)


def _rope_tables(t):
    rows = t // GRID_W
    row = jnp.repeat(jnp.arange(rows, dtype=F32), GRID_W)
    col = jnp.tile(jnp.arange(GRID_W, dtype=F32), rows)
    n_freq = MLA_ROPE // 4
    inv = ROPE_THETA ** (-jnp.arange(n_freq, dtype=F32) / n_freq)
    ang = jnp.concatenate([row[:, None] * inv, col[:, None] * inv], axis=-1)
    cos, sin = jnp.cos(ang), jnp.sin(ang)
    pad = jnp.zeros((t, LANES - MLA_ROPE), F32)
    return (jnp.concatenate([cos, cos, pad], axis=-1), jnp.concatenate([-sin, sin, pad], axis=-1))


def _pad_lanes(g, n):
    return jnp.pad(g.reshape(1, -1), ((0, 0), (0, n - g.shape[-1])))


def _s5_perm():
    p = jnp.arange(S5_CHUNK * LANES)
    tau, g, i = p // LANES, (p % LANES) // S5_GROUP, p % S5_GROUP
    q = g * (S5_CHUNK * S5_GROUP) + tau * S5_GROUP + i
    return (q[:, None] == p[None, :]).astype(BF)


def kernel(x, c, ctx, c_ctx, ada_w, ada_b, norm1_g, norm2_g, w_in, mla_q_lora_g, mla_kv_lora_g, mla_w_uq, mla_w_ukv, mla_qn_nope_g, mla_qn_rope_g, mla_kn_nope_g, mla_kn_rope_g, rwkv_conv, rwkv_w0, rwkv_w2, rwkv_a0, rwkv_a2, rwkv_g2, rwkv_k_k, rwkv_k_a, rwkv_r_k, rwkv_ln_g, rwkv_ln_b, s5_lam_re, s5_lam_im, s5_log_dt, s5_b_re, s5_b_im, s5_c_re, s5_c_im, s5_d, s5_glu_w, s5_glu_b, w_branch, w_out, w_mlp1, w_mlp2):
    nb, t, d = x.shape
    tc = ctx.shape[1]
    depth = ada_w.shape[0]
    ctx_row = nb

    cc = jnp.concatenate([c, c_ctx[None, :], jnp.zeros((8 - nb - 1, d), F32)], axis=0)
    mod_all = _ada(cc, ada_w, ada_b)

    bt = lambda z: jnp.swapaxes(z, -1, -2)
    mw, wot, a16 = _s5_setup(s5_lam_re, s5_lam_im, s5_log_dt, bt(s5_b_re), bt(s5_b_im), s5_c_re, s5_c_im)

    cos_t, sin_t = _rope_tables(t)
    cos_c = jnp.concatenate([jnp.ones((tc, MLA_ROPE), F32), jnp.zeros((tc, LANES - MLA_ROPE), F32)], axis=-1)
    sin_c = jnp.zeros((tc, LANES), F32)
    hi = jnp.arange(HEAD_GROUP) // RWKV_HEAD
    ones_bd = (hi[:, None] == hi[None, :]).astype(BF)
    ones_bd128 = ones_bd[:LANES, :LANES]

    tm_t = min(1024, t)
    tm_c = min(1024, nb * tc)
    lat_row = lambda tm: (lambda i: i // (t // tm))
    ctx_rowf = lambda i: ctx_row

    x2 = x.reshape(nb * t, d)
    xc2 = ctx.reshape(nb * tc, d)
    s_zero = jnp.zeros((nb, 2, 1024 // HEAD_GROUP, HEAD_GROUP, HEAD_GROUP), F32)
    h_zero = jnp.zeros((nb, 1024 // S5_GROUP, 2, 2 * S5_STATE), F32)
    perm = _s5_perm()
    w_perm_all = _permute_w_in(w_in)

    for l in range(depth):
        last = l == depth - 1
        mod = mod_all[l].reshape(8, 6, 1, d)
        w_perm = w_perm_all[l]
        g1 = norm1_g[l].reshape(1, d)
        g2n = norm2_g[l].reshape(1, d)
        w_uq = jnp.pad(mla_w_uq[l].reshape(-1, N_HEADS, MLA_NOPE + MLA_ROPE),
                       ((0, 0), (0, 0), (0, 256 - MLA_NOPE - MLA_ROPE))).reshape(-1, N_HEADS * 256).astype(BF)
        w_ukv = mla_w_ukv[l].astype(BF)
        gq = mla_q_lora_g[l].reshape(1, -1)
        gkv = mla_kv_lora_g[l].reshape(1, -1)
        gqn, gqr = mla_qn_nope_g[l].reshape(1, -1), _pad_lanes(mla_qn_rope_g[l], LANES)
        gkn, gkr = mla_kn_nope_g[l].reshape(1, -1), _pad_lanes(mla_kn_rope_g[l], LANES)
        rp = dict(w0=rwkv_w0[l], w2=rwkv_w2[l].astype(BF), a0=rwkv_a0[l], a2=rwkv_a2[l].astype(BF),
                  k_a=rwkv_k_a[l].reshape(1, -1), r_k=rwkv_r_k[l].reshape(1, -1),
                  ln_g=rwkv_ln_g[l].reshape(1, -1), ln_b=rwkv_ln_b[l].reshape(1, -1),
                  g2_pad=jnp.pad(rwkv_g2[l], ((SM_GD, 512 - SM_GD - rwkv_g2.shape[1]), (0, 0))).astype(BF))
        conv_w = rwkv_conv[l]
        k_k = rwkv_k_k[l].reshape(1, -1)
        glu_w = s5_glu_w[l].astype(BF)
        glu_b = s5_glu_b[l].reshape(1, -1)
        d_skip = s5_d[l].reshape(1, -1)
        wb = w_branch[l].astype(BF)
        wo = w_out[l].astype(BF)
        w1 = w_mlp1[l].astype(BF)
        w2 = w_mlp2[l].astype(BF)

        hc = _inproj(xc2, g1, mod, w_perm, ctx_rowf, tm_c)
        k_c, v_c = _kvprep(hc, nb, tc, gkv, w_ukv, gkn, gkr, cos_c, sin_c, tc)
        rc_c, kc_c, vc_c, kk_c = _rwkv_conv(hc.reshape(nb, tc, -1), conv_w, k_k, ones_bd128)
        yf_c, yb_c, s_ctx = _rwkv_scan(rc_c, kc_c, vc_c, kk_c, hc, rp, s_zero)
        ys_c, h_ctx = _s5_scan(hc, nb, tc, nb, perm, mw, wot, a16, h_zero, l)

        ht = _inproj(x2, g1, mod, w_perm, lat_row(tm_t), tm_t)
        tq = min(512, t)
        q_t = _qprep(ht, nb, t, gq, w_uq, gqn, gqr, cos_t, sin_t, tq)
        k_t, v_t = _kvprep(ht, nb, t, gkv, w_ukv, gkn, gkr, cos_t, sin_t, tq)
        o_a = _attention(q_t, k_t, v_t, (k_c, v_c), min(512, t), min(1024, t)).reshape(nb * t, -1)

        rc_t, kc_t, vc_t, kk_t = _rwkv_conv(ht.reshape(nb, t, -1), conv_w, k_k, ones_bd128)
        yf_t, yb_t, _ = _rwkv_scan(rc_t, kc_t, vc_t, kk_t, ht, rp, s_ctx)
        flat = lambda z: z.reshape(-1, z.shape[-1])
        o_b = _rwkv_post(flat(yf_t), flat(yb_t), flat(rc_t), flat(kc_t), flat(vc_t), ht, rp, ones_bd,
                         min(512, nb * t))

        ys_t, _ = _s5_scan(ht, nb, t, 1, perm, mw, wot, a16, h_ctx, l)
        o_c = _s5_post(ys_t, ht, d_skip, glu_w, glu_b, min(512, nb * t))

        mg = _merge(o_a, o_b, o_c, ht, wb, tm_t)
        tm_m = min(512, t)
        x2 = _outproj(mg, wo, x2, mod, lat_row(tm_m), tm_m)
        x2 = _mlp(x2, g2n, mod, w1, w2, lat_row(tm_m), tm_m)

        if not last:
            q_c = _qprep(hc, nb, tc, gq, w_uq, gqn, gqr, cos_c, sin_c, tc)
            o_a_c = _attention(q_c, k_c, v_c, None, tc, tc).reshape(nb * tc, -1)
            o_b_c = _rwkv_post(flat(yf_c), flat(yb_c), flat(rc_c), flat(kc_c), flat(vc_c), hc, rp, ones_bd,
                               min(512, nb * tc))
            o_c_c = _s5_post(ys_c, hc, d_skip, glu_w, glu_b, min(512, nb * tc))
            mg_c = _merge(o_a_c, o_b_c, o_c_c, hc, wb, tm_c)
            tm_mc = min(512, nb * tc)
            xc2 = _outproj(mg_c, wo, xc2, mod, ctx_rowf, tm_mc)
            xc2 = _mlp(xc2, g2n, mod, w1, w2, ctx_rowf, tm_mc)

    return x2.reshape(nb, t, d)
```

```python
import functools
import math

import jax
import jax.numpy as jnp
from jax import lax
from jax.experimental import pallas as pl
from jax.experimental.pallas import tpu as pltpu

F32 = jnp.float32
BF = jnp.bfloat16

GRID_W = 64
N_HEADS = 8
MLA_NOPE = 128
MLA_ROPE = 64
MLA_V = 128
ROPE_THETA = 10000.0
RWKV_HEAD = 64
RWKV_GN_EPS = 64e-5
L2_EPS = 1e-12
S5_GROUP = 16
S5_STATE = 64
NORM_EPS = 1e-6

LANES = 128
VMEM_LIMIT = 48 * 1024 * 1024

COL_R, COL_K, COL_V, COL_U, COL_GATE, COL_CQ, COL_CKV, COL_SMALL = (
    0, 1024, 2048, 3072, 4096, 10240, 10752, 11264)
N_IN_PAD = 11776
SM_KR, SM_WD, SM_AD, SM_GD = 0, 64, 192, 320

RWKV_CHUNK = 64
HEAD_GROUP = 256
S5_CHUNK = 16
S5_SETUP_GROUPS = 8
ATTN_HEADS_PER_STEP = 2


def _cp(sem, vmem=VMEM_LIMIT):
    return pltpu.CompilerParams(dimension_semantics=sem, vmem_limit_bytes=vmem)


def _mm(a, b):
    return jnp.dot(a.astype(BF), b.astype(BF), preferred_element_type=F32)


def _mm_nt(a, b):
    return lax.dot_general(a.astype(BF), b.astype(BF), (((1,), (1,)), ((), ())),
                           preferred_element_type=F32)


def _mm_tn(a, b):
    return lax.dot_general(a.astype(BF), b.astype(BF), (((0,), (0,)), ((), ())),
                           preferred_element_type=F32)


def _idiv(x, n):
    assert n & (n - 1) == 0, n
    return lax.shift_right_logical(x, n.bit_length() - 1)


def _imod(x, n):
    assert n & (n - 1) == 0, n
    return lax.bitwise_and(x, n - 1)


def _rms(x, g, eps=NORM_EPS):
    return x * lax.rsqrt(jnp.mean(x * x, axis=-1, keepdims=True) + eps) * g


def _softplus(x):
    return jnp.maximum(x, 0.0) + jnp.log(1.0 + jnp.exp(-jnp.abs(x)))


def _ada_kernel(c_ref, w_ref, b_ref, o_ref):
    c = c_ref[...]
    s = c * jax.nn.sigmoid(c)
    o_ref[0] = _mm(s, w_ref[0]) + b_ref[0]


def _ada(cc, ada_w, ada_b):
    n_layer, d, n = ada_w.shape
    tn = 1024
    return pl.pallas_call(
        _ada_kernel,
        grid=(n_layer, n // tn),
        in_specs=[pl.BlockSpec((8, d), lambda l, j: (0, 0)),
                  pl.BlockSpec((1, d, tn), lambda l, j: (l, 0, j)),
                  pl.BlockSpec((1, 1, tn), lambda l, j: (l, 0, j))],
        out_specs=pl.BlockSpec((1, 8, tn), lambda l, j: (l, 0, j)),
        out_shape=jax.ShapeDtypeStruct((n_layer, 8, n), F32),
        compiler_params=_cp(("arbitrary", "arbitrary")),
    )(cc, ada_w, ada_b.reshape(n_layer, 1, n))


W_IN_SEGMENTS = ((1088, 3072), (4576, 1024), (5600, 6144), (0, 1024), (1024, 64), (4160, 416))


def _wperm_kernel(w_ref, o_ref):
    off = 0
    for src, n in W_IN_SEGMENTS:
        o_ref[0, :, off:off + n] = w_ref[0, :, src:src + n].astype(BF)
        off += n
    o_ref[0, :, off:] = jnp.zeros((o_ref.shape[1], o_ref.shape[2] - off), BF)


def _permute_w_in(w_in):
    nl, d, n_in = w_in.shape
    tr = 256
    return pl.pallas_call(
        _wperm_kernel,
        grid=(nl, d // tr),
        in_specs=[pl.BlockSpec((1, tr, n_in), lambda l, i: (l, i, 0))],
        out_specs=pl.BlockSpec((1, tr, N_IN_PAD), lambda l, i: (l, i, 0)),
        out_shape=jax.ShapeDtypeStruct((nl, d, N_IN_PAD), BF),
        compiler_params=_cp(("arbitrary", "arbitrary")),
    )(w_in)


def _inproj_kernel(x_ref, g_ref, sh_ref, sc_ref, w_ref, o_ref, xn_ref):
    @pl.when(pl.program_id(1) == 0)
    def _():
        y = _rms(x_ref[...], g_ref[...])
        xn_ref[...] = (y * (1.0 + sc_ref[0, 0]) + sh_ref[0, 0]).astype(BF)

    o_ref[...] = jnp.dot(xn_ref[...], w_ref[...], preferred_element_type=F32).astype(o_ref.dtype)


def _inproj(x2, g, mod, w, row_fn, tm):
    m, d = x2.shape
    n = w.shape[1]
    tn = 512
    return pl.pallas_call(
        _inproj_kernel,
        grid=(m // tm, n // tn),
        in_specs=[pl.BlockSpec((tm, d), lambda i, j: (i, 0)),
                  pl.BlockSpec((1, d), lambda i, j: (0, 0)),
                  pl.BlockSpec((1, 1, 1, d), lambda i, j: (row_fn(i), 0, 0, 0)),
                  pl.BlockSpec((1, 1, 1, d), lambda i, j: (row_fn(i), 1, 0, 0)),
                  pl.BlockSpec((d, tn), lambda i, j: (0, j))],
        out_specs=pl.BlockSpec((tm, tn), lambda i, j: (i, j)),
        out_shape=jax.ShapeDtypeStruct((m, n), BF),
        scratch_shapes=[pltpu.VMEM((tm, d), BF)],
        compiler_params=_cp(("arbitrary", "arbitrary")),
    )(x2, g, mod, mod, w)


def _rope_rotate(x, cos_t, sin_t):
    lane = lax.broadcasted_iota(jnp.int32, x.shape, 1)
    sw = jnp.where(lane < 32, pltpu.roll(x, 96, 1), pltpu.roll(x, 32, 1))
    return x * cos_t + sw * sin_t


def _qprep_kernel(scale, cq_ref, g_ref, w_ref, gn_ref, gr_ref, cos_ref, sin_ref, q_ref):
    xn = _rms(cq_ref[...].astype(F32), g_ref[...])
    q = _mm(xn, w_ref[...])
    cos_t, sin_t = cos_ref[...], sin_ref[...]
    for h in range(N_HEADS):
        nope = q[:, h * 256:h * 256 + 128]
        rp = q[:, h * 256 + 128:h * 256 + 256]
        nope = _rms(nope, gn_ref[...])
        rp = rp * lax.rsqrt(jnp.sum(rp * rp, axis=-1, keepdims=True) * (1.0 / MLA_ROPE) + NORM_EPS) * gr_ref[...]
        rp = _rope_rotate(rp, cos_t, sin_t)
        q_ref[0, h, :, 0:128] = (nope * scale).astype(BF)
        q_ref[0, h, :, 128:256] = (rp * scale).astype(BF)


def _qprep(h2, nb, t, g, w, gn, gr, cos_t, sin_t, tm):
    nt = t // tm
    scale = math.log2(math.e) / math.sqrt(MLA_NOPE + MLA_ROPE)
    return pl.pallas_call(
        functools.partial(_qprep_kernel, scale),
        grid=(nb * nt,),
        in_specs=[pl.BlockSpec((tm, 512), lambda i: (i, COL_CQ // 512)),
                  pl.BlockSpec((1, 512), lambda i: (0, 0)),
                  pl.BlockSpec((512, N_HEADS * 256), lambda i: (0, 0)),
                  pl.BlockSpec((1, 128), lambda i: (0, 0)),
                  pl.BlockSpec((1, 128), lambda i: (0, 0)),
                  pl.BlockSpec((tm, 128), lambda i: (i % nt, 0)),
                  pl.BlockSpec((tm, 128), lambda i: (i % nt, 0))],
        out_specs=pl.BlockSpec((1, N_HEADS, tm, 256), lambda i: (i // nt, 0, i % nt, 0)),
        out_shape=jax.ShapeDtypeStruct((nb, N_HEADS, t, 256), BF),
        compiler_params=_cp(("arbitrary",)),
    )(h2, g, w, gn, gr, cos_t, sin_t)


def _kvprep_kernel(ckv_ref, sm_ref, g_ref, w_ref, gn_ref, gr_ref, cos_ref, sin_ref, k_ref, v_ref):
    xn = _rms(ckv_ref[...].astype(F32), g_ref[...])
    kv = _mm(xn, w_ref[...])
    sm = sm_ref[...].astype(F32)
    lane = lax.broadcasted_iota(jnp.int32, sm.shape, 1)
    kr = jnp.where(lane < MLA_ROPE, sm, 0.0)
    kr = kr * lax.rsqrt(jnp.sum(kr * kr, axis=-1, keepdims=True) * (1.0 / MLA_ROPE) + NORM_EPS) * gr_ref[...]
    kr = _rope_rotate(kr, cos_ref[...], sin_ref[...]).astype(BF)
    ones_col = jnp.where(lane == 0, 1.0, 0.0).astype(BF)
    for h in range(N_HEADS):
        kn = _rms(kv[:, h * 256:h * 256 + 128], gn_ref[...])
        k_ref[0, h, :, 0:128] = kn.astype(BF)
        k_ref[0, h, :, 128:256] = kr
        v_ref[0, h, :, 0:128] = kv[:, h * 256 + 128:h * 256 + 256].astype(BF)
        v_ref[0, h, :, 128:256] = ones_col


def _kvprep(h2, nb, t, g, w, gn, gr, cos_t, sin_t, tm):
    nt = t // tm
    return pl.pallas_call(
        _kvprep_kernel,
        grid=(nb * nt,),
        in_specs=[pl.BlockSpec((tm, 512), lambda i: (i, COL_CKV // 512)),
                  pl.BlockSpec((tm, 128), lambda i: (i, COL_SMALL // 128)),
                  pl.BlockSpec((1, 512), lambda i: (0, 0)),
                  pl.BlockSpec((512, N_HEADS * 256), lambda i: (0, 0)),
                  pl.BlockSpec((1, 128), lambda i: (0, 0)),
                  pl.BlockSpec((1, 128), lambda i: (0, 0)),
                  pl.BlockSpec((tm, 128), lambda i: (i % nt, 0)),
                  pl.BlockSpec((tm, 128), lambda i: (i % nt, 0))],
        out_specs=[pl.BlockSpec((1, N_HEADS, tm, 256), lambda i: (i // nt, 0, i % nt, 0)),
                   pl.BlockSpec((1, N_HEADS, tm, 256), lambda i: (i // nt, 0, i % nt, 0))],
        out_shape=[jax.ShapeDtypeStruct((nb, N_HEADS, t, 256), BF),
                   jax.ShapeDtypeStruct((nb, N_HEADS, t, 256), BF)],
        compiler_params=_cp(("arbitrary",)),
    )(h2, h2, g, w, gn, gr, cos_t, sin_t)


def _attn_kernel(has_extra, nk, *refs):
    if has_extra:
        q_ref, k_ref, v_ref, kc_ref, vc_ref, o_ref, m_ref, acc_ref = refs
    else:
        q_ref, k_ref, v_ref, o_ref, m_ref, acc_ref = refs
    j = pl.program_id(3)

    @pl.when(j == 0)
    def _():
        m_ref[...] = jnp.full(m_ref.shape, -1e30, F32)
        acc_ref[...] = jnp.zeros(acc_ref.shape, F32)

    heads = range(ATTN_HEADS_PER_STEP)

    def step(k_r, v_r):
        s = [lax.dot_general(q_ref[0, h], k_r[0, h], (((1,), (1,)), ((), ())),
                             preferred_element_type=F32) for h in heads]
        m_prev = [m_ref[h] for h in heads]
        m_new = [jnp.maximum(m_prev[h], jnp.max(s[h], axis=1, keepdims=True)) for h in heads]
        alpha = [jnp.exp2(m_prev[h] - m_new[h]) for h in heads]
        p = [jnp.exp2((s[h] - m_new[h]).astype(BF)) for h in heads]
        pv = [jnp.dot(p[h], v_r[0, h], preferred_element_type=F32) for h in heads]
        for h in heads:
            acc_ref[h] = alpha[h] * acc_ref[h] + pv[h]
            m_ref[h] = m_new[h]

    step(k_ref, v_ref)

    @pl.when(j == nk - 1)
    def _():
        if has_extra:
            step(kc_ref, vc_ref)
        for h in heads:
            acc = acc_ref[h]
            o_ref[0, :, h * MLA_V:(h + 1) * MLA_V] = (
                acc[:, :MLA_V] / acc[:, MLA_V:MLA_V + 1]).astype(o_ref.dtype)


def _attention(q, k, v, extra, bq, bk):
    nb, nh, t, _ = q.shape
    tk = k.shape[2]
    nk = tk // bk
    hb = ATTN_HEADS_PER_STEP
    in_specs = [pl.BlockSpec((1, hb, bq, 256), lambda b, h, i, j: (b, h, i, 0)),
                pl.BlockSpec((1, hb, bk, 256), lambda b, h, i, j: (b, h, j, 0)),
                pl.BlockSpec((1, hb, bk, 256), lambda b, h, i, j: (b, h, j, 0))]
    args = [q, k, v]
    if extra is not None:
        kc, vc = extra
        tc = kc.shape[2]
        in_specs += [pl.BlockSpec((1, hb, tc, 256), lambda b, h, i, j: (b, h, 0, 0)),
                     pl.BlockSpec((1, hb, tc, 256), lambda b, h, i, j: (b, h, 0, 0))]
        args += [kc, vc]
    return pl.pallas_call(
        functools.partial(_attn_kernel, extra is not None, nk),
        grid=(nb, nh // hb, t // bq, nk),
        in_specs=in_specs,
        out_specs=pl.BlockSpec((1, bq, hb * MLA_V), lambda b, h, i, j: (b, i, h)),
        out_shape=jax.ShapeDtypeStruct((nb, t, nh * MLA_V), BF),
        scratch_shapes=[pltpu.VMEM((hb, bq, 1), F32), pltpu.VMEM((hb, bq, 2 * MLA_V), F32)],
        compiler_params=_cp(("arbitrary",) * 4),
    )(*args)


def _conv_kernel(r_ref, k_ref, v_ref, wr_ref, wk_ref, wv_ref, kk_w_ref, ones_ref,
                 ro_ref, ko_ref, vo_ref, kko_ref):
    t = r_ref.shape[1]
    row = lax.broadcasted_iota(jnp.int32, (t, LANES), 0)

    def conv(x_ref, w_ref):
        x = x_ref[0].astype(F32)
        xm = jnp.where(row == 0, 0.0, pltpu.roll(x, 1, 0))
        xp = jnp.where(row == t - 1, 0.0, pltpu.roll(x, t - 1, 0))
        w = w_ref[...]
        return xm * w[0:1] + x * w[1:2] + xp * w[2:3]

    ro_ref[0] = conv(r_ref, wr_ref).astype(BF)
    vo_ref[0] = conv(v_ref, wv_ref).astype(BF)
    k = conv(k_ref, wk_ref)
    ko_ref[0] = k.astype(BF)
    kk = k * kk_w_ref[...]
    ss = _mm(kk * kk, ones_ref[...])
    kko_ref[0] = (kk * lax.rsqrt(ss + L2_EPS)).astype(BF)


def _rwkv_conv(h3, conv_w, k_k, ones_bd):
    nb, t, _ = h3.shape
    nj = 1024 // LANES
    blk = lambda off: pl.BlockSpec((1, t, LANES), lambda b, j: (b, 0, off // LANES + j))
    wblk = lambda off: pl.BlockSpec((3, LANES), lambda b, j: (0, off // LANES + j))
    oblk = pl.BlockSpec((1, t, LANES), lambda b, j: (b, 0, j))
    osh = jax.ShapeDtypeStruct((nb, t, 1024), BF)
    return pl.pallas_call(
        _conv_kernel,
        grid=(nb, nj),
        in_specs=[blk(COL_R), blk(COL_K), blk(COL_V), wblk(0), wblk(1024), wblk(2048),
                  pl.BlockSpec((1, LANES), lambda b, j: (0, j)),
                  pl.BlockSpec((LANES, LANES), lambda b, j: (0, 0))],
        out_specs=[oblk, oblk, oblk, oblk],
        out_shape=[osh, osh, osh, osh],
        compiler_params=_cp(("arbitrary", "arbitrary")),
    )(h3, h3, h3, conv_w, conv_w, conv_w, k_k, ones_bd)


def _bd_rows(x, bdmask):
    reps = HEAD_GROUP // x.shape[0]
    return jnp.where(bdmask, jnp.concatenate([x] * reps, axis=0), 0.0)


def _scan_prep(d, r_ref, k_ref, v_ref, kk_ref, sm_ref, w0_ref, w2_ref, a0_ref, a2_ref, ka_ref, tri):
    L = RWKV_CHUNK
    r = r_ref[0].astype(F32)
    k = k_ref[0].astype(F32)
    v = v_ref[0].astype(F32)
    kk = kk_ref[0].astype(F32)
    sm = sm_ref[...].astype(F32)
    wd = sm[:, SM_WD + 64 * d:SM_WD + 64 * d + 64]
    ad = sm[:, SM_AD + 64 * d:SM_AD + 64 * d + 64]
    w_log = -_softplus(-(w0_ref[d:d + 1] + _mm(jnp.tanh(wd), w2_ref[d]))) - 0.5
    lw = -jnp.exp(w_log)
    a = jax.nn.sigmoid(a0_ref[d:d + 1] + _mm(ad, a2_ref[d]))
    b = kk * a
    krep = k * (1.0 + (a - 1.0) * ka_ref[...])
    cl = jnp.dot(tri, lw, precision=lax.Precision.HIGHEST, preferred_element_type=F32)
    cl_last = cl[0:1] if d == 1 else cl[L - 1:L]
    e_neg = jnp.exp(-cl)
    e_last = jnp.exp(cl_last - cl)
    at = -(kk * jnp.exp(cl - lw))
    rt = r * jnp.exp(cl)
    bt = b * e_neg
    kt = krep * e_neg
    bh = b * e_last
    kh = krep * e_last
    return dict(at=at, rt=rt, bt=bt, kt=kt, bh=bh, kh=kh, v=v, p_last=jnp.exp(cl_last))


def _scan_kernel(nc, rf, kf, vf, kkf, smf, rb, kb, vb, kkb, smb, w0, w2, a0, a2, ka, s_in,
                 yf, yb, s_out, s_ref):
    c = pl.program_id(1)

    @pl.when(c == 0)
    def _():
        s_ref[...] = s_in[0]

    L = RWKV_CHUNK
    row2 = lax.broadcasted_iota(jnp.int32, (HEAD_GROUP, HEAD_GROUP), 0)
    col2 = lax.broadcasted_iota(jnp.int32, (HEAD_GROUP, HEAD_GROUP), 1)
    bdmask = _idiv(row2, RWKV_HEAD) == _idiv(col2, RWKV_HEAD)
    tt = lax.broadcasted_iota(jnp.int32, (L, HEAD_GROUP), 0)
    ss = _imod(lax.broadcasted_iota(jnp.int32, (L, HEAD_GROUP), 1), L)
    rl = lax.broadcasted_iota(jnp.int32, (L, L), 0)
    cl = lax.broadcasted_iota(jnp.int32, (L, L), 1)
    blk = tuple(_idiv(ss, n) == _idiv(tt, n) for n in (8, 16, 32, 64))
    eye = jnp.where(ss == tt, 1.0, 0.0)
    strict = (ss < tt, ss > tt)
    incl = (ss <= tt, ss >= tt)
    prep = (_scan_prep(0, rf, kf, vf, kkf, smf, w0, w2, a0, a2, ka, (rl >= cl).astype(F32)),
            _scan_prep(1, rb, kb, vb, kkb, smb, w0, w2, a0, a2, ka, (rl <= cl).astype(F32)))
    y_refs = (yf, yb)

    chains = [(d, gi) for d in range(2) for gi in range(1024 // HEAD_GROUP)]
    n = len(chains)
    bd = lambda x: _bd_rows(x, bdmask)
    cat = jnp.concatenate
    g_ = lambda name: [prep[d][name][:, gi * HEAD_GROUP:(gi + 1) * HEAD_GROUP] for d, gi in chains]
    at, rt, bt, kt, bh, kh, v = (g_(nm) for nm in ("at", "rt", "bt", "kt", "bh", "kh", "v"))
    ac = [_mm_nt(cat([at[i], rt[i]], axis=0), cat([bd(bt[i]), bd(kt[i])], axis=0)) for i in range(n)]
    a_ab = [jnp.where(strict[chains[i][0]], ac[i][:L, :HEAD_GROUP], 0.0) for i in range(n)]
    a_ak = [jnp.where(strict[chains[i][0]], ac[i][:L, HEAD_GROUP:], 0.0) for i in range(n)]
    c_b = [jnp.where(incl[chains[i][0]], ac[i][L:, :HEAD_GROUP], 0.0) for i in range(n)]
    c_k = [jnp.where(incl[chains[i][0]], ac[i][L:, HEAD_GROUP:], 0.0) for i in range(n)]
    av = [_mm(a_ak[i], bd(v[i])) for i in range(n)]
    a8 = [jnp.where(blk[0], a_ab[i], 0.0) for i in range(n)]
    p2 = [_mm(a8[i], bd(a8[i])) for i in range(n)]
    tm = [eye + a8[i] for i in range(n)]
    pt = [_mm(cat([p2[i], tm[i]], axis=0), bd(p2[i])) for i in range(n)]
    tm = [tm[i] + pt[i][L:] for i in range(n)]
    tm = [tm[i] + _mm(tm[i], bd(pt[i][:L])) for i in range(n)]
    for lvl in range(3):
        off_mask = blk[lvl + 1] & ~blk[lvl]
        z = [_mm(jnp.where(off_mask, a_ab[i], 0.0), bd(tm[i])) for i in range(n)]
        tm = [tm[i] + _mm(tm[i], bd(z[i])) for i in range(n)]
    wu = [_mm(tm[i], cat([bd(at[i]), bd(av[i])], axis=1)) for i in range(n)]
    w = [wu[i][:, :HEAD_GROUP] for i in range(n)]
    uv = [wu[i][:, HEAD_GROUP:] for i in range(n)]
    q = [rt[i] + _mm(c_b[i], bd(w[i])) for i in range(n)]
    y0 = [_mm(cat([c_b[i], c_k[i]], axis=1), cat([bd(uv[i]), bd(v[i])], axis=0)) for i in range(n)]
    zeros = jnp.zeros((L, HEAD_GROUP), F32)
    gh = [_mm_tn(cat([cat([w[i], uv[i]], axis=1), cat([zeros, v[i]], axis=1)], axis=0),
                 cat([bh[i], kh[i]], axis=0)) for i in range(n)]
    for i, (d, gi) in enumerate(chains):
        sl = slice(gi * HEAD_GROUP, (gi + 1) * HEAD_GROUP)
        s0 = s_ref[d, gi]
        y_refs[d][0, :, sl] = y0[i] + _mm_nt(q[i], s0)
        g = jnp.where(bdmask, gh[i][:HEAD_GROUP], 0.0)
        hh = jnp.where(bdmask, gh[i][HEAD_GROUP:], 0.0)
        s_ref[d, gi] = s0 * prep[d]["p_last"][:, sl] + _mm(s0, g) + hh

    @pl.when(c == nc - 1)
    def _():
        s_out[0] = s_ref[...]


def _rwkv_scan(rc, kc, vc, kk, h2, p, s_in):
    nb, t, _ = rc.shape
    L = RWKV_CHUNK
    nc = t // L
    fw = lambda b, c: (b, c, 0)
    bw = lambda b, c: (b, nc - 1 - c, 0)
    seq = lambda im: pl.BlockSpec((1, L, 1024), im)
    smf = pl.BlockSpec((L, 512), lambda b, c: (b * nc + c, COL_SMALL // 512))
    smb = pl.BlockSpec((L, 512), lambda b, c: (b * nc + nc - 1 - c, COL_SMALL // 512))
    full = lambda shape: pl.BlockSpec(shape, lambda b, c: (0,) * len(shape))
    st = pl.BlockSpec((1, 2, 4, HEAD_GROUP, HEAD_GROUP), lambda b, c: (b, 0, 0, 0, 0))
    ysh = jax.ShapeDtypeStruct((nb, t, 1024), F32)
    return pl.pallas_call(
        functools.partial(_scan_kernel, nc),
        grid=(nb, nc),
        in_specs=[seq(fw), seq(fw), seq(fw), seq(fw), smf, seq(bw), seq(bw), seq(bw), seq(bw), smb,
                  full((2, 1024)), full((2, 64, 1024)), full((2, 1024)), full((2, 64, 1024)),
                  full((1, 1024)), st],
        out_specs=[seq(fw), seq(bw), st],
        out_shape=[ysh, ysh, jax.ShapeDtypeStruct(s_in.shape, F32)],
        scratch_shapes=[pltpu.VMEM((2, 4, HEAD_GROUP, HEAD_GROUP), F32)],
        compiler_params=_cp(("arbitrary", "arbitrary")),
    )(rc, kc, vc, kk, h2, rc, kc, vc, kk, h2, p["w0"], p["w2"], p["a0"], p["a2"], p["k_a"], s_in)


def _rpost_kernel(yf_ref, yb_ref, r_ref, k_ref, v_ref, sm_ref, a0_ref, a2_ref, ka_ref, rk_ref,
                  lng_ref, lnb_ref, g2_ref, ones_ref, o_ref):
    y = yf_ref[...] + yb_ref[...]
    r = r_ref[...].astype(F32)
    k = k_ref[...].astype(F32)
    v = v_ref[...].astype(F32)
    sm = sm_ref[...].astype(F32)
    a_sum = 0.0
    for d in range(2):
        ad = sm[:, SM_AD + 64 * d:SM_AD + 64 * d + 64]
        a_sum = a_sum + jax.nn.sigmoid(a0_ref[d:d + 1] + _mm(ad, a2_ref[d]))
    kb = k * (1.0 + (0.5 * a_sum - 1.0) * ka_ref[...])
    rkb = r * kb * rk_ref[...]
    gate = _mm(jax.nn.sigmoid(sm), g2_ref[...])
    ones = ones_ref[...]
    inv_n = 1.0 / RWKV_HEAD
    for gi in range(1024 // HEAD_GROUP):
        sl = slice(gi * HEAD_GROUP, (gi + 1) * HEAD_GROUP)
        yg = y[:, sl]
        mu = _mm(yg, ones) * inv_n
        dlt = yg - mu
        var = _mm(dlt * dlt, ones) * inv_n
        yn = dlt * lax.rsqrt(var + RWKV_GN_EPS) * lng_ref[:, sl] + lnb_ref[:, sl]
        bonus = _mm(rkb[:, sl], ones) * v[:, sl]
        o_ref[:, sl] = ((yn + bonus) * gate[:, sl]).astype(o_ref.dtype)


def _rwkv_post(yf, yb, rc, kc, vc, h2, p, ones_bd, tm):
    m = yf.shape[0]
    row = lambda i: (i, 0)
    full = lambda shape: pl.BlockSpec(shape, lambda i: (0,) * len(shape))
    seq = pl.BlockSpec((tm, 1024), row)
    return pl.pallas_call(
        _rpost_kernel,
        grid=(m // tm,),
        in_specs=[seq, seq, seq, seq, seq,
                  pl.BlockSpec((tm, 512), lambda i: (i, COL_SMALL // 512)),
                  full((2, 1024)), full((2, 64, 1024)), full((1, 1024)), full((1, 1024)),
                  full((1, 1024)), full((1, 1024)), full((512, 1024)),
                  full((HEAD_GROUP, HEAD_GROUP))],
        out_specs=seq,
        out_shape=jax.ShapeDtypeStruct((m, 1024), BF),
        compiler_params=_cp(("arbitrary",)),
    )(yf, yb, rc, kc, vc, h2, p["a0"], p["a2"], p["k_a"], p["r_k"], p["ln_g"], p["ln_b"],
      p["g2_pad"], ones_bd)


def _cmul(x, y):
    return x[0] * y[0] - x[1] * y[1], x[0] * y[1] + x[1] * y[0]


def _s5setup_kernel(lre_ref, lim_ref, ldt_ref, btr_ref, bti_ref, cr_ref, ci_ref,
                    mw_ref, wot_ref, a16_ref):
    n = S5_CHUNK
    half = n // 2
    rt = _idiv(lax.broadcasted_iota(jnp.int32, (n * S5_GROUP, n * S5_GROUP), 0), S5_GROUP)
    ct = _idiv(lax.broadcasted_iota(jnp.int32, (n * S5_GROUP, n * S5_GROUP), 1), S5_GROUP)
    for gi, d in [(gi, d) for gi in range(S5_SETUP_GROUPS) for d in range(2)]:
        lr = lre_ref[0, d, gi]
        li = lim_ref[0, d, gi]
        dt = jnp.exp(ldt_ref[0, d, gi])

        def cexp(mult):
            mag = jnp.exp(mult * lr * dt)
            ang = mult * li * dt
            return mag * jnp.cos(ang), mag * jnp.sin(ang)

        a1 = cexp(1.0)
        den = lr * lr + li * li
        q_re = ((a1[0] - 1.0) * lr + a1[1] * li) / den
        q_im = (a1[1] * lr - (a1[0] - 1.0) * li) / den
        bb = (q_re * btr_ref[0, d, gi] - q_im * bti_ref[0, d, gi],
              q_re * bti_ref[0, d, gi] + q_im * btr_ref[0, d, gi])
        cc = (cr_ref[0, d, gi], ci_ref[0, d, gi])
        pw = {0: (jnp.ones_like(lr), jnp.zeros_like(lr)), 1: a1, -1: cexp(-1.0)}
        for m in range(2, n + 1):
            pw[m] = _cmul(pw[m - 1], pw[1])
        for m in range(2, half + 1):
            pw[-m] = _cmul(pw[-m + 1], pw[-1])

        def rows(base, efn):
            parts = [_cmul(base, pw[efn(t)]) for t in range(n)]
            return (jnp.concatenate([z[0] for z in parts], axis=0),
                    jnp.concatenate([z[1] for z in parts], axis=0))

        if d == 0:
            e_pow, f_pow = (lambda t: half - t), (lambda t: t - half)
            wi_pow, wo_pow = (lambda t: n - 1 - t), (lambda t: t + 1)
            mask = ct >= rt
        else:
            e_pow, f_pow = (lambda t: t - half), (lambda t: half - t)
            wi_pow, wo_pow = (lambda t: t), (lambda t: n - t)
            mask = ct <= rt
        e = rows(bb, e_pow)
        f = rows(cc, f_pow)
        mmat = lax.dot_general(jnp.concatenate([e[0], -e[1]], axis=1),
                               jnp.concatenate([f[0], f[1]], axis=1),
                               (((1,), (1,)), ((), ())),
                               precision=lax.Precision.HIGHEST, preferred_element_type=F32)
        mmat = jnp.where(mask, mmat, 0.0)
        wi = rows(bb, wi_pow)
        wo = rows(cc, wo_pow)
        mw_ref[0, d, gi] = jnp.concatenate([mmat, wi[0], wi[1]], axis=1).astype(BF)
        wot_ref[0, d, gi] = jnp.concatenate([wo[0], -wo[1]], axis=1).astype(BF)
        a16_ref[0, d, gi] = jnp.concatenate([pw[n][0], pw[n][1]], axis=1)


def _s5_setup(lam_re, lam_im, log_dt, bt_re, bt_im, c_re, c_im):
    nl, _, ng, _ = lam_re.shape
    n2 = S5_CHUNK * S5_GROUP
    gs = S5_SETUP_GROUPS
    vec = pl.BlockSpec((1, 2, gs,1, S5_STATE), lambda l, g: (l, 0, g, 0, 0))
    sc = pl.BlockSpec((1, 2, gs,1, 1), lambda l, g: (l, 0, g, 0, 0))
    mat = pl.BlockSpec((1, 2, gs,S5_GROUP, S5_STATE), lambda l, g: (l, 0, g, 0, 0))
    return pl.pallas_call(
        _s5setup_kernel,
        grid=(nl, ng // gs),
        in_specs=[vec, vec, sc, mat, mat, mat, mat],
        out_specs=[pl.BlockSpec((1, 2, gs,n2, n2 + 2 * S5_STATE), lambda l, g: (l, 0, g, 0, 0)),
                   pl.BlockSpec((1, 2, gs,n2, 2 * S5_STATE), lambda l, g: (l, 0, g, 0, 0)),
                   pl.BlockSpec((1, 2, gs,1, 2 * S5_STATE), lambda l, g: (l, 0, g, 0, 0))],
        out_shape=[jax.ShapeDtypeStruct((nl, 2, ng, n2, n2 + 2 * S5_STATE), BF),
                   jax.ShapeDtypeStruct((nl, 2, ng, n2, 2 * S5_STATE), BF),
                   jax.ShapeDtypeStruct((nl, 2, ng, 1, 2 * S5_STATE), F32)],
        compiler_params=_cp(("arbitrary", "arbitrary")),
    )(lam_re.reshape(nl, 2, ng, 1, S5_STATE), lam_im.reshape(nl, 2, ng, 1, S5_STATE),
      log_dt.reshape(nl, 2, ng, 1, 1), bt_re, bt_im, c_re, c_im)


S5_GROUPS_PER_STEP = LANES // S5_GROUP


def _s5_kernel(bb, nc, u_ref, perm_ref, mw_ref, wot_ref, a16_ref, h0_ref, y_ref, hfin_ref,
               uf_ref, yf_ref):
    n = S5_CHUNK
    n2 = S5_CHUNK * S5_GROUP
    ns = S5_STATE
    rows = bb * nc
    perm = perm_ref[...]
    uf_ref[...] = u_ref[...].astype(F32)
    ucat = jnp.concatenate([uf_ref[pl.ds(tau, rows, stride=n), :] for tau in range(n)],
                           axis=1).astype(BF)
    xcat = jnp.dot(ucat, perm, preferred_element_type=F32).astype(BF)
    ridx = lax.broadcasted_iota(jnp.int32, (rows, 2 * ns), 0)
    cpos = _imod(ridx, nc)
    lane = lax.broadcasted_iota(jnp.int32, (1, 2 * ns), 1)

    def coef_pair(coef):
        sw = pltpu.roll(coef, ns, 1)
        return jnp.where(lane < ns, coef, sw), jnp.where(lane < ns, -sw, coef)

    def cmul_rows(coef, val):
        c_re, c_im = coef_pair(coef)
        return val * c_re + pltpu.roll(val, ns, 1) * c_im

    chains = [(g, d) for g in range(S5_GROUPS_PER_STEP) for d in range(2)]
    rng = range(len(chains))
    first = (0, nc - 1)
    last = (nc - 1, 0)
    yz = [jnp.dot(xcat[:, g * n2:(g + 1) * n2], mw_ref[0, d, g], preferred_element_type=F32)
          for g, d in chains]
    h0r = []
    for g, d in chains:
        z = jnp.zeros((rows, 2 * ns), F32)
        for bi in range(bb):
            z = jnp.where(ridx == bi * nc + first[d], h0_ref[bi, g, d:d + 1], z)
        h0r.append(z)
    ap = [a16_ref[0, d, g] for g, d in chains]
    e = [yz[i][:, n2:] + cmul_rows(ap[i], h0r[i]) for i in rng]
    sh = 1
    while sh < nc:
        shifted = [jnp.where(cpos >= sh, pltpu.roll(e[i], sh, 0), 0.0) if chains[i][1] == 0 else
                   jnp.where(cpos < nc - sh, pltpu.roll(e[i], rows - sh, 0), 0.0) for i in rng]
        e = [e[i] + cmul_rows(ap[i], shifted[i]) for i in rng]
        ap = [cmul_rows(ap[i], ap[i]) for i in rng]
        sh *= 2
    hs = [jnp.where(cpos >= 1, pltpu.roll(e[i], 1, 0), h0r[i]) if chains[i][1] == 0 else
          jnp.where(cpos < nc - 1, pltpu.roll(e[i], rows - 1, 0), h0r[i]) for i in rng]
    yd = [yz[i][:, :n2] + _mm_nt(hs[i], wot_ref[0, d, g]) for i, (g, d) in enumerate(chains)]
    ycat = jnp.concatenate([yd[2 * g] + yd[2 * g + 1] for g in range(S5_GROUPS_PER_STEP)], axis=1)
    ytok = lax.dot_general(ycat.astype(BF), perm, (((1,), (1,)), ((), ())), preferred_element_type=F32)
    for tau in range(n):
        yf_ref[pl.ds(tau, rows, stride=n), :] = ytok[:, tau * LANES:(tau + 1) * LANES]
    y_ref[...] = yf_ref[...].astype(y_ref.dtype)
    for i, (g, d) in enumerate(chains):
        for bi in range(bb):
            hfin_ref[bi, g, d:d + 1] = e[i][bi * nc + last[d]:bi * nc + last[d] + 1]


def _s5_scan(h2, nb, t, bb, perm, mw, wot, a16, h0, layer):
    n = S5_CHUNK
    nc = t // n
    gs = S5_GROUPS_PER_STEP
    ng = 1024 // S5_GROUP
    n2 = S5_CHUNK * S5_GROUP
    par = lambda shape: pl.BlockSpec((1, 2, gs) + shape, lambda s, b: (layer, 0, s, 0, 0))
    st = pl.BlockSpec((bb, gs, 2, 2 * S5_STATE), lambda s, b: (b, s, 0, 0))
    return pl.pallas_call(
        functools.partial(_s5_kernel, bb, nc),
        grid=(ng // gs, nb // bb),
        in_specs=[pl.BlockSpec((bb * t, LANES), lambda s, b: (b, COL_U // LANES + s)),
                  pl.BlockSpec(perm.shape, lambda s, b: (0, 0)),
                  par((n2, n2 + 2 * S5_STATE)), par((n2, 2 * S5_STATE)), par((1, 2 * S5_STATE)), st],
        out_specs=[pl.BlockSpec((bb * t, LANES), lambda s, b: (b, s)), st],
        out_shape=[jax.ShapeDtypeStruct((nb * t, 1024), BF),
                   jax.ShapeDtypeStruct((nb, ng, 2, 2 * S5_STATE), F32)],
        scratch_shapes=[pltpu.VMEM((bb * t, LANES), F32), pltpu.VMEM((bb * t, LANES), F32)],
        compiler_params=_cp(("arbitrary", "arbitrary")),
    )(h2, perm, mw, wot, a16, h0)


def _s5post_kernel(y_ref, u_ref, d_ref, w_ref, b_ref, o_ref):
    yy = y_ref[...].astype(F32) + d_ref[...] * u_ref[...].astype(F32)
    z = jax.nn.gelu(yy)
    o_ref[...] = (z * jax.nn.sigmoid(_mm(z, w_ref[...]) + b_ref[...])).astype(o_ref.dtype)


def _s5_post(y2, h2, d_skip, glu_w, glu_b, tm):
    m = y2.shape[0]
    full = lambda shape: pl.BlockSpec(shape, lambda i: (0,) * len(shape))
    return pl.pallas_call(
        _s5post_kernel,
        grid=(m // tm,),
        in_specs=[pl.BlockSpec((tm, 1024), lambda i: (i, 0)),
                  pl.BlockSpec((tm, 1024), lambda i: (i, COL_U // 1024)),
                  full((1, 1024)), full((1024, 1024)), full((1, 1024))],
        out_specs=pl.BlockSpec((tm, 1024), lambda i: (i, 0)),
        out_shape=jax.ShapeDtypeStruct((m, 1024), BF),
        compiler_params=_cp(("arbitrary",)),
    )(y2, h2, d_skip, glu_w, glu_b)


def _merge_kernel(oa_ref, ob_ref, oc_ref, g0_ref, g1_ref, g2_ref, w_ref, o_ref):
    acc = None
    for o_r, g_r, n in ((oa_ref, g0_ref, 0), (ob_ref, g1_ref, 1), (oc_ref, g2_ref, 2)):
        pr = jnp.dot(o_r[...], w_ref[n], preferred_element_type=F32)
        term = (0.5 + 0.5 * jnp.tanh(0.5 * g_r[...].astype(F32))) * pr
        acc = term if acc is None else acc + term
    o_ref[...] = acc.astype(o_ref.dtype)


def _merge(oa, ob, oc, h2, w_branch, tm):
    m = oa.shape[0]
    d = w_branch.shape[2]
    tn = 512
    seq = pl.BlockSpec((tm, 1024), lambda i, j: (i, 0))
    gate = lambda n: pl.BlockSpec((tm, tn), lambda i, j: (i, (COL_GATE + n * d) // tn + j))
    return pl.pallas_call(
        _merge_kernel,
        grid=(m // tm, d // tn),
        in_specs=[seq, seq, seq, gate(0), gate(1), gate(2),
                  pl.BlockSpec((3, 1024, tn), lambda i, j: (0, 0, j))],
        out_specs=pl.BlockSpec((tm, tn), lambda i, j: (i, j)),
        out_shape=jax.ShapeDtypeStruct((m, d), BF),
        compiler_params=_cp(("arbitrary", "arbitrary")),
    )(oa, ob, oc, h2, h2, h2, w_branch)


def _outproj_kernel(m_ref, w_ref, x_ref, gt_ref, o_ref):
    o_ref[...] = x_ref[...] + gt_ref[0, 0] * jnp.dot(m_ref[...], w_ref[...], preferred_element_type=F32)


def _outproj(mg, w_out, x2, mod, row_fn, tm):
    m, d = x2.shape
    tn = d
    return pl.pallas_call(
        _outproj_kernel,
        grid=(m // tm, d // tn),
        in_specs=[pl.BlockSpec((tm, d), lambda i, j: (i, 0)),
                  pl.BlockSpec((d, tn), lambda i, j: (0, j)),
                  pl.BlockSpec((tm, tn), lambda i, j: (i, j)),
                  pl.BlockSpec((1, 1, 1, tn), lambda i, j: (row_fn(i), 2, 0, j))],
        out_specs=pl.BlockSpec((tm, tn), lambda i, j: (i, j)),
        out_shape=jax.ShapeDtypeStruct((m, d), F32),
        compiler_params=_cp(("arbitrary", "arbitrary")),
    )(mg, w_out, x2, mod)


def _mlp_kernel(nf, x_ref, g_ref, sh_ref, sc_ref, gt_ref, w1_ref, w2_ref, o_ref, xn_ref, acc_ref):
    f = pl.program_id(1)

    @pl.when(f == 0)
    def _():
        y = _rms(x_ref[...], g_ref[...])
        xn_ref[...] = (y * (1.0 + sc_ref[0, 0]) + sh_ref[0, 0]).astype(BF)
        acc_ref[...] = jnp.zeros(acc_ref.shape, F32)

    hmid = jnp.maximum(jnp.dot(xn_ref[...], w1_ref[...], preferred_element_type=F32), 0.0)
    acc_ref[...] += jnp.dot((hmid * hmid).astype(BF), w2_ref[...], preferred_element_type=F32)

    @pl.when(f == nf - 1)
    def _():
        o_ref[...] = x_ref[...] + gt_ref[0, 0] * acc_ref[...]


def _mlp(x2, g, mod, w1, w2, row_fn, tm):
    m, d = x2.shape
    dff = w1.shape[1]
    tf = 1024
    nf = dff // tf
    modspec = lambda k: pl.BlockSpec((1, 1, 1, d), lambda i, f: (row_fn(i), k, 0, 0))
    return pl.pallas_call(
        functools.partial(_mlp_kernel, nf),
        grid=(m // tm, nf),
        in_specs=[pl.BlockSpec((tm, d), lambda i, f: (i, 0)),
                  pl.BlockSpec((1, d), lambda i, f: (0, 0)),
                  modspec(3), modspec(4), modspec(5),
                  pl.BlockSpec((d, tf), lambda i, f: (0, f)),
                  pl.BlockSpec((tf, d), lambda i, f: (f, 0))],
        out_specs=pl.BlockSpec((tm, d), lambda i, f: (i, 0)),
        out_shape=jax.ShapeDtypeStruct((m, d), F32),
        scratch_shapes=[pltpu.VMEM((tm, d), BF), pltpu.VMEM((tm, d), F32)],
        compiler_params=_cp(("arbitrary", "arbitrary")),
    )(x2, g, mod, mod, mod, w1, w2)


def _rope_tables(t):
    rows = t // GRID_W
    row = jnp.repeat(jnp.arange(rows, dtype=F32), GRID_W)
    col = jnp.tile(jnp.arange(GRID_W, dtype=F32), rows)
    n_freq = MLA_ROPE // 4
    inv = ROPE_THETA ** (-jnp.arange(n_freq, dtype=F32) / n_freq)
    ang = jnp.concatenate([row[:, None] * inv, col[:, None] * inv], axis=-1)
    cos, sin = jnp.cos(ang), jnp.sin(ang)
    pad = jnp.zeros((t, LANES - MLA_ROPE), F32)
    return (jnp.concatenate([cos, cos, pad], axis=-1), jnp.concatenate([-sin, sin, pad], axis=-1))


def _pad_lanes(g, n):
    return jnp.pad(g.reshape(1, -1), ((0, 0), (0, n - g.shape[-1])))


def _s5_perm():
    p = jnp.arange(S5_CHUNK * LANES)
    tau, g, i = p // LANES, (p % LANES) // S5_GROUP, p % S5_GROUP
    q = g * (S5_CHUNK * S5_GROUP) + tau * S5_GROUP + i
    return (q[:, None] == p[None, :]).astype(BF)


def kernel(x, c, ctx, c_ctx, ada_w, ada_b, norm1_g, norm2_g, w_in, mla_q_lora_g, mla_kv_lora_g, mla_w_uq, mla_w_ukv, mla_qn_nope_g, mla_qn_rope_g, mla_kn_nope_g, mla_kn_rope_g, rwkv_conv, rwkv_w0, rwkv_w2, rwkv_a0, rwkv_a2, rwkv_g2, rwkv_k_k, rwkv_k_a, rwkv_r_k, rwkv_ln_g, rwkv_ln_b, s5_lam_re, s5_lam_im, s5_log_dt, s5_b_re, s5_b_im, s5_c_re, s5_c_im, s5_d, s5_glu_w, s5_glu_b, w_branch, w_out, w_mlp1, w_mlp2):
    nb, t, d = x.shape
    tc = ctx.shape[1]
    depth = ada_w.shape[0]
    ctx_row = nb

    cc = jnp.concatenate([c, c_ctx[None, :], jnp.zeros((8 - nb - 1, d), F32)], axis=0)
    mod_all = _ada(cc, ada_w, ada_b)

    bt = lambda z: jnp.swapaxes(z, -1, -2)
    mw, wot, a16 = _s5_setup(s5_lam_re, s5_lam_im, s5_log_dt, bt(s5_b_re), bt(s5_b_im), s5_c_re, s5_c_im)

    cos_t, sin_t = _rope_tables(t)
    cos_c = jnp.concatenate([jnp.ones((tc, MLA_ROPE), F32), jnp.zeros((tc, LANES - MLA_ROPE), F32)], axis=-1)
    sin_c = jnp.zeros((tc, LANES), F32)
    hi = jnp.arange(HEAD_GROUP) // RWKV_HEAD
    ones_bd = (hi[:, None] == hi[None, :]).astype(BF)
    ones_bd128 = ones_bd[:LANES, :LANES]

    tm_t = min(1024, t)
    tm_c = min(1024, nb * tc)
    lat_row = lambda tm: (lambda i: i // (t // tm))
    ctx_rowf = lambda i: ctx_row

    x2 = x.reshape(nb * t, d)
    xc2 = ctx.reshape(nb * tc, d)
    s_zero = jnp.zeros((nb, 2, 1024 // HEAD_GROUP, HEAD_GROUP, HEAD_GROUP), F32)
    h_zero = jnp.zeros((nb, 1024 // S5_GROUP, 2, 2 * S5_STATE), F32)
    perm = _s5_perm()
    w_perm_all = _permute_w_in(w_in)

    for l in range(depth):
        last = l == depth - 1
        mod = mod_all[l].reshape(8, 6, 1, d)
        w_perm = w_perm_all[l]
        g1 = norm1_g[l].reshape(1, d)
        g2n = norm2_g[l].reshape(1, d)
        w_uq = jnp.pad(mla_w_uq[l].reshape(-1, N_HEADS, MLA_NOPE + MLA_ROPE),
                       ((0, 0), (0, 0), (0, 256 - MLA_NOPE - MLA_ROPE))).reshape(-1, N_HEADS * 256).astype(BF)
        w_ukv = mla_w_ukv[l].astype(BF)
        gq = mla_q_lora_g[l].reshape(1, -1)
        gkv = mla_kv_lora_g[l].reshape(1, -1)
        gqn, gqr = mla_qn_nope_g[l].reshape(1, -1), _pad_lanes(mla_qn_rope_g[l], LANES)
        gkn, gkr = mla_kn_nope_g[l].reshape(1, -1), _pad_lanes(mla_kn_rope_g[l], LANES)
        rp = dict(w0=rwkv_w0[l], w2=rwkv_w2[l].astype(BF), a0=rwkv_a0[l], a2=rwkv_a2[l].astype(BF),
                  k_a=rwkv_k_a[l].reshape(1, -1), r_k=rwkv_r_k[l].reshape(1, -1),
                  ln_g=rwkv_ln_g[l].reshape(1, -1), ln_b=rwkv_ln_b[l].reshape(1, -1),
                  g2_pad=jnp.pad(rwkv_g2[l], ((SM_GD, 512 - SM_GD - rwkv_g2.shape[1]), (0, 0))).astype(BF))
        conv_w = rwkv_conv[l]
        k_k = rwkv_k_k[l].reshape(1, -1)
        glu_w = s5_glu_w[l].astype(BF)
        glu_b = s5_glu_b[l].reshape(1, -1)
        d_skip = s5_d[l].reshape(1, -1)
        wb = w_branch[l].astype(BF)
        wo = w_out[l].astype(BF)
        w1 = w_mlp1[l].astype(BF)
        w2 = w_mlp2[l].astype(BF)

        hc = _inproj(xc2, g1, mod, w_perm, ctx_rowf, tm_c)
        k_c, v_c = _kvprep(hc, nb, tc, gkv, w_ukv, gkn, gkr, cos_c, sin_c, tc)
        rc_c, kc_c, vc_c, kk_c = _rwkv_conv(hc.reshape(nb, tc, -1), conv_w, k_k, ones_bd128)
        yf_c, yb_c, s_ctx = _rwkv_scan(rc_c, kc_c, vc_c, kk_c, hc, rp, s_zero)
        ys_c, h_ctx = _s5_scan(hc, nb, tc, nb, perm, mw, wot, a16, h_zero, l)

        ht = _inproj(x2, g1, mod, w_perm, lat_row(tm_t), tm_t)
        tq = min(512, t)
        q_t = _qprep(ht, nb, t, gq, w_uq, gqn, gqr, cos_t, sin_t, tq)
        k_t, v_t = _kvprep(ht, nb, t, gkv, w_ukv, gkn, gkr, cos_t, sin_t, tq)
        o_a = _attention(q_t, k_t, v_t, (k_c, v_c), min(256, t), min(4096, t)).reshape(nb * t, -1)

        rc_t, kc_t, vc_t, kk_t = _rwkv_conv(ht.reshape(nb, t, -1), conv_w, k_k, ones_bd128)
        yf_t, yb_t, _ = _rwkv_scan(rc_t, kc_t, vc_t, kk_t, ht, rp, s_ctx)
        flat = lambda z: z.reshape(-1, z.shape[-1])
        o_b = _rwkv_post(flat(yf_t), flat(yb_t), flat(rc_t), flat(kc_t), flat(vc_t), ht, rp, ones_bd,
                         min(512, nb * t))

        ys_t, _ = _s5_scan(ht, nb, t, 1, perm, mw, wot, a16, h_ctx, l)
        o_c = _s5_post(ys_t, ht, d_skip, glu_w, glu_b, min(512, nb * t))

        mg = _merge(o_a, o_b, o_c, ht, wb, tm_t)
        tm_m = min(512, t)
        x2 = _outproj(mg, wo, x2, mod, lat_row(tm_m), tm_m)
        x2 = _mlp(x2, g2n, mod, w1, w2, lat_row(tm_m), tm_m)

        if not last:
            q_c = _qprep(hc, nb, tc, gq, w_uq, gqn, gqr, cos_c, sin_c, tc)
            o_a_c = _attention(q_c, k_c, v_c, None, tc, tc).reshape(nb * tc, -1)
            o_b_c = _rwkv_post(flat(yf_c), flat(yb_c), flat(rc_c), flat(kc_c), flat(vc_c), hc, rp, ones_bd,
                               min(512, nb * tc))
            o_c_c = _s5_post(ys_c, hc, d_skip, glu_w, glu_b, min(512, nb * tc))
            mg_c = _merge(o_a_c, o_b_c, o_c_c, hc, wb, tm_c)
            tm_mc = min(512, nb * tc)
            xc2 = _outproj(mg_c, wo, xc2, mod, ctx_rowf, tm_mc)
            xc2 = _mlp(xc2, g2n, mod, w1, w2, ctx_rowf, tm_mc)

    return x2.reshape(nb, t, d)
```

```python
import functools
import math

import jax
import jax.numpy as jnp
from jax import lax
from jax.experimental import pallas as pl
from jax.experimental.pallas import tpu as pltpu

F32 = jnp.float32
BF = jnp.bfloat16

GRID_W = 64
N_HEADS = 8
MLA_NOPE = 128
MLA_ROPE = 64
MLA_V = 128
ROPE_THETA = 10000.0
RWKV_HEAD = 64
RWKV_GN_EPS = 64e-5
L2_EPS = 1e-12
S5_GROUP = 16
S5_STATE = 64
NORM_EPS = 1e-6

LANES = 128
VMEM_LIMIT = 48 * 1024 * 1024

COL_R, COL_K, COL_V, COL_U, COL_GATE, COL_CQ, COL_CKV, COL_SMALL = (
    0, 1024, 2048, 3072, 4096, 10240, 10752, 11264)
N_IN_PAD = 11776
SM_KR, SM_WD, SM_AD, SM_GD = 0, 64, 192, 320

RWKV_CHUNK = 64
RWKV_CHUNKS_PER_STEP = 4
HEAD_GROUP = 256
S5_CHUNK = 16
S5_SETUP_GROUPS = 8
ATTN_HEADS_PER_STEP = 2


def _cp(sem, vmem=VMEM_LIMIT):
    return pltpu.CompilerParams(dimension_semantics=sem, vmem_limit_bytes=vmem)


def _mm(a, b):
    return jnp.dot(a.astype(BF), b.astype(BF), preferred_element_type=F32)


def _mm_nt(a, b):
    return lax.dot_general(a.astype(BF), b.astype(BF), (((1,), (1,)), ((), ())),
                           preferred_element_type=F32)


def _mm_tn(a, b):
    return lax.dot_general(a.astype(BF), b.astype(BF), (((0,), (0,)), ((), ())),
                           preferred_element_type=F32)


def _idiv(x, n):
    assert n & (n - 1) == 0, n
    return lax.shift_right_logical(x, n.bit_length() - 1)


def _imod(x, n):
    assert n & (n - 1) == 0, n
    return lax.bitwise_and(x, n - 1)


def _rms(x, g, eps=NORM_EPS):
    return x * lax.rsqrt(jnp.mean(x * x, axis=-1, keepdims=True) + eps) * g


def _softplus(x):
    return jnp.maximum(x, 0.0) + jnp.log(1.0 + jnp.exp(-jnp.abs(x)))


def _ada_kernel(c_ref, w_ref, b_ref, o_ref):
    c = c_ref[...]
    s = c * jax.nn.sigmoid(c)
    o_ref[0] = _mm(s, w_ref[0]) + b_ref[0]


def _ada(cc, ada_w, ada_b):
    n_layer, d, n = ada_w.shape
    tn = 1024
    return pl.pallas_call(
        _ada_kernel,
        grid=(n_layer, n // tn),
        in_specs=[pl.BlockSpec((8, d), lambda l, j: (0, 0)),
                  pl.BlockSpec((1, d, tn), lambda l, j: (l, 0, j)),
                  pl.BlockSpec((1, 1, tn), lambda l, j: (l, 0, j))],
        out_specs=pl.BlockSpec((1, 8, tn), lambda l, j: (l, 0, j)),
        out_shape=jax.ShapeDtypeStruct((n_layer, 8, n), F32),
        compiler_params=_cp(("arbitrary", "arbitrary")),
    )(cc, ada_w, ada_b.reshape(n_layer, 1, n))


W_IN_SEGMENTS = ((1088, 3072), (4576, 1024), (5600, 6144), (0, 1024), (1024, 64), (4160, 416))


def _wperm_kernel(w_ref, o_ref):
    off = 0
    for src, n in W_IN_SEGMENTS:
        o_ref[0, :, off:off + n] = w_ref[0, :, src:src + n].astype(BF)
        off += n
    o_ref[0, :, off:] = jnp.zeros((o_ref.shape[1], o_ref.shape[2] - off), BF)


def _permute_w_in(w_in):
    nl, d, n_in = w_in.shape
    tr = 256
    return pl.pallas_call(
        _wperm_kernel,
        grid=(nl, d // tr),
        in_specs=[pl.BlockSpec((1, tr, n_in), lambda l, i: (l, i, 0))],
        out_specs=pl.BlockSpec((1, tr, N_IN_PAD), lambda l, i: (l, i, 0)),
        out_shape=jax.ShapeDtypeStruct((nl, d, N_IN_PAD), BF),
        compiler_params=_cp(("arbitrary", "arbitrary")),
    )(w_in)


def _inproj_kernel(x_ref, g_ref, sh_ref, sc_ref, w_ref, o_ref, xn_ref):
    @pl.when(pl.program_id(1) == 0)
    def _():
        y = _rms(x_ref[...], g_ref[...])
        xn_ref[...] = (y * (1.0 + sc_ref[0, 0]) + sh_ref[0, 0]).astype(BF)

    o_ref[...] = jnp.dot(xn_ref[...], w_ref[...], preferred_element_type=F32).astype(o_ref.dtype)


def _inproj(x2, g, mod, w, row_fn, tm):
    m, d = x2.shape
    n = w.shape[1]
    tn = 512
    return pl.pallas_call(
        _inproj_kernel,
        grid=(m // tm, n // tn),
        in_specs=[pl.BlockSpec((tm, d), lambda i, j: (i, 0)),
                  pl.BlockSpec((1, d), lambda i, j: (0, 0)),
                  pl.BlockSpec((1, 1, 1, d), lambda i, j: (row_fn(i), 0, 0, 0)),
                  pl.BlockSpec((1, 1, 1, d), lambda i, j: (row_fn(i), 1, 0, 0)),
                  pl.BlockSpec((d, tn), lambda i, j: (0, j))],
        out_specs=pl.BlockSpec((tm, tn), lambda i, j: (i, j)),
        out_shape=jax.ShapeDtypeStruct((m, n), BF),
        scratch_shapes=[pltpu.VMEM((tm, d), BF)],
        compiler_params=_cp(("arbitrary", "arbitrary")),
    )(x2, g, mod, mod, w)


def _rope_rotate(x, cos_t, sin_t):
    lane = lax.broadcasted_iota(jnp.int32, x.shape, 1)
    sw = jnp.where(lane < 32, pltpu.roll(x, 96, 1), pltpu.roll(x, 32, 1))
    return x * cos_t + sw * sin_t


def _qprep_kernel(scale, cq_ref, g_ref, w_ref, gn_ref, gr_ref, cos_ref, sin_ref, q_ref):
    xn = _rms(cq_ref[...].astype(F32), g_ref[...])
    q = _mm(xn, w_ref[...])
    cos_t, sin_t = cos_ref[...], sin_ref[...]
    for h in range(N_HEADS):
        nope = q[:, h * 256:h * 256 + 128]
        rp = q[:, h * 256 + 128:h * 256 + 256]
        nope = _rms(nope, gn_ref[...])
        rp = rp * lax.rsqrt(jnp.sum(rp * rp, axis=-1, keepdims=True) * (1.0 / MLA_ROPE) + NORM_EPS) * gr_ref[...]
        rp = _rope_rotate(rp, cos_t, sin_t)
        q_ref[0, h, :, 0:128] = (nope * scale).astype(BF)
        q_ref[0, h, :, 128:256] = (rp * scale).astype(BF)


def _qprep(h2, nb, t, g, w, gn, gr, cos_t, sin_t, tm):
    nt = t // tm
    scale = math.log2(math.e) / math.sqrt(MLA_NOPE + MLA_ROPE)
    return pl.pallas_call(
        functools.partial(_qprep_kernel, scale),
        grid=(nb * nt,),
        in_specs=[pl.BlockSpec((tm, 512), lambda i: (i, COL_CQ // 512)),
                  pl.BlockSpec((1, 512), lambda i: (0, 0)),
                  pl.BlockSpec((512, N_HEADS * 256), lambda i: (0, 0)),
                  pl.BlockSpec((1, 128), lambda i: (0, 0)),
                  pl.BlockSpec((1, 128), lambda i: (0, 0)),
                  pl.BlockSpec((tm, 128), lambda i: (i % nt, 0)),
                  pl.BlockSpec((tm, 128), lambda i: (i % nt, 0))],
        out_specs=pl.BlockSpec((1, N_HEADS, tm, 256), lambda i: (i // nt, 0, i % nt, 0)),
        out_shape=jax.ShapeDtypeStruct((nb, N_HEADS, t, 256), BF),
        compiler_params=_cp(("arbitrary",)),
    )(h2, g, w, gn, gr, cos_t, sin_t)


def _kvprep_kernel(ckv_ref, sm_ref, g_ref, w_ref, gn_ref, gr_ref, cos_ref, sin_ref, k_ref, v_ref):
    xn = _rms(ckv_ref[...].astype(F32), g_ref[...])
    kv = _mm(xn, w_ref[...])
    sm = sm_ref[...].astype(F32)
    lane = lax.broadcasted_iota(jnp.int32, sm.shape, 1)
    kr = jnp.where(lane < MLA_ROPE, sm, 0.0)
    kr = kr * lax.rsqrt(jnp.sum(kr * kr, axis=-1, keepdims=True) * (1.0 / MLA_ROPE) + NORM_EPS) * gr_ref[...]
    kr = _rope_rotate(kr, cos_ref[...], sin_ref[...]).astype(BF)
    ones_col = jnp.where(lane == 0, 1.0, 0.0).astype(BF)
    for h in range(N_HEADS):
        kn = _rms(kv[:, h * 256:h * 256 + 128], gn_ref[...])
        k_ref[0, h, :, 0:128] = kn.astype(BF)
        k_ref[0, h, :, 128:256] = kr
        v_ref[0, h, :, 0:128] = kv[:, h * 256 + 128:h * 256 + 256].astype(BF)
        v_ref[0, h, :, 128:256] = ones_col


def _kvprep(h2, nb, t, g, w, gn, gr, cos_t, sin_t, tm):
    nt = t // tm
    return pl.pallas_call(
        _kvprep_kernel,
        grid=(nb * nt,),
        in_specs=[pl.BlockSpec((tm, 512), lambda i: (i, COL_CKV // 512)),
                  pl.BlockSpec((tm, 128), lambda i: (i, COL_SMALL // 128)),
                  pl.BlockSpec((1, 512), lambda i: (0, 0)),
                  pl.BlockSpec((512, N_HEADS * 256), lambda i: (0, 0)),
                  pl.BlockSpec((1, 128), lambda i: (0, 0)),
                  pl.BlockSpec((1, 128), lambda i: (0, 0)),
                  pl.BlockSpec((tm, 128), lambda i: (i % nt, 0)),
                  pl.BlockSpec((tm, 128), lambda i: (i % nt, 0))],
        out_specs=[pl.BlockSpec((1, N_HEADS, tm, 256), lambda i: (i // nt, 0, i % nt, 0)),
                   pl.BlockSpec((1, N_HEADS, tm, 256), lambda i: (i // nt, 0, i % nt, 0))],
        out_shape=[jax.ShapeDtypeStruct((nb, N_HEADS, t, 256), BF),
                   jax.ShapeDtypeStruct((nb, N_HEADS, t, 256), BF)],
        compiler_params=_cp(("arbitrary",)),
    )(h2, h2, g, w, gn, gr, cos_t, sin_t)


def _attn_kernel(has_extra, nk, *refs):
    if has_extra:
        q_ref, k_ref, v_ref, kc_ref, vc_ref, o_ref, m_ref, acc_ref = refs
    else:
        q_ref, k_ref, v_ref, o_ref, m_ref, acc_ref = refs
    j = pl.program_id(3)

    @pl.when(j == 0)
    def _():
        m_ref[...] = jnp.full(m_ref.shape, -1e30, F32)
        acc_ref[...] = jnp.zeros(acc_ref.shape, F32)

    heads = range(ATTN_HEADS_PER_STEP)

    def step(k_r, v_r):
        s = [lax.dot_general(q_ref[0, h], k_r[0, h], (((1,), (1,)), ((), ())),
                             preferred_element_type=F32) for h in heads]
        m_prev = [m_ref[h] for h in heads]
        m_new = [jnp.maximum(m_prev[h], jnp.max(s[h], axis=1, keepdims=True)) for h in heads]
        alpha = [jnp.exp2(m_prev[h] - m_new[h]) for h in heads]
        p = [jnp.exp2((s[h] - m_new[h]).astype(BF)) for h in heads]
        pv = [jnp.dot(p[h], v_r[0, h], preferred_element_type=F32) for h in heads]
        for h in heads:
            acc_ref[h] = alpha[h] * acc_ref[h] + pv[h]
            m_ref[h] = m_new[h]

    step(k_ref, v_ref)

    @pl.when(j == nk - 1)
    def _():
        if has_extra:
            step(kc_ref, vc_ref)
        for h in heads:
            acc = acc_ref[h]
            o_ref[0, :, h * MLA_V:(h + 1) * MLA_V] = (
                acc[:, :MLA_V] / acc[:, MLA_V:MLA_V + 1]).astype(o_ref.dtype)


def _attention(q, k, v, extra, bq, bk):
    nb, nh, t, _ = q.shape
    tk = k.shape[2]
    nk = tk // bk
    hb = ATTN_HEADS_PER_STEP
    in_specs = [pl.BlockSpec((1, hb, bq, 256), lambda b, h, i, j: (b, h, i, 0)),
                pl.BlockSpec((1, hb, bk, 256), lambda b, h, i, j: (b, h, j, 0)),
                pl.BlockSpec((1, hb, bk, 256), lambda b, h, i, j: (b, h, j, 0))]
    args = [q, k, v]
    if extra is not None:
        kc, vc = extra
        tc = kc.shape[2]
        in_specs += [pl.BlockSpec((1, hb, tc, 256), lambda b, h, i, j: (b, h, 0, 0)),
                     pl.BlockSpec((1, hb, tc, 256), lambda b, h, i, j: (b, h, 0, 0))]
        args += [kc, vc]
    return pl.pallas_call(
        functools.partial(_attn_kernel, extra is not None, nk),
        grid=(nb, nh // hb, t // bq, nk),
        in_specs=in_specs,
        out_specs=pl.BlockSpec((1, bq, hb * MLA_V), lambda b, h, i, j: (b, i, h)),
        out_shape=jax.ShapeDtypeStruct((nb, t, nh * MLA_V), BF),
        scratch_shapes=[pltpu.VMEM((hb, bq, 1), F32), pltpu.VMEM((hb, bq, 2 * MLA_V), F32)],
        compiler_params=_cp(("arbitrary",) * 4),
    )(*args)


def _conv_kernel(r_ref, k_ref, v_ref, wr_ref, wk_ref, wv_ref, kk_w_ref, ones_ref,
                 ro_ref, ko_ref, vo_ref, kko_ref):
    t = r_ref.shape[1]
    row = lax.broadcasted_iota(jnp.int32, (t, LANES), 0)

    def conv(x_ref, w_ref):
        x = x_ref[0].astype(F32)
        xm = jnp.where(row == 0, 0.0, pltpu.roll(x, 1, 0))
        xp = jnp.where(row == t - 1, 0.0, pltpu.roll(x, t - 1, 0))
        w = w_ref[...]
        return xm * w[0:1] + x * w[1:2] + xp * w[2:3]

    ro_ref[0] = conv(r_ref, wr_ref).astype(BF)
    vo_ref[0] = conv(v_ref, wv_ref).astype(BF)
    k = conv(k_ref, wk_ref)
    ko_ref[0] = k.astype(BF)
    kk = k * kk_w_ref[...]
    ss = _mm(kk * kk, ones_ref[...])
    kko_ref[0] = (kk * lax.rsqrt(ss + L2_EPS)).astype(BF)


def _rwkv_conv(h3, conv_w, k_k, ones_bd):
    nb, t, _ = h3.shape
    nj = 1024 // LANES
    blk = lambda off: pl.BlockSpec((1, t, LANES), lambda b, j: (b, 0, off // LANES + j))
    wblk = lambda off: pl.BlockSpec((3, LANES), lambda b, j: (0, off // LANES + j))
    oblk = pl.BlockSpec((1, t, LANES), lambda b, j: (b, 0, j))
    osh = jax.ShapeDtypeStruct((nb, t, 1024), BF)
    return pl.pallas_call(
        _conv_kernel,
        grid=(nb, nj),
        in_specs=[blk(COL_R), blk(COL_K), blk(COL_V), wblk(0), wblk(1024), wblk(2048),
                  pl.BlockSpec((1, LANES), lambda b, j: (0, j)),
                  pl.BlockSpec((LANES, LANES), lambda b, j: (0, 0))],
        out_specs=[oblk, oblk, oblk, oblk],
        out_shape=[osh, osh, osh, osh],
        compiler_params=_cp(("arbitrary", "arbitrary")),
    )(h3, h3, h3, conv_w, conv_w, conv_w, k_k, ones_bd)


def _bd_rows(x, bdmask):
    reps = HEAD_GROUP // x.shape[0]
    return jnp.where(bdmask, jnp.concatenate([x] * reps, axis=0), 0.0)


def _scan_prep(d, j, r_ref, k_ref, v_ref, kk_ref, sm_ref, w0_ref, w2_ref, a0_ref, a2_ref, ka_ref, tri):
    L = RWKV_CHUNK
    rows = slice(j * L, (j + 1) * L)
    r = r_ref[0, rows].astype(F32)
    k = k_ref[0, rows].astype(F32)
    v = v_ref[0, rows].astype(F32)
    kk = kk_ref[0, rows].astype(F32)
    sm = sm_ref[rows].astype(F32)
    wd = sm[:, SM_WD + 64 * d:SM_WD + 64 * d + 64]
    ad = sm[:, SM_AD + 64 * d:SM_AD + 64 * d + 64]
    w_log = -_softplus(-(w0_ref[d:d + 1] + _mm(jnp.tanh(wd), w2_ref[d]))) - 0.5
    lw = -jnp.exp(w_log)
    a = jax.nn.sigmoid(a0_ref[d:d + 1] + _mm(ad, a2_ref[d]))
    b = kk * a
    krep = k * (1.0 + (a - 1.0) * ka_ref[...])
    cl = jnp.dot(tri, lw, precision=lax.Precision.HIGHEST, preferred_element_type=F32)
    cl_last = cl[0:1] if d == 1 else cl[L - 1:L]
    e_neg = jnp.exp(-cl)
    e_last = jnp.exp(cl_last - cl)
    at = -(kk * jnp.exp(cl - lw))
    rt = r * jnp.exp(cl)
    bt = b * e_neg
    kt = krep * e_neg
    bh = b * e_last
    kh = krep * e_last
    return dict(at=at, rt=rt, bt=bt, kt=kt, bh=bh, kh=kh, v=v, p_last=jnp.exp(cl_last))


def _scan_kernel(nc, rf, kf, vf, kkf, smf, rb, kb, vb, kkb, smb, w0, w2, a0, a2, ka, s_in,
                 yf, yb, s_out, s_ref):
    c = pl.program_id(1)

    @pl.when(c == 0)
    def _():
        s_ref[...] = s_in[0]

    L = RWKV_CHUNK
    row2 = lax.broadcasted_iota(jnp.int32, (HEAD_GROUP, HEAD_GROUP), 0)
    col2 = lax.broadcasted_iota(jnp.int32, (HEAD_GROUP, HEAD_GROUP), 1)
    bdmask = _idiv(row2, RWKV_HEAD) == _idiv(col2, RWKV_HEAD)
    tt = lax.broadcasted_iota(jnp.int32, (L, HEAD_GROUP), 0)
    ss = _imod(lax.broadcasted_iota(jnp.int32, (L, HEAD_GROUP), 1), L)
    rl = lax.broadcasted_iota(jnp.int32, (L, L), 0)
    cl = lax.broadcasted_iota(jnp.int32, (L, L), 1)
    blk = tuple(_idiv(ss, n) == _idiv(tt, n) for n in (8, 16, 32, 64))
    eye = jnp.where(ss == tt, 1.0, 0.0)
    strict = (ss < tt, ss > tt)
    incl = (ss <= tt, ss >= tt)
    cps = RWKV_CHUNKS_PER_STEP
    tri = ((rl >= cl).astype(F32), (rl <= cl).astype(F32))
    in_refs = ((rf, kf, vf, kkf, smf), (rb, kb, vb, kkb, smb))
    sub = lambda d, p: p if d == 0 else cps - 1 - p
    prep = {(d, p): _scan_prep(d, sub(d, p), *in_refs[d], w0, w2, a0, a2, ka, tri[d])
            for d in range(2) for p in range(cps)}
    y_refs = (yf, yb)

    chains = [(d, p, gi) for p in range(cps) for d in range(2) for gi in range(1024 // HEAD_GROUP)]
    n = len(chains)
    bd = lambda x: _bd_rows(x, bdmask)
    cat = jnp.concatenate
    g_ = lambda name: [prep[d, p][name][:, gi * HEAD_GROUP:(gi + 1) * HEAD_GROUP] for d, p, gi in chains]
    at, rt, bt, kt, bh, kh, v = (g_(nm) for nm in ("at", "rt", "bt", "kt", "bh", "kh", "v"))
    ac = [_mm_nt(cat([at[i], rt[i]], axis=0), cat([bd(bt[i]), bd(kt[i])], axis=0)) for i in range(n)]
    a_ab = [jnp.where(strict[chains[i][0]], ac[i][:L, :HEAD_GROUP], 0.0) for i in range(n)]
    a_ak = [jnp.where(strict[chains[i][0]], ac[i][:L, HEAD_GROUP:], 0.0) for i in range(n)]
    c_b = [jnp.where(incl[chains[i][0]], ac[i][L:, :HEAD_GROUP], 0.0) for i in range(n)]
    c_k = [jnp.where(incl[chains[i][0]], ac[i][L:, HEAD_GROUP:], 0.0) for i in range(n)]
    av = [_mm(a_ak[i], bd(v[i])) for i in range(n)]
    a8 = [jnp.where(blk[0], a_ab[i], 0.0) for i in range(n)]
    p2 = [_mm(a8[i], bd(a8[i])) for i in range(n)]
    tm = [eye + a8[i] for i in range(n)]
    pt = [_mm(cat([p2[i], tm[i]], axis=0), bd(p2[i])) for i in range(n)]
    tm = [tm[i] + pt[i][L:] for i in range(n)]
    tm = [tm[i] + _mm(tm[i], bd(pt[i][:L])) for i in range(n)]
    for lvl in range(3):
        off_mask = blk[lvl + 1] & ~blk[lvl]
        z = [_mm(jnp.where(off_mask, a_ab[i], 0.0), bd(tm[i])) for i in range(n)]
        tm = [tm[i] + _mm(tm[i], bd(z[i])) for i in range(n)]
    wu = [_mm(tm[i], cat([bd(at[i]), bd(av[i])], axis=1)) for i in range(n)]
    w = [wu[i][:, :HEAD_GROUP] for i in range(n)]
    uv = [wu[i][:, HEAD_GROUP:] for i in range(n)]
    q = [rt[i] + _mm(c_b[i], bd(w[i])) for i in range(n)]
    y0 = [_mm(cat([c_b[i], c_k[i]], axis=1), cat([bd(uv[i]), bd(v[i])], axis=0)) for i in range(n)]
    zeros = jnp.zeros((L, HEAD_GROUP), F32)
    gh = [_mm_tn(cat([cat([w[i], uv[i]], axis=1), cat([zeros, v[i]], axis=1)], axis=0),
                 cat([bh[i], kh[i]], axis=0)) for i in range(n)]
    for i, (d, p, gi) in enumerate(chains):
        sl = slice(gi * HEAD_GROUP, (gi + 1) * HEAD_GROUP)
        rows = slice(sub(d, p) * L, (sub(d, p) + 1) * L)
        s0 = s_ref[d, gi]
        y_refs[d][0, rows, sl] = y0[i] + _mm_nt(q[i], s0)
        g = jnp.where(bdmask, gh[i][:HEAD_GROUP], 0.0)
        hh = jnp.where(bdmask, gh[i][HEAD_GROUP:], 0.0)
        s_ref[d, gi] = s0 * prep[d, p]["p_last"][:, sl] + _mm(s0, g) + hh

    @pl.when(c == nc - 1)
    def _():
        s_out[0] = s_ref[...]


def _rwkv_scan(rc, kc, vc, kk, h2, p, s_in):
    nb, t, _ = rc.shape
    L = RWKV_CHUNK * RWKV_CHUNKS_PER_STEP
    assert t % L == 0, (t, L)
    nc = t // L
    fw = lambda b, c: (b, c, 0)
    bw = lambda b, c: (b, nc - 1 - c, 0)
    seq = lambda im: pl.BlockSpec((1, L, 1024), im)
    smf = pl.BlockSpec((L, 512), lambda b, c: (b * nc + c, COL_SMALL // 512))
    smb = pl.BlockSpec((L, 512), lambda b, c: (b * nc + nc - 1 - c, COL_SMALL // 512))
    full = lambda shape: pl.BlockSpec(shape, lambda b, c: (0,) * len(shape))
    st = pl.BlockSpec((1, 2, 4, HEAD_GROUP, HEAD_GROUP), lambda b, c: (b, 0, 0, 0, 0))
    ysh = jax.ShapeDtypeStruct((nb, t, 1024), F32)
    return pl.pallas_call(
        functools.partial(_scan_kernel, nc),
        grid=(nb, nc),
        in_specs=[seq(fw), seq(fw), seq(fw), seq(fw), smf, seq(bw), seq(bw), seq(bw), seq(bw), smb,
                  full((2, 1024)), full((2, 64, 1024)), full((2, 1024)), full((2, 64, 1024)),
                  full((1, 1024)), st],
        out_specs=[seq(fw), seq(bw), st],
        out_shape=[ysh, ysh, jax.ShapeDtypeStruct(s_in.shape, F32)],
        scratch_shapes=[pltpu.VMEM((2, 4, HEAD_GROUP, HEAD_GROUP), F32)],
        compiler_params=_cp(("arbitrary", "arbitrary")),
    )(rc, kc, vc, kk, h2, rc, kc, vc, kk, h2, p["w0"], p["w2"], p["a0"], p["a2"], p["k_a"], s_in)


def _rpost_kernel(yf_ref, yb_ref, r_ref, k_ref, v_ref, sm_ref, a0_ref, a2_ref, ka_ref, rk_ref,
                  lng_ref, lnb_ref, g2_ref, ones_ref, o_ref):
    y = yf_ref[...] + yb_ref[...]
    r = r_ref[...].astype(F32)
    k = k_ref[...].astype(F32)
    v = v_ref[...].astype(F32)
    sm = sm_ref[...].astype(F32)
    a_sum = 0.0
    for d in range(2):
        ad = sm[:, SM_AD + 64 * d:SM_AD + 64 * d + 64]
        a_sum = a_sum + jax.nn.sigmoid(a0_ref[d:d + 1] + _mm(ad, a2_ref[d]))
    kb = k * (1.0 + (0.5 * a_sum - 1.0) * ka_ref[...])
    rkb = r * kb * rk_ref[...]
    gate = _mm(jax.nn.sigmoid(sm), g2_ref[...])
    ones = ones_ref[...]
    inv_n = 1.0 / RWKV_HEAD
    for gi in range(1024 // HEAD_GROUP):
        sl = slice(gi * HEAD_GROUP, (gi + 1) * HEAD_GROUP)
        yg = y[:, sl]
        mu = _mm(yg, ones) * inv_n
        dlt = yg - mu
        var = _mm(dlt * dlt, ones) * inv_n
        yn = dlt * lax.rsqrt(var + RWKV_GN_EPS) * lng_ref[:, sl] + lnb_ref[:, sl]
        bonus = _mm(rkb[:, sl], ones) * v[:, sl]
        o_ref[:, sl] = ((yn + bonus) * gate[:, sl]).astype(o_ref.dtype)


def _rwkv_post(yf, yb, rc, kc, vc, h2, p, ones_bd, tm):
    m = yf.shape[0]
    row = lambda i: (i, 0)
    full = lambda shape: pl.BlockSpec(shape, lambda i: (0,) * len(shape))
    seq = pl.BlockSpec((tm, 1024), row)
    return pl.pallas_call(
        _rpost_kernel,
        grid=(m // tm,),
        in_specs=[seq, seq, seq, seq, seq,
                  pl.BlockSpec((tm, 512), lambda i: (i, COL_SMALL // 512)),
                  full((2, 1024)), full((2, 64, 1024)), full((1, 1024)), full((1, 1024)),
                  full((1, 1024)), full((1, 1024)), full((512, 1024)),
                  full((HEAD_GROUP, HEAD_GROUP))],
        out_specs=seq,
        out_shape=jax.ShapeDtypeStruct((m, 1024), BF),
        compiler_params=_cp(("arbitrary",)),
    )(yf, yb, rc, kc, vc, h2, p["a0"], p["a2"], p["k_a"], p["r_k"], p["ln_g"], p["ln_b"],
      p["g2_pad"], ones_bd)


def _cmul(x, y):
    return x[0] * y[0] - x[1] * y[1], x[0] * y[1] + x[1] * y[0]


def _s5setup_kernel(lre_ref, lim_ref, ldt_ref, btr_ref, bti_ref, cr_ref, ci_ref,
                    mw_ref, wot_ref, a16_ref):
    n = S5_CHUNK
    half = n // 2
    rt = _idiv(lax.broadcasted_iota(jnp.int32, (n * S5_GROUP, n * S5_GROUP), 0), S5_GROUP)
    ct = _idiv(lax.broadcasted_iota(jnp.int32, (n * S5_GROUP, n * S5_GROUP), 1), S5_GROUP)
    for gi, d in [(gi, d) for gi in range(S5_SETUP_GROUPS) for d in range(2)]:
        lr = lre_ref[0, d, gi]
        li = lim_ref[0, d, gi]
        dt = jnp.exp(ldt_ref[0, d, gi])

        def cexp(mult):
            mag = jnp.exp(mult * lr * dt)
            ang = mult * li * dt
            return mag * jnp.cos(ang), mag * jnp.sin(ang)

        a1 = cexp(1.0)
        den = lr * lr + li * li
        q_re = ((a1[0] - 1.0) * lr + a1[1] * li) / den
        q_im = (a1[1] * lr - (a1[0] - 1.0) * li) / den
        bb = (q_re * btr_ref[0, d, gi] - q_im * bti_ref[0, d, gi],
              q_re * bti_ref[0, d, gi] + q_im * btr_ref[0, d, gi])
        cc = (cr_ref[0, d, gi], ci_ref[0, d, gi])
        pw = {0: (jnp.ones_like(lr), jnp.zeros_like(lr)), 1: a1, -1: cexp(-1.0)}
        for m in range(2, n + 1):
            pw[m] = _cmul(pw[m - 1], pw[1])
        for m in range(2, half + 1):
            pw[-m] = _cmul(pw[-m + 1], pw[-1])

        def rows(base, efn):
            parts = [_cmul(base, pw[efn(t)]) for t in range(n)]
            return (jnp.concatenate([z[0] for z in parts], axis=0),
                    jnp.concatenate([z[1] for z in parts], axis=0))

        if d == 0:
            e_pow, f_pow = (lambda t: half - t), (lambda t: t - half)
            wi_pow, wo_pow = (lambda t: n - 1 - t), (lambda t: t + 1)
            mask = ct >= rt
        else:
            e_pow, f_pow = (lambda t: t - half), (lambda t: half - t)
            wi_pow, wo_pow = (lambda t: t), (lambda t: n - t)
            mask = ct <= rt
        e = rows(bb, e_pow)
        f = rows(cc, f_pow)
        mmat = lax.dot_general(jnp.concatenate([e[0], -e[1]], axis=1),
                               jnp.concatenate([f[0], f[1]], axis=1),
                               (((1,), (1,)), ((), ())),
                               precision=lax.Precision.HIGHEST, preferred_element_type=F32)
        mmat = jnp.where(mask, mmat, 0.0)
        wi = rows(bb, wi_pow)
        wo = rows(cc, wo_pow)
        mw_ref[0, d, gi] = jnp.concatenate([mmat, wi[0], wi[1]], axis=1).astype(BF)
        wot_ref[0, d, gi] = jnp.concatenate([wo[0], -wo[1]], axis=1).astype(BF)
        a16_ref[0, d, gi] = jnp.concatenate([pw[n][0], pw[n][1]], axis=1)


def _s5_setup(lam_re, lam_im, log_dt, bt_re, bt_im, c_re, c_im):
    nl, _, ng, _ = lam_re.shape
    n2 = S5_CHUNK * S5_GROUP
    gs = S5_SETUP_GROUPS
    vec = pl.BlockSpec((1, 2, gs,1, S5_STATE), lambda l, g: (l, 0, g, 0, 0))
    sc = pl.BlockSpec((1, 2, gs,1, 1), lambda l, g: (l, 0, g, 0, 0))
    mat = pl.BlockSpec((1, 2, gs,S5_GROUP, S5_STATE), lambda l, g: (l, 0, g, 0, 0))
    return pl.pallas_call(
        _s5setup_kernel,
        grid=(nl, ng // gs),
        in_specs=[vec, vec, sc, mat, mat, mat, mat],
        out_specs=[pl.BlockSpec((1, 2, gs,n2, n2 + 2 * S5_STATE), lambda l, g: (l, 0, g, 0, 0)),
                   pl.BlockSpec((1, 2, gs,n2, 2 * S5_STATE), lambda l, g: (l, 0, g, 0, 0)),
                   pl.BlockSpec((1, 2, gs,1, 2 * S5_STATE), lambda l, g: (l, 0, g, 0, 0))],
        out_shape=[jax.ShapeDtypeStruct((nl, 2, ng, n2, n2 + 2 * S5_STATE), BF),
                   jax.ShapeDtypeStruct((nl, 2, ng, n2, 2 * S5_STATE), BF),
                   jax.ShapeDtypeStruct((nl, 2, ng, 1, 2 * S5_STATE), F32)],
        compiler_params=_cp(("arbitrary", "arbitrary")),
    )(lam_re.reshape(nl, 2, ng, 1, S5_STATE), lam_im.reshape(nl, 2, ng, 1, S5_STATE),
      log_dt.reshape(nl, 2, ng, 1, 1), bt_re, bt_im, c_re, c_im)


S5_GROUPS_PER_STEP = LANES // S5_GROUP


def _s5_kernel(bb, nc, u_ref, perm_ref, mw_ref, wot_ref, a16_ref, h0_ref, y_ref, hfin_ref,
               uf_ref, yf_ref):
    n = S5_CHUNK
    n2 = S5_CHUNK * S5_GROUP
    ns = S5_STATE
    rows = bb * nc
    perm = perm_ref[...]
    uf_ref[...] = u_ref[...].astype(F32)
    ucat = jnp.concatenate([uf_ref[pl.ds(tau, rows, stride=n), :] for tau in range(n)],
                           axis=1).astype(BF)
    xcat = jnp.dot(ucat, perm, preferred_element_type=F32).astype(BF)
    ridx = lax.broadcasted_iota(jnp.int32, (rows, 2 * ns), 0)
    cpos = _imod(ridx, nc)
    lane = lax.broadcasted_iota(jnp.int32, (1, 2 * ns), 1)

    def coef_pair(coef):
        sw = pltpu.roll(coef, ns, 1)
        return jnp.where(lane < ns, coef, sw), jnp.where(lane < ns, -sw, coef)

    def cmul_rows(coef, val):
        c_re, c_im = coef_pair(coef)
        return val * c_re + pltpu.roll(val, ns, 1) * c_im

    chains = [(g, d) for g in range(S5_GROUPS_PER_STEP) for d in range(2)]
    rng = range(len(chains))
    first = (0, nc - 1)
    last = (nc - 1, 0)
    yz = [jnp.dot(xcat[:, g * n2:(g + 1) * n2], mw_ref[0, d, g], preferred_element_type=F32)
          for g, d in chains]
    h0r = []
    for g, d in chains:
        z = jnp.zeros((rows, 2 * ns), F32)
        for bi in range(bb):
            z = jnp.where(ridx == bi * nc + first[d], h0_ref[bi, g, d:d + 1], z)
        h0r.append(z)
    ap = [a16_ref[0, d, g] for g, d in chains]
    e = [yz[i][:, n2:] + cmul_rows(ap[i], h0r[i]) for i in rng]
    sh = 1
    while sh < nc:
        shifted = [jnp.where(cpos >= sh, pltpu.roll(e[i], sh, 0), 0.0) if chains[i][1] == 0 else
                   jnp.where(cpos < nc - sh, pltpu.roll(e[i], rows - sh, 0), 0.0) for i in rng]
        e = [e[i] + cmul_rows(ap[i], shifted[i]) for i in rng]
        ap = [cmul_rows(ap[i], ap[i]) for i in rng]
        sh *= 2
    hs = [jnp.where(cpos >= 1, pltpu.roll(e[i], 1, 0), h0r[i]) if chains[i][1] == 0 else
          jnp.where(cpos < nc - 1, pltpu.roll(e[i], rows - 1, 0), h0r[i]) for i in rng]
    yd = [yz[i][:, :n2] + _mm_nt(hs[i], wot_ref[0, d, g]) for i, (g, d) in enumerate(chains)]
    ycat = jnp.concatenate([yd[2 * g] + yd[2 * g + 1] for g in range(S5_GROUPS_PER_STEP)], axis=1)
    ytok = lax.dot_general(ycat.astype(BF), perm, (((1,), (1,)), ((), ())), preferred_element_type=F32)
    for tau in range(n):
        yf_ref[pl.ds(tau, rows, stride=n), :] = ytok[:, tau * LANES:(tau + 1) * LANES]
    y_ref[...] = yf_ref[...].astype(y_ref.dtype)
    for i, (g, d) in enumerate(chains):
        for bi in range(bb):
            hfin_ref[bi, g, d:d + 1] = e[i][bi * nc + last[d]:bi * nc + last[d] + 1]


def _s5_scan(h2, nb, t, bb, perm, mw, wot, a16, h0, layer):
    n = S5_CHUNK
    nc = t // n
    gs = S5_GROUPS_PER_STEP
    ng = 1024 // S5_GROUP
    n2 = S5_CHUNK * S5_GROUP
    par = lambda shape: pl.BlockSpec((1, 2, gs) + shape, lambda s, b: (layer, 0, s, 0, 0))
    st = pl.BlockSpec((bb, gs, 2, 2 * S5_STATE), lambda s, b: (b, s, 0, 0))
    return pl.pallas_call(
        functools.partial(_s5_kernel, bb, nc),
        grid=(ng // gs, nb // bb),
        in_specs=[pl.BlockSpec((bb * t, LANES), lambda s, b: (b, COL_U // LANES + s)),
                  pl.BlockSpec(perm.shape, lambda s, b: (0, 0)),
                  par((n2, n2 + 2 * S5_STATE)), par((n2, 2 * S5_STATE)), par((1, 2 * S5_STATE)), st],
        out_specs=[pl.BlockSpec((bb * t, LANES), lambda s, b: (b, s)), st],
        out_shape=[jax.ShapeDtypeStruct((nb * t, 1024), BF),
                   jax.ShapeDtypeStruct((nb, ng, 2, 2 * S5_STATE), F32)],
        scratch_shapes=[pltpu.VMEM((bb * t, LANES), F32), pltpu.VMEM((bb * t, LANES), F32)],
        compiler_params=_cp(("arbitrary", "arbitrary")),
    )(h2, perm, mw, wot, a16, h0)


def _s5post_kernel(y_ref, u_ref, d_ref, w_ref, b_ref, o_ref):
    yy = y_ref[...].astype(F32) + d_ref[...] * u_ref[...].astype(F32)
    z = jax.nn.gelu(yy)
    o_ref[...] = (z * jax.nn.sigmoid(_mm(z, w_ref[...]) + b_ref[...])).astype(o_ref.dtype)


def _s5_post(y2, h2, d_skip, glu_w, glu_b, tm):
    m = y2.shape[0]
    full = lambda shape: pl.BlockSpec(shape, lambda i: (0,) * len(shape))
    return pl.pallas_call(
        _s5post_kernel,
        grid=(m // tm,),
        in_specs=[pl.BlockSpec((tm, 1024), lambda i: (i, 0)),
                  pl.BlockSpec((tm, 1024), lambda i: (i, COL_U // 1024)),
                  full((1, 1024)), full((1024, 1024)), full((1, 1024))],
        out_specs=pl.BlockSpec((tm, 1024), lambda i: (i, 0)),
        out_shape=jax.ShapeDtypeStruct((m, 1024), BF),
        compiler_params=_cp(("arbitrary",)),
    )(y2, h2, d_skip, glu_w, glu_b)


def _merge_kernel(oa_ref, ob_ref, oc_ref, g0_ref, g1_ref, g2_ref, w_ref, o_ref):
    acc = None
    for o_r, g_r, n in ((oa_ref, g0_ref, 0), (ob_ref, g1_ref, 1), (oc_ref, g2_ref, 2)):
        pr = jnp.dot(o_r[...], w_ref[n], preferred_element_type=F32)
        term = (0.5 + 0.5 * jnp.tanh(0.5 * g_r[...].astype(F32))) * pr
        acc = term if acc is None else acc + term
    o_ref[...] = acc.astype(o_ref.dtype)


def _merge(oa, ob, oc, h2, w_branch, tm):
    m = oa.shape[0]
    d = w_branch.shape[2]
    tn = 512
    seq = pl.BlockSpec((tm, 1024), lambda i, j: (i, 0))
    gate = lambda n: pl.BlockSpec((tm, tn), lambda i, j: (i, (COL_GATE + n * d) // tn + j))
    return pl.pallas_call(
        _merge_kernel,
        grid=(m // tm, d // tn),
        in_specs=[seq, seq, seq, gate(0), gate(1), gate(2),
                  pl.BlockSpec((3, 1024, tn), lambda i, j: (0, 0, j))],
        out_specs=pl.BlockSpec((tm, tn), lambda i, j: (i, j)),
        out_shape=jax.ShapeDtypeStruct((m, d), BF),
        compiler_params=_cp(("arbitrary", "arbitrary")),
    )(oa, ob, oc, h2, h2, h2, w_branch)


def _outproj_kernel(m_ref, w_ref, x_ref, gt_ref, o_ref):
    o_ref[...] = x_ref[...] + gt_ref[0, 0] * jnp.dot(m_ref[...], w_ref[...], preferred_element_type=F32)


def _outproj(mg, w_out, x2, mod, row_fn, tm):
    m, d = x2.shape
    tn = d
    return pl.pallas_call(
        _outproj_kernel,
        grid=(m // tm, d // tn),
        in_specs=[pl.BlockSpec((tm, d), lambda i, j: (i, 0)),
                  pl.BlockSpec((d, tn), lambda i, j: (0, j)),
                  pl.BlockSpec((tm, tn), lambda i, j: (i, j)),
                  pl.BlockSpec((1, 1, 1, tn), lambda i, j: (row_fn(i), 2, 0, j))],
        out_specs=pl.BlockSpec((tm, tn), lambda i, j: (i, j)),
        out_shape=jax.ShapeDtypeStruct((m, d), F32),
        compiler_params=_cp(("arbitrary", "arbitrary")),
    )(mg, w_out, x2, mod)


def _mlp_kernel(nf, x_ref, g_ref, sh_ref, sc_ref, gt_ref, w1_ref, w2_ref, o_ref, xn_ref, acc_ref):
    f = pl.program_id(1)

    @pl.when(f == 0)
    def _():
        y = _rms(x_ref[...], g_ref[...])
        xn_ref[...] = (y * (1.0 + sc_ref[0, 0]) + sh_ref[0, 0]).astype(BF)
        acc_ref[...] = jnp.zeros(acc_ref.shape, F32)

    hmid = jnp.maximum(jnp.dot(xn_ref[...], w1_ref[...], preferred_element_type=F32), 0.0)
    acc_ref[...] += jnp.dot((hmid * hmid).astype(BF), w2_ref[...], preferred_element_type=F32)

    @pl.when(f == nf - 1)
    def _():
        o_ref[...] = x_ref[...] + gt_ref[0, 0] * acc_ref[...]


def _mlp(x2, g, mod, w1, w2, row_fn, tm):
    m, d = x2.shape
    dff = w1.shape[1]
    tf = 1024
    nf = dff // tf
    modspec = lambda k: pl.BlockSpec((1, 1, 1, d), lambda i, f: (row_fn(i), k, 0, 0))
    return pl.pallas_call(
        functools.partial(_mlp_kernel, nf),
        grid=(m // tm, nf),
        in_specs=[pl.BlockSpec((tm, d), lambda i, f: (i, 0)),
                  pl.BlockSpec((1, d), lambda i, f: (0, 0)),
                  modspec(3), modspec(4), modspec(5),
                  pl.BlockSpec((d, tf), lambda i, f: (0, f)),
                  pl.BlockSpec((tf, d), lambda i, f: (f, 0))],
        out_specs=pl.BlockSpec((tm, d), lambda i, f: (i, 0)),
        out_shape=jax.ShapeDtypeStruct((m, d), F32),
        scratch_shapes=[pltpu.VMEM((tm, d), BF), pltpu.VMEM((tm, d), F32)],
        compiler_params=_cp(("arbitrary", "arbitrary")),
    )(x2, g, mod, mod, mod, w1, w2)


def _rope_tables(t):
    rows = t // GRID_W
    row = jnp.repeat(jnp.arange(rows, dtype=F32), GRID_W)
    col = jnp.tile(jnp.arange(GRID_W, dtype=F32), rows)
    n_freq = MLA_ROPE // 4
    inv = ROPE_THETA ** (-jnp.arange(n_freq, dtype=F32) / n_freq)
    ang = jnp.concatenate([row[:, None] * inv, col[:, None] * inv], axis=-1)
    cos, sin = jnp.cos(ang), jnp.sin(ang)
    pad = jnp.zeros((t, LANES - MLA_ROPE), F32)
    return (jnp.concatenate([cos, cos, pad], axis=-1), jnp.concatenate([-sin, sin, pad], axis=-1))


def _pad_lanes(g, n):
    return jnp.pad(g.reshape(1, -1), ((0, 0), (0, n - g.shape[-1])))


def _s5_perm():
    p = jnp.arange(S5_CHUNK * LANES)
    tau, g, i = p // LANES, (p % LANES) // S5_GROUP, p % S5_GROUP
    q = g * (S5_CHUNK * S5_GROUP) + tau * S5_GROUP + i
    return (q[:, None] == p[None, :]).astype(BF)


def kernel(x, c, ctx, c_ctx, ada_w, ada_b, norm1_g, norm2_g, w_in, mla_q_lora_g, mla_kv_lora_g, mla_w_uq, mla_w_ukv, mla_qn_nope_g, mla_qn_rope_g, mla_kn_nope_g, mla_kn_rope_g, rwkv_conv, rwkv_w0, rwkv_w2, rwkv_a0, rwkv_a2, rwkv_g2, rwkv_k_k, rwkv_k_a, rwkv_r_k, rwkv_ln_g, rwkv_ln_b, s5_lam_re, s5_lam_im, s5_log_dt, s5_b_re, s5_b_im, s5_c_re, s5_c_im, s5_d, s5_glu_w, s5_glu_b, w_branch, w_out, w_mlp1, w_mlp2):
    nb, t, d = x.shape
    tc = ctx.shape[1]
    depth = ada_w.shape[0]
    ctx_row = nb

    cc = jnp.concatenate([c, c_ctx[None, :], jnp.zeros((8 - nb - 1, d), F32)], axis=0)
    mod_all = _ada(cc, ada_w, ada_b)

    bt = lambda z: jnp.swapaxes(z, -1, -2)
    mw, wot, a16 = _s5_setup(s5_lam_re, s5_lam_im, s5_log_dt, bt(s5_b_re), bt(s5_b_im), s5_c_re, s5_c_im)

    cos_t, sin_t = _rope_tables(t)
    cos_c = jnp.concatenate([jnp.ones((tc, MLA_ROPE), F32), jnp.zeros((tc, LANES - MLA_ROPE), F32)], axis=-1)
    sin_c = jnp.zeros((tc, LANES), F32)
    hi = jnp.arange(HEAD_GROUP) // RWKV_HEAD
    ones_bd = (hi[:, None] == hi[None, :]).astype(BF)
    ones_bd128 = ones_bd[:LANES, :LANES]

    tm_t = min(1024, t)
    tm_c = min(1024, nb * tc)
    lat_row = lambda tm: (lambda i: i // (t // tm))
    ctx_rowf = lambda i: ctx_row

    x2 = x.reshape(nb * t, d)
    xc2 = ctx.reshape(nb * tc, d)
    s_zero = jnp.zeros((nb, 2, 1024 // HEAD_GROUP, HEAD_GROUP, HEAD_GROUP), F32)
    h_zero = jnp.zeros((nb, 1024 // S5_GROUP, 2, 2 * S5_STATE), F32)
    perm = _s5_perm()
    w_perm_all = _permute_w_in(w_in)

    for l in range(depth):
        last = l == depth - 1
        mod = mod_all[l].reshape(8, 6, 1, d)
        w_perm = w_perm_all[l]
        g1 = norm1_g[l].reshape(1, d)
        g2n = norm2_g[l].reshape(1, d)
        w_uq = jnp.pad(mla_w_uq[l].reshape(-1, N_HEADS, MLA_NOPE + MLA_ROPE),
                       ((0, 0), (0, 0), (0, 256 - MLA_NOPE - MLA_ROPE))).reshape(-1, N_HEADS * 256).astype(BF)
        w_ukv = mla_w_ukv[l].astype(BF)
        gq = mla_q_lora_g[l].reshape(1, -1)
        gkv = mla_kv_lora_g[l].reshape(1, -1)
        gqn, gqr = mla_qn_nope_g[l].reshape(1, -1), _pad_lanes(mla_qn_rope_g[l], LANES)
        gkn, gkr = mla_kn_nope_g[l].reshape(1, -1), _pad_lanes(mla_kn_rope_g[l], LANES)
        rp = dict(w0=rwkv_w0[l], w2=rwkv_w2[l].astype(BF), a0=rwkv_a0[l], a2=rwkv_a2[l].astype(BF),
                  k_a=rwkv_k_a[l].reshape(1, -1), r_k=rwkv_r_k[l].reshape(1, -1),
                  ln_g=rwkv_ln_g[l].reshape(1, -1), ln_b=rwkv_ln_b[l].reshape(1, -1),
                  g2_pad=jnp.pad(rwkv_g2[l], ((SM_GD, 512 - SM_GD - rwkv_g2.shape[1]), (0, 0))).astype(BF))
        conv_w = rwkv_conv[l]
        k_k = rwkv_k_k[l].reshape(1, -1)
        glu_w = s5_glu_w[l].astype(BF)
        glu_b = s5_glu_b[l].reshape(1, -1)
        d_skip = s5_d[l].reshape(1, -1)
        wb = w_branch[l].astype(BF)
        wo = w_out[l].astype(BF)
        w1 = w_mlp1[l].astype(BF)
        w2 = w_mlp2[l].astype(BF)

        hc = _inproj(xc2, g1, mod, w_perm, ctx_rowf, tm_c)
        k_c, v_c = _kvprep(hc, nb, tc, gkv, w_ukv, gkn, gkr, cos_c, sin_c, tc)
        rc_c, kc_c, vc_c, kk_c = _rwkv_conv(hc.reshape(nb, tc, -1), conv_w, k_k, ones_bd128)
        yf_c, yb_c, s_ctx = _rwkv_scan(rc_c, kc_c, vc_c, kk_c, hc, rp, s_zero)
        ys_c, h_ctx = _s5_scan(hc, nb, tc, nb, perm, mw, wot, a16, h_zero, l)

        ht = _inproj(x2, g1, mod, w_perm, lat_row(tm_t), tm_t)
        tq = min(512, t)
        q_t = _qprep(ht, nb, t, gq, w_uq, gqn, gqr, cos_t, sin_t, tq)
        k_t, v_t = _kvprep(ht, nb, t, gkv, w_ukv, gkn, gkr, cos_t, sin_t, tq)
        o_a = _attention(q_t, k_t, v_t, (k_c, v_c), min(256, t), min(4096, t)).reshape(nb * t, -1)

        rc_t, kc_t, vc_t, kk_t = _rwkv_conv(ht.reshape(nb, t, -1), conv_w, k_k, ones_bd128)
        yf_t, yb_t, _ = _rwkv_scan(rc_t, kc_t, vc_t, kk_t, ht, rp, s_ctx)
        flat = lambda z: z.reshape(-1, z.shape[-1])
        o_b = _rwkv_post(flat(yf_t), flat(yb_t), flat(rc_t), flat(kc_t), flat(vc_t), ht, rp, ones_bd,
                         min(512, nb * t))

        ys_t, _ = _s5_scan(ht, nb, t, 1, perm, mw, wot, a16, h_ctx, l)
        o_c = _s5_post(ys_t, ht, d_skip, glu_w, glu_b, min(512, nb * t))

        mg = _merge(o_a, o_b, o_c, ht, wb, tm_t)
        tm_m = min(512, t)
        x2 = _outproj(mg, wo, x2, mod, lat_row(tm_m), tm_m)
        x2 = _mlp(x2, g2n, mod, w1, w2, lat_row(tm_m), tm_m)

        if not last:
            q_c = _qprep(hc, nb, tc, gq, w_uq, gqn, gqr, cos_c, sin_c, tc)
            o_a_c = _attention(q_c, k_c, v_c, None, tc, tc).reshape(nb * tc, -1)
            o_b_c = _rwkv_post(flat(yf_c), flat(yb_c), flat(rc_c), flat(kc_c), flat(vc_c), hc, rp, ones_bd,
                               min(512, nb * tc))
            o_c_c = _s5_post(ys_c, hc, d_skip, glu_w, glu_b, min(512, nb * tc))
            mg_c = _merge(o_a_c, o_b_c, o_c_c, hc, wb, tm_c)
            tm_mc = min(512, nb * tc)
            xc2 = _outproj(mg_c, wo, xc2, mod, ctx_rowf, tm_mc)
            xc2 = _mlp(xc2, g2n, mod, w1, w2, ctx_rowf, tm_mc)

    return x2.reshape(nb, t, d)
```

```python
import functools
import math

import jax
import jax.numpy as jnp
from jax import lax
from jax.experimental import pallas as pl
from jax.experimental.pallas import tpu as pltpu

F32 = jnp.float32
BF = jnp.bfloat16

GRID_W = 64
N_HEADS = 8
MLA_NOPE = 128
MLA_ROPE = 64
MLA_V = 128
ROPE_THETA = 10000.0
RWKV_HEAD = 64
RWKV_GN_EPS = 64e-5
L2_EPS = 1e-12
S5_GROUP = 16
S5_STATE = 64
NORM_EPS = 1e-6

LANES = 128
VMEM_LIMIT = 48 * 1024 * 1024

COL_R, COL_K, COL_V, COL_U, COL_GATE, COL_CQ, COL_CKV, COL_SMALL = (
    0, 1024, 2048, 3072, 4096, 10240, 10752, 11264)
N_IN_PAD = 11776
SM_KR, SM_WD, SM_AD, SM_GD = 0, 64, 192, 320

RWKV_CHUNK = 64
RWKV_CHUNKS_PER_STEP = 4
HEAD_GROUP = 256
S5_CHUNK = 16
S5_SETUP_GROUPS = 8
ATTN_HEADS_PER_STEP = 2


def _cp(sem, vmem=VMEM_LIMIT):
    return pltpu.CompilerParams(dimension_semantics=sem, vmem_limit_bytes=vmem)


def _mm(a, b):
    return jnp.dot(a.astype(BF), b.astype(BF), preferred_element_type=F32)


def _mm_nt(a, b):
    return lax.dot_general(a.astype(BF), b.astype(BF), (((1,), (1,)), ((), ())),
                           preferred_element_type=F32)


def _mm_tn(a, b):
    return lax.dot_general(a.astype(BF), b.astype(BF), (((0,), (0,)), ((), ())),
                           preferred_element_type=F32)


def _idiv(x, n):
    assert n & (n - 1) == 0, n
    return lax.shift_right_logical(x, n.bit_length() - 1)


def _imod(x, n):
    assert n & (n - 1) == 0, n
    return lax.bitwise_and(x, n - 1)


def _rms(x, g, eps=NORM_EPS):
    return x * lax.rsqrt(jnp.mean(x * x, axis=-1, keepdims=True) + eps) * g


def _softplus(x):
    return jnp.maximum(x, 0.0) + jnp.log(1.0 + jnp.exp(-jnp.abs(x)))


def _ada_kernel(c_ref, w_ref, b_ref, o_ref):
    c = c_ref[...]
    s = c * jax.nn.sigmoid(c)
    o_ref[0] = _mm(s, w_ref[0]) + b_ref[0]


def _ada(cc, ada_w, ada_b):
    n_layer, d, n = ada_w.shape
    tn = 1024
    return pl.pallas_call(
        _ada_kernel,
        grid=(n_layer, n // tn),
        in_specs=[pl.BlockSpec((8, d), lambda l, j: (0, 0)),
                  pl.BlockSpec((1, d, tn), lambda l, j: (l, 0, j)),
                  pl.BlockSpec((1, 1, tn), lambda l, j: (l, 0, j))],
        out_specs=pl.BlockSpec((1, 8, tn), lambda l, j: (l, 0, j)),
        out_shape=jax.ShapeDtypeStruct((n_layer, 8, n), F32),
        compiler_params=_cp(("arbitrary", "arbitrary")),
    )(cc, ada_w, ada_b.reshape(n_layer, 1, n))


W_IN_SEGMENTS = ((1088, 3072), (4576, 1024), (5600, 6144), (0, 1024), (1024, 64), (4160, 416))


def _wperm_kernel(w_ref, o_ref):
    off = 0
    for src, n in W_IN_SEGMENTS:
        o_ref[0, :, off:off + n] = w_ref[0, :, src:src + n].astype(BF)
        off += n
    o_ref[0, :, off:] = jnp.zeros((o_ref.shape[1], o_ref.shape[2] - off), BF)


def _permute_w_in(w_in):
    nl, d, n_in = w_in.shape
    tr = 256
    return pl.pallas_call(
        _wperm_kernel,
        grid=(nl, d // tr),
        in_specs=[pl.BlockSpec((1, tr, n_in), lambda l, i: (l, i, 0))],
        out_specs=pl.BlockSpec((1, tr, N_IN_PAD), lambda l, i: (l, i, 0)),
        out_shape=jax.ShapeDtypeStruct((nl, d, N_IN_PAD), BF),
        compiler_params=_cp(("arbitrary", "arbitrary")),
    )(w_in)


def _inproj_kernel(x_ref, g_ref, sh_ref, sc_ref, w_ref, o_ref, xn_ref):
    @pl.when(pl.program_id(1) == 0)
    def _():
        y = _rms(x_ref[...], g_ref[...])
        xn_ref[...] = (y * (1.0 + sc_ref[0, 0]) + sh_ref[0, 0]).astype(BF)

    o_ref[...] = jnp.dot(xn_ref[...], w_ref[...], preferred_element_type=F32).astype(o_ref.dtype)


def _inproj(x2, g, mod, w, row_fn, tm):
    m, d = x2.shape
    n = w.shape[1]
    tn = 512
    return pl.pallas_call(
        _inproj_kernel,
        grid=(m // tm, n // tn),
        in_specs=[pl.BlockSpec((tm, d), lambda i, j: (i, 0)),
                  pl.BlockSpec((1, d), lambda i, j: (0, 0)),
                  pl.BlockSpec((1, 1, 1, d), lambda i, j: (row_fn(i), 0, 0, 0)),
                  pl.BlockSpec((1, 1, 1, d), lambda i, j: (row_fn(i), 1, 0, 0)),
                  pl.BlockSpec((d, tn), lambda i, j: (0, j))],
        out_specs=pl.BlockSpec((tm, tn), lambda i, j: (i, j)),
        out_shape=jax.ShapeDtypeStruct((m, n), BF),
        scratch_shapes=[pltpu.VMEM((tm, d), BF)],
        compiler_params=_cp(("arbitrary", "arbitrary")),
    )(x2, g, mod, mod, w)


def _rope_rotate(x, cos_t, sin_t):
    lane = lax.broadcasted_iota(jnp.int32, x.shape, 1)
    sw = jnp.where(lane < 32, pltpu.roll(x, 96, 1), pltpu.roll(x, 32, 1))
    return x * cos_t + sw * sin_t


def _qprep_kernel(scale, cq_ref, g_ref, w_ref, gn_ref, gr_ref, cos_ref, sin_ref, q_ref):
    xn = _rms(cq_ref[...].astype(F32), g_ref[...])
    q = _mm(xn, w_ref[...])
    cos_t, sin_t = cos_ref[...], sin_ref[...]
    for h in range(N_HEADS):
        nope = q[:, h * 256:h * 256 + 128]
        rp = q[:, h * 256 + 128:h * 256 + 256]
        nope = _rms(nope, gn_ref[...])
        rp = rp * lax.rsqrt(jnp.sum(rp * rp, axis=-1, keepdims=True) * (1.0 / MLA_ROPE) + NORM_EPS) * gr_ref[...]
        rp = _rope_rotate(rp, cos_t, sin_t)
        q_ref[0, h, :, 0:128] = (nope * scale).astype(BF)
        q_ref[0, h, :, 128:256] = (rp * scale).astype(BF)


def _qprep(h2, nb, t, g, w, gn, gr, cos_t, sin_t, tm):
    nt = t // tm
    scale = math.log2(math.e) / math.sqrt(MLA_NOPE + MLA_ROPE)
    return pl.pallas_call(
        functools.partial(_qprep_kernel, scale),
        grid=(nb * nt,),
        in_specs=[pl.BlockSpec((tm, 512), lambda i: (i, COL_CQ // 512)),
                  pl.BlockSpec((1, 512), lambda i: (0, 0)),
                  pl.BlockSpec((512, N_HEADS * 256), lambda i: (0, 0)),
                  pl.BlockSpec((1, 128), lambda i: (0, 0)),
                  pl.BlockSpec((1, 128), lambda i: (0, 0)),
                  pl.BlockSpec((tm, 128), lambda i: (i % nt, 0)),
                  pl.BlockSpec((tm, 128), lambda i: (i % nt, 0))],
        out_specs=pl.BlockSpec((1, N_HEADS, tm, 256), lambda i: (i // nt, 0, i % nt, 0)),
        out_shape=jax.ShapeDtypeStruct((nb, N_HEADS, t, 256), BF),
        compiler_params=_cp(("arbitrary",)),
    )(h2, g, w, gn, gr, cos_t, sin_t)


def _kvprep_kernel(ckv_ref, sm_ref, g_ref, w_ref, gn_ref, gr_ref, cos_ref, sin_ref, k_ref, v_ref):
    xn = _rms(ckv_ref[...].astype(F32), g_ref[...])
    kv = _mm(xn, w_ref[...])
    sm = sm_ref[...].astype(F32)
    lane = lax.broadcasted_iota(jnp.int32, sm.shape, 1)
    kr = jnp.where(lane < MLA_ROPE, sm, 0.0)
    kr = kr * lax.rsqrt(jnp.sum(kr * kr, axis=-1, keepdims=True) * (1.0 / MLA_ROPE) + NORM_EPS) * gr_ref[...]
    kr = _rope_rotate(kr, cos_ref[...], sin_ref[...]).astype(BF)
    ones_col = jnp.where(lane == 0, 1.0, 0.0).astype(BF)
    for h in range(N_HEADS):
        kn = _rms(kv[:, h * 256:h * 256 + 128], gn_ref[...])
        k_ref[0, h, :, 0:128] = kn.astype(BF)
        k_ref[0, h, :, 128:256] = kr
        v_ref[0, h, :, 0:128] = kv[:, h * 256 + 128:h * 256 + 256].astype(BF)
        v_ref[0, h, :, 128:256] = ones_col


def _kvprep(h2, nb, t, g, w, gn, gr, cos_t, sin_t, tm):
    nt = t // tm
    return pl.pallas_call(
        _kvprep_kernel,
        grid=(nb * nt,),
        in_specs=[pl.BlockSpec((tm, 512), lambda i: (i, COL_CKV // 512)),
                  pl.BlockSpec((tm, 128), lambda i: (i, COL_SMALL // 128)),
                  pl.BlockSpec((1, 512), lambda i: (0, 0)),
                  pl.BlockSpec((512, N_HEADS * 256), lambda i: (0, 0)),
                  pl.BlockSpec((1, 128), lambda i: (0, 0)),
                  pl.BlockSpec((1, 128), lambda i: (0, 0)),
                  pl.BlockSpec((tm, 128), lambda i: (i % nt, 0)),
                  pl.BlockSpec((tm, 128), lambda i: (i % nt, 0))],
        out_specs=[pl.BlockSpec((1, N_HEADS, tm, 256), lambda i: (i // nt, 0, i % nt, 0)),
                   pl.BlockSpec((1, N_HEADS, tm, 256), lambda i: (i // nt, 0, i % nt, 0))],
        out_shape=[jax.ShapeDtypeStruct((nb, N_HEADS, t, 256), BF),
                   jax.ShapeDtypeStruct((nb, N_HEADS, t, 256), BF)],
        compiler_params=_cp(("arbitrary",)),
    )(h2, h2, g, w, gn, gr, cos_t, sin_t)


def _attn_kernel(has_extra, nk, *refs):
    if has_extra:
        q_ref, k_ref, v_ref, kc_ref, vc_ref, o_ref, m_ref, acc_ref = refs
    else:
        q_ref, k_ref, v_ref, o_ref, m_ref, acc_ref = refs
    j = pl.program_id(3)

    @pl.when(j == 0)
    def _():
        m_ref[...] = jnp.full(m_ref.shape, -1e30, F32)
        acc_ref[...] = jnp.zeros(acc_ref.shape, F32)

    heads = range(ATTN_HEADS_PER_STEP)

    def step(k_r, v_r):
        s = [lax.dot_general(q_ref[0, h], k_r[0, h], (((1,), (1,)), ((), ())),
                             preferred_element_type=F32) for h in heads]
        m_prev = [m_ref[h] for h in heads]
        m_new = [jnp.maximum(m_prev[h], jnp.max(s[h], axis=1, keepdims=True)) for h in heads]
        alpha = [jnp.exp2(m_prev[h] - m_new[h]) for h in heads]
        p = [jnp.exp2((s[h] - m_new[h]).astype(BF)) for h in heads]
        pv = [jnp.dot(p[h], v_r[0, h], preferred_element_type=F32) for h in heads]
        for h in heads:
            acc_ref[h] = alpha[h] * acc_ref[h] + pv[h]
            m_ref[h] = m_new[h]

    step(k_ref, v_ref)

    @pl.when(j == nk - 1)
    def _():
        if has_extra:
            step(kc_ref, vc_ref)
        for h in heads:
            acc = acc_ref[h]
            o_ref[0, :, h * MLA_V:(h + 1) * MLA_V] = (
                acc[:, :MLA_V] / acc[:, MLA_V:MLA_V + 1]).astype(o_ref.dtype)


def _attention(q, k, v, extra, bq, bk):
    nb, nh, t, _ = q.shape
    tk = k.shape[2]
    nk = tk // bk
    hb = ATTN_HEADS_PER_STEP
    in_specs = [pl.BlockSpec((1, hb, bq, 256), lambda b, h, i, j: (b, h, i, 0)),
                pl.BlockSpec((1, hb, bk, 256), lambda b, h, i, j: (b, h, j, 0)),
                pl.BlockSpec((1, hb, bk, 256), lambda b, h, i, j: (b, h, j, 0))]
    args = [q, k, v]
    if extra is not None:
        kc, vc = extra
        tc = kc.shape[2]
        in_specs += [pl.BlockSpec((1, hb, tc, 256), lambda b, h, i, j: (b, h, 0, 0)),
                     pl.BlockSpec((1, hb, tc, 256), lambda b, h, i, j: (b, h, 0, 0))]
        args += [kc, vc]
    return pl.pallas_call(
        functools.partial(_attn_kernel, extra is not None, nk),
        grid=(nb, nh // hb, t // bq, nk),
        in_specs=in_specs,
        out_specs=pl.BlockSpec((1, bq, hb * MLA_V), lambda b, h, i, j: (b, i, h)),
        out_shape=jax.ShapeDtypeStruct((nb, t, nh * MLA_V), BF),
        scratch_shapes=[pltpu.VMEM((hb, bq, 1), F32), pltpu.VMEM((hb, bq, 2 * MLA_V), F32)],
        compiler_params=_cp(("arbitrary",) * 4),
    )(*args)


def _conv_kernel(r_ref, k_ref, v_ref, wr_ref, wk_ref, wv_ref, kk_w_ref, ones_ref,
                 ro_ref, ko_ref, vo_ref, kko_ref):
    t = r_ref.shape[1]
    row = lax.broadcasted_iota(jnp.int32, (t, LANES), 0)

    def conv(x_ref, w_ref):
        x = x_ref[0].astype(F32)
        xm = jnp.where(row == 0, 0.0, pltpu.roll(x, 1, 0))
        xp = jnp.where(row == t - 1, 0.0, pltpu.roll(x, t - 1, 0))
        w = w_ref[...]
        return xm * w[0:1] + x * w[1:2] + xp * w[2:3]

    ro_ref[0] = conv(r_ref, wr_ref).astype(BF)
    vo_ref[0] = conv(v_ref, wv_ref).astype(BF)
    k = conv(k_ref, wk_ref)
    ko_ref[0] = k.astype(BF)
    kk = k * kk_w_ref[...]
    ss = _mm(kk * kk, ones_ref[...])
    kko_ref[0] = (kk * lax.rsqrt(ss + L2_EPS)).astype(BF)


def _rwkv_conv(h3, conv_w, k_k, ones_bd):
    nb, t, _ = h3.shape
    nj = 1024 // LANES
    blk = lambda off: pl.BlockSpec((1, t, LANES), lambda b, j: (b, 0, off // LANES + j))
    wblk = lambda off: pl.BlockSpec((3, LANES), lambda b, j: (0, off // LANES + j))
    oblk = pl.BlockSpec((1, t, LANES), lambda b, j: (b, 0, j))
    osh = jax.ShapeDtypeStruct((nb, t, 1024), BF)
    return pl.pallas_call(
        _conv_kernel,
        grid=(nb, nj),
        in_specs=[blk(COL_R), blk(COL_K), blk(COL_V), wblk(0), wblk(1024), wblk(2048),
                  pl.BlockSpec((1, LANES), lambda b, j: (0, j)),
                  pl.BlockSpec((LANES, LANES), lambda b, j: (0, 0))],
        out_specs=[oblk, oblk, oblk, oblk],
        out_shape=[osh, osh, osh, osh],
        compiler_params=_cp(("arbitrary", "arbitrary")),
    )(h3, h3, h3, conv_w, conv_w, conv_w, k_k, ones_bd)


def _bd_rows(x, bdmask):
    reps = HEAD_GROUP // x.shape[0]
    return jnp.where(bdmask, jnp.concatenate([x] * reps, axis=0), 0.0)


def _scan_prep(d, j, r_ref, k_ref, v_ref, kk_ref, sm_ref, w0_ref, w2_ref, a0_ref, a2_ref, ka_ref, tri):
    L = RWKV_CHUNK
    rows = slice(j * L, (j + 1) * L)
    r = r_ref[0, rows].astype(F32)
    k = k_ref[0, rows].astype(F32)
    v = v_ref[0, rows].astype(F32)
    kk = kk_ref[0, rows].astype(F32)
    sm = sm_ref[rows].astype(F32)
    wd = sm[:, SM_WD + 64 * d:SM_WD + 64 * d + 64]
    ad = sm[:, SM_AD + 64 * d:SM_AD + 64 * d + 64]
    w_log = -_softplus(-(w0_ref[d:d + 1] + _mm(jnp.tanh(wd), w2_ref[d]))) - 0.5
    lw = -jnp.exp(w_log)
    a = jax.nn.sigmoid(a0_ref[d:d + 1] + _mm(ad, a2_ref[d]))
    b = kk * a
    krep = k * (1.0 + (a - 1.0) * ka_ref[...])
    cl = jnp.dot(tri, lw, precision=lax.Precision.HIGHEST, preferred_element_type=F32)
    cl_last = cl[0:1] if d == 1 else cl[L - 1:L]
    e_neg = jnp.exp(-cl)
    e_last = jnp.exp(cl_last - cl)
    at = -(kk * jnp.exp(cl - lw))
    rt = r * jnp.exp(cl)
    bt = b * e_neg
    kt = krep * e_neg
    bh = b * e_last
    kh = krep * e_last
    return dict(at=at, rt=rt, bt=bt, kt=kt, bh=bh, kh=kh, v=v, p_last=jnp.exp(cl_last))


def _scan_kernel(nc, rf, kf, vf, kkf, smf, rb, kb, vb, kkb, smb, w0, w2, a0, a2, ka, s_in,
                 yf, yb, s_out, s_ref):
    c = pl.program_id(1)

    @pl.when(c == 0)
    def _():
        s_ref[...] = s_in[0]

    L = RWKV_CHUNK
    row2 = lax.broadcasted_iota(jnp.int32, (HEAD_GROUP, HEAD_GROUP), 0)
    col2 = lax.broadcasted_iota(jnp.int32, (HEAD_GROUP, HEAD_GROUP), 1)
    bdmask = _idiv(row2, RWKV_HEAD) == _idiv(col2, RWKV_HEAD)
    tt = lax.broadcasted_iota(jnp.int32, (L, HEAD_GROUP), 0)
    ss = _imod(lax.broadcasted_iota(jnp.int32, (L, HEAD_GROUP), 1), L)
    rl = lax.broadcasted_iota(jnp.int32, (L, L), 0)
    cl = lax.broadcasted_iota(jnp.int32, (L, L), 1)
    blk = tuple(_idiv(ss, n) == _idiv(tt, n) for n in (8, 16, 32, 64))
    eye = jnp.where(ss == tt, 1.0, 0.0)
    strict = (ss < tt, ss > tt)
    incl = (ss <= tt, ss >= tt)
    cps = RWKV_CHUNKS_PER_STEP
    tri = ((rl >= cl).astype(F32), (rl <= cl).astype(F32))
    in_refs = ((rf, kf, vf, kkf, smf), (rb, kb, vb, kkb, smb))
    sub = lambda d, p: p if d == 0 else cps - 1 - p
    prep = {(d, p): _scan_prep(d, sub(d, p), *in_refs[d], w0, w2, a0, a2, ka, tri[d])
            for d in range(2) for p in range(cps)}
    y_refs = (yf, yb)

    chains = [(d, p, gi) for p in range(cps) for d in range(2) for gi in range(1024 // HEAD_GROUP)]
    n = len(chains)
    bd = lambda x: _bd_rows(x, bdmask)
    cat = jnp.concatenate
    g_ = lambda name: [prep[d, p][name][:, gi * HEAD_GROUP:(gi + 1) * HEAD_GROUP] for d, p, gi in chains]
    at, rt, bt, kt, bh, kh, v = (g_(nm) for nm in ("at", "rt", "bt", "kt", "bh", "kh", "v"))
    ac = [_mm_nt(cat([at[i], rt[i]], axis=0), cat([bd(bt[i]), bd(kt[i])], axis=0)) for i in range(n)]
    a_ab = [jnp.where(strict[chains[i][0]], ac[i][:L, :HEAD_GROUP], 0.0) for i in range(n)]
    a_ak = [jnp.where(strict[chains[i][0]], ac[i][:L, HEAD_GROUP:], 0.0) for i in range(n)]
    c_b = [jnp.where(incl[chains[i][0]], ac[i][L:, :HEAD_GROUP], 0.0) for i in range(n)]
    c_k = [jnp.where(incl[chains[i][0]], ac[i][L:, HEAD_GROUP:], 0.0) for i in range(n)]
    av = [_mm(a_ak[i], bd(v[i])) for i in range(n)]
    a8 = [jnp.where(blk[0], a_ab[i], 0.0) for i in range(n)]
    p2 = [_mm(a8[i], bd(a8[i])) for i in range(n)]
    tm = [eye + a8[i] for i in range(n)]
    pt = [_mm(cat([p2[i], tm[i]], axis=0), bd(p2[i])) for i in range(n)]
    tm = [tm[i] + pt[i][L:] for i in range(n)]
    tm = [tm[i] + _mm(tm[i], bd(pt[i][:L])) for i in range(n)]
    for lvl in range(3):
        off_mask = blk[lvl + 1] & ~blk[lvl]
        z = [_mm(jnp.where(off_mask, a_ab[i], 0.0), bd(tm[i])) for i in range(n)]
        tm = [tm[i] + _mm(tm[i], bd(z[i])) for i in range(n)]
    wu = [_mm(tm[i], cat([bd(at[i]), bd(av[i])], axis=1)) for i in range(n)]
    w = [wu[i][:, :HEAD_GROUP] for i in range(n)]
    uv = [wu[i][:, HEAD_GROUP:] for i in range(n)]
    q = [rt[i] + _mm(c_b[i], bd(w[i])) for i in range(n)]
    y0 = [_mm(cat([c_b[i], c_k[i]], axis=1), cat([bd(uv[i]), bd(v[i])], axis=0)) for i in range(n)]
    zeros = jnp.zeros((L, HEAD_GROUP), F32)
    gh = [_mm_tn(cat([cat([w[i], uv[i]], axis=1), cat([zeros, v[i]], axis=1)], axis=0),
                 cat([bh[i], kh[i]], axis=0)) for i in range(n)]
    for i, (d, p, gi) in enumerate(chains):
        sl = slice(gi * HEAD_GROUP, (gi + 1) * HEAD_GROUP)
        rows = slice(sub(d, p) * L, (sub(d, p) + 1) * L)
        s0 = s_ref[d, gi]
        y_refs[d][0, rows, sl] = (y0[i] + _mm_nt(q[i], s0)).astype(y_refs[d].dtype)
        g = jnp.where(bdmask, gh[i][:HEAD_GROUP], 0.0)
        hh = jnp.where(bdmask, gh[i][HEAD_GROUP:], 0.0)
        s_ref[d, gi] = s0 * prep[d, p]["p_last"][:, sl] + _mm(s0, g) + hh

    @pl.when(c == nc - 1)
    def _():
        s_out[0] = s_ref[...]


def _rwkv_scan(rc, kc, vc, kk, h2, p, s_in):
    nb, t, _ = rc.shape
    L = RWKV_CHUNK * RWKV_CHUNKS_PER_STEP
    assert t % L == 0, (t, L)
    nc = t // L
    fw = lambda b, c: (b, c, 0)
    bw = lambda b, c: (b, nc - 1 - c, 0)
    seq = lambda im: pl.BlockSpec((1, L, 1024), im)
    smf = pl.BlockSpec((L, 512), lambda b, c: (b * nc + c, COL_SMALL // 512))
    smb = pl.BlockSpec((L, 512), lambda b, c: (b * nc + nc - 1 - c, COL_SMALL // 512))
    full = lambda shape: pl.BlockSpec(shape, lambda b, c: (0,) * len(shape))
    st = pl.BlockSpec((1, 2, 4, HEAD_GROUP, HEAD_GROUP), lambda b, c: (b, 0, 0, 0, 0))
    ysh = jax.ShapeDtypeStruct((nb, t, 1024), BF)
    return pl.pallas_call(
        functools.partial(_scan_kernel, nc),
        grid=(nb, nc),
        in_specs=[seq(fw), seq(fw), seq(fw), seq(fw), smf, seq(bw), seq(bw), seq(bw), seq(bw), smb,
                  full((2, 1024)), full((2, 64, 1024)), full((2, 1024)), full((2, 64, 1024)),
                  full((1, 1024)), st],
        out_specs=[seq(fw), seq(bw), st],
        out_shape=[ysh, ysh, jax.ShapeDtypeStruct(s_in.shape, F32)],
        scratch_shapes=[pltpu.VMEM((2, 4, HEAD_GROUP, HEAD_GROUP), F32)],
        compiler_params=_cp(("arbitrary", "arbitrary")),
    )(rc, kc, vc, kk, h2, rc, kc, vc, kk, h2, p["w0"], p["w2"], p["a0"], p["a2"], p["k_a"], s_in)


def _rpost_kernel(yf_ref, yb_ref, r_ref, k_ref, v_ref, sm_ref, a0_ref, a2_ref, ka_ref, rk_ref,
                  lng_ref, lnb_ref, g2_ref, ones_ref, o_ref):
    y = yf_ref[...].astype(F32) + yb_ref[...].astype(F32)
    r = r_ref[...].astype(F32)
    k = k_ref[...].astype(F32)
    v = v_ref[...].astype(F32)
    sm = sm_ref[...].astype(F32)
    a_sum = 0.0
    for d in range(2):
        ad = sm[:, SM_AD + 64 * d:SM_AD + 64 * d + 64]
        a_sum = a_sum + jax.nn.sigmoid(a0_ref[d:d + 1] + _mm(ad, a2_ref[d]))
    kb = k * (1.0 + (0.5 * a_sum - 1.0) * ka_ref[...])
    rkb = r * kb * rk_ref[...]
    gate = _mm(jax.nn.sigmoid(sm), g2_ref[...])
    ones = ones_ref[...]
    inv_n = 1.0 / RWKV_HEAD
    for gi in range(1024 // HEAD_GROUP):
        sl = slice(gi * HEAD_GROUP, (gi + 1) * HEAD_GROUP)
        yg = y[:, sl]
        mu = _mm(yg, ones) * inv_n
        dlt = yg - mu
        var = _mm(dlt * dlt, ones) * inv_n
        yn = dlt * lax.rsqrt(var + RWKV_GN_EPS) * lng_ref[:, sl] + lnb_ref[:, sl]
        bonus = _mm(rkb[:, sl], ones) * v[:, sl]
        o_ref[:, sl] = ((yn + bonus) * gate[:, sl]).astype(o_ref.dtype)


def _rwkv_post(yf, yb, rc, kc, vc, h2, p, ones_bd, tm):
    m = yf.shape[0]
    row = lambda i: (i, 0)
    full = lambda shape: pl.BlockSpec(shape, lambda i: (0,) * len(shape))
    seq = pl.BlockSpec((tm, 1024), row)
    return pl.pallas_call(
        _rpost_kernel,
        grid=(m // tm,),
        in_specs=[seq, seq, seq, seq, seq,
                  pl.BlockSpec((tm, 512), lambda i: (i, COL_SMALL // 512)),
                  full((2, 1024)), full((2, 64, 1024)), full((1, 1024)), full((1, 1024)),
                  full((1, 1024)), full((1, 1024)), full((512, 1024)),
                  full((HEAD_GROUP, HEAD_GROUP))],
        out_specs=seq,
        out_shape=jax.ShapeDtypeStruct((m, 1024), BF),
        compiler_params=_cp(("arbitrary",)),
    )(yf, yb, rc, kc, vc, h2, p["a0"], p["a2"], p["k_a"], p["r_k"], p["ln_g"], p["ln_b"],
      p["g2_pad"], ones_bd)


def _cmul(x, y):
    return x[0] * y[0] - x[1] * y[1], x[0] * y[1] + x[1] * y[0]


def _s5setup_kernel(lre_ref, lim_ref, ldt_ref, btr_ref, bti_ref, cr_ref, ci_ref,
                    mw_ref, wot_ref, a16_ref):
    n = S5_CHUNK
    half = n // 2
    rt = _idiv(lax.broadcasted_iota(jnp.int32, (n * S5_GROUP, n * S5_GROUP), 0), S5_GROUP)
    ct = _idiv(lax.broadcasted_iota(jnp.int32, (n * S5_GROUP, n * S5_GROUP), 1), S5_GROUP)
    for gi, d in [(gi, d) for gi in range(S5_SETUP_GROUPS) for d in range(2)]:
        lr = lre_ref[0, d, gi]
        li = lim_ref[0, d, gi]
        dt = jnp.exp(ldt_ref[0, d, gi])

        def cexp(mult):
            mag = jnp.exp(mult * lr * dt)
            ang = mult * li * dt
            return mag * jnp.cos(ang), mag * jnp.sin(ang)

        a1 = cexp(1.0)
        den = lr * lr + li * li
        q_re = ((a1[0] - 1.0) * lr + a1[1] * li) / den
        q_im = (a1[1] * lr - (a1[0] - 1.0) * li) / den
        bb = (q_re * btr_ref[0, d, gi] - q_im * bti_ref[0, d, gi],
              q_re * bti_ref[0, d, gi] + q_im * btr_ref[0, d, gi])
        cc = (cr_ref[0, d, gi], ci_ref[0, d, gi])
        pw = {0: (jnp.ones_like(lr), jnp.zeros_like(lr)), 1: a1, -1: cexp(-1.0)}
        for m in range(2, n + 1):
            pw[m] = _cmul(pw[m - 1], pw[1])
        for m in range(2, half + 1):
            pw[-m] = _cmul(pw[-m + 1], pw[-1])

        def rows(base, efn):
            parts = [_cmul(base, pw[efn(t)]) for t in range(n)]
            return (jnp.concatenate([z[0] for z in parts], axis=0),
                    jnp.concatenate([z[1] for z in parts], axis=0))

        if d == 0:
            e_pow, f_pow = (lambda t: half - t), (lambda t: t - half)
            wi_pow, wo_pow = (lambda t: n - 1 - t), (lambda t: t + 1)
            mask = ct >= rt
        else:
            e_pow, f_pow = (lambda t: t - half), (lambda t: half - t)
            wi_pow, wo_pow = (lambda t: t), (lambda t: n - t)
            mask = ct <= rt
        e = rows(bb, e_pow)
        f = rows(cc, f_pow)
        mmat = lax.dot_general(jnp.concatenate([e[0], -e[1]], axis=1),
                               jnp.concatenate([f[0], f[1]], axis=1),
                               (((1,), (1,)), ((), ())),
                               precision=lax.Precision.HIGHEST, preferred_element_type=F32)
        mmat = jnp.where(mask, mmat, 0.0)
        wi = rows(bb, wi_pow)
        wo = rows(cc, wo_pow)
        mw_ref[0, d, gi] = jnp.concatenate([mmat, wi[0], wi[1]], axis=1).astype(BF)
        wot_ref[0, d, gi] = jnp.concatenate([wo[0], -wo[1]], axis=1).astype(BF)
        a16_ref[0, d, gi] = jnp.concatenate([pw[n][0], pw[n][1]], axis=1)


def _s5_setup(lam_re, lam_im, log_dt, bt_re, bt_im, c_re, c_im):
    nl, _, ng, _ = lam_re.shape
    n2 = S5_CHUNK * S5_GROUP
    gs = S5_SETUP_GROUPS
    vec = pl.BlockSpec((1, 2, gs,1, S5_STATE), lambda l, g: (l, 0, g, 0, 0))
    sc = pl.BlockSpec((1, 2, gs,1, 1), lambda l, g: (l, 0, g, 0, 0))
    mat = pl.BlockSpec((1, 2, gs,S5_GROUP, S5_STATE), lambda l, g: (l, 0, g, 0, 0))
    return pl.pallas_call(
        _s5setup_kernel,
        grid=(nl, ng // gs),
        in_specs=[vec, vec, sc, mat, mat, mat, mat],
        out_specs=[pl.BlockSpec((1, 2, gs,n2, n2 + 2 * S5_STATE), lambda l, g: (l, 0, g, 0, 0)),
                   pl.BlockSpec((1, 2, gs,n2, 2 * S5_STATE), lambda l, g: (l, 0, g, 0, 0)),
                   pl.BlockSpec((1, 2, gs,1, 2 * S5_STATE), lambda l, g: (l, 0, g, 0, 0))],
        out_shape=[jax.ShapeDtypeStruct((nl, 2, ng, n2, n2 + 2 * S5_STATE), BF),
                   jax.ShapeDtypeStruct((nl, 2, ng, n2, 2 * S5_STATE), BF),
                   jax.ShapeDtypeStruct((nl, 2, ng, 1, 2 * S5_STATE), F32)],
        compiler_params=_cp(("arbitrary", "arbitrary")),
    )(lam_re.reshape(nl, 2, ng, 1, S5_STATE), lam_im.reshape(nl, 2, ng, 1, S5_STATE),
      log_dt.reshape(nl, 2, ng, 1, 1), bt_re, bt_im, c_re, c_im)


S5_GROUPS_PER_STEP = LANES // S5_GROUP


def _s5_kernel(bb, nc, u_ref, perm_ref, mw_ref, wot_ref, a16_ref, h0_ref, y_ref, hfin_ref,
               uf_ref, yf_ref):
    n = S5_CHUNK
    n2 = S5_CHUNK * S5_GROUP
    ns = S5_STATE
    rows = bb * nc
    perm = perm_ref[...]
    uf_ref[...] = u_ref[...].astype(F32)
    ucat = jnp.concatenate([uf_ref[pl.ds(tau, rows, stride=n), :] for tau in range(n)],
                           axis=1).astype(BF)
    xcat = jnp.dot(ucat, perm, preferred_element_type=F32).astype(BF)
    ridx = lax.broadcasted_iota(jnp.int32, (rows, 2 * ns), 0)
    cpos = _imod(ridx, nc)
    lane = lax.broadcasted_iota(jnp.int32, (1, 2 * ns), 1)

    def coef_pair(coef):
        sw = pltpu.roll(coef, ns, 1)
        return jnp.where(lane < ns, coef, sw), jnp.where(lane < ns, -sw, coef)

    def cmul_rows(coef, val):
        c_re, c_im = coef_pair(coef)
        return val * c_re + pltpu.roll(val, ns, 1) * c_im

    chains = [(g, d) for g in range(S5_GROUPS_PER_STEP) for d in range(2)]
    rng = range(len(chains))
    first = (0, nc - 1)
    last = (nc - 1, 0)
    yz = [jnp.dot(xcat[:, g * n2:(g + 1) * n2], mw_ref[0, d, g], preferred_element_type=F32)
          for g, d in chains]
    h0r = []
    for g, d in chains:
        z = jnp.zeros((rows, 2 * ns), F32)
        for bi in range(bb):
            z = jnp.where(ridx == bi * nc + first[d], h0_ref[bi, g, d:d + 1], z)
        h0r.append(z)
    ap = [a16_ref[0, d, g] for g, d in chains]
    e = [yz[i][:, n2:] + cmul_rows(ap[i], h0r[i]) for i in rng]
    sh = 1
    while sh < nc:
        shifted = [jnp.where(cpos >= sh, pltpu.roll(e[i], sh, 0), 0.0) if chains[i][1] == 0 else
                   jnp.where(cpos < nc - sh, pltpu.roll(e[i], rows - sh, 0), 0.0) for i in rng]
        e = [e[i] + cmul_rows(ap[i], shifted[i]) for i in rng]
        ap = [cmul_rows(ap[i], ap[i]) for i in rng]
        sh *= 2
    hs = [jnp.where(cpos >= 1, pltpu.roll(e[i], 1, 0), h0r[i]) if chains[i][1] == 0 else
          jnp.where(cpos < nc - 1, pltpu.roll(e[i], rows - 1, 0), h0r[i]) for i in rng]
    yd = [yz[i][:, :n2] + _mm_nt(hs[i], wot_ref[0, d, g]) for i, (g, d) in enumerate(chains)]
    ycat = jnp.concatenate([yd[2 * g] + yd[2 * g + 1] for g in range(S5_GROUPS_PER_STEP)], axis=1)
    ytok = lax.dot_general(ycat.astype(BF), perm, (((1,), (1,)), ((), ())), preferred_element_type=F32)
    for tau in range(n):
        yf_ref[pl.ds(tau, rows, stride=n), :] = ytok[:, tau * LANES:(tau + 1) * LANES]
    y_ref[...] = yf_ref[...].astype(y_ref.dtype)
    for i, (g, d) in enumerate(chains):
        for bi in range(bb):
            hfin_ref[bi, g, d:d + 1] = e[i][bi * nc + last[d]:bi * nc + last[d] + 1]


def _s5_scan(h2, nb, t, bb, perm, mw, wot, a16, h0, layer):
    n = S5_CHUNK
    nc = t // n
    gs = S5_GROUPS_PER_STEP
    ng = 1024 // S5_GROUP
    n2 = S5_CHUNK * S5_GROUP
    par = lambda shape: pl.BlockSpec((1, 2, gs) + shape, lambda s, b: (layer, 0, s, 0, 0))
    st = pl.BlockSpec((bb, gs, 2, 2 * S5_STATE), lambda s, b: (b, s, 0, 0))
    return pl.pallas_call(
        functools.partial(_s5_kernel, bb, nc),
        grid=(ng // gs, nb // bb),
        in_specs=[pl.BlockSpec((bb * t, LANES), lambda s, b: (b, COL_U // LANES + s)),
                  pl.BlockSpec(perm.shape, lambda s, b: (0, 0)),
                  par((n2, n2 + 2 * S5_STATE)), par((n2, 2 * S5_STATE)), par((1, 2 * S5_STATE)), st],
        out_specs=[pl.BlockSpec((bb * t, LANES), lambda s, b: (b, s)), st],
        out_shape=[jax.ShapeDtypeStruct((nb * t, 1024), BF),
                   jax.ShapeDtypeStruct((nb, ng, 2, 2 * S5_STATE), F32)],
        scratch_shapes=[pltpu.VMEM((bb * t, LANES), F32), pltpu.VMEM((bb * t, LANES), F32)],
        compiler_params=_cp(("arbitrary", "arbitrary")),
    )(h2, perm, mw, wot, a16, h0)


def _s5post_kernel(y_ref, u_ref, d_ref, w_ref, b_ref, o_ref):
    yy = y_ref[...].astype(F32) + d_ref[...] * u_ref[...].astype(F32)
    z = jax.nn.gelu(yy)
    o_ref[...] = (z * jax.nn.sigmoid(_mm(z, w_ref[...]) + b_ref[...])).astype(o_ref.dtype)


def _s5_post(y2, h2, d_skip, glu_w, glu_b, tm):
    m = y2.shape[0]
    full = lambda shape: pl.BlockSpec(shape, lambda i: (0,) * len(shape))
    return pl.pallas_call(
        _s5post_kernel,
        grid=(m // tm,),
        in_specs=[pl.BlockSpec((tm, 1024), lambda i: (i, 0)),
                  pl.BlockSpec((tm, 1024), lambda i: (i, COL_U // 1024)),
                  full((1, 1024)), full((1024, 1024)), full((1, 1024))],
        out_specs=pl.BlockSpec((tm, 1024), lambda i: (i, 0)),
        out_shape=jax.ShapeDtypeStruct((m, 1024), BF),
        compiler_params=_cp(("arbitrary",)),
    )(y2, h2, d_skip, glu_w, glu_b)


def _merge_kernel(oa_ref, ob_ref, oc_ref, g0_ref, g1_ref, g2_ref, w_ref, o_ref):
    acc = None
    for o_r, g_r, n in ((oa_ref, g0_ref, 0), (ob_ref, g1_ref, 1), (oc_ref, g2_ref, 2)):
        pr = jnp.dot(o_r[...], w_ref[n], preferred_element_type=F32)
        term = (0.5 + 0.5 * jnp.tanh(0.5 * g_r[...].astype(F32))) * pr
        acc = term if acc is None else acc + term
    o_ref[...] = acc.astype(o_ref.dtype)


def _merge(oa, ob, oc, h2, w_branch, tm):
    m = oa.shape[0]
    d = w_branch.shape[2]
    tn = 512
    seq = pl.BlockSpec((tm, 1024), lambda i, j: (i, 0))
    gate = lambda n: pl.BlockSpec((tm, tn), lambda i, j: (i, (COL_GATE + n * d) // tn + j))
    return pl.pallas_call(
        _merge_kernel,
        grid=(m // tm, d // tn),
        in_specs=[seq, seq, seq, gate(0), gate(1), gate(2),
                  pl.BlockSpec((3, 1024, tn), lambda i, j: (0, 0, j))],
        out_specs=pl.BlockSpec((tm, tn), lambda i, j: (i, j)),
        out_shape=jax.ShapeDtypeStruct((m, d), BF),
        compiler_params=_cp(("arbitrary", "arbitrary")),
    )(oa, ob, oc, h2, h2, h2, w_branch)


def _outproj_kernel(m_ref, w_ref, x_ref, gt_ref, o_ref):
    o_ref[...] = x_ref[...] + gt_ref[0, 0] * jnp.dot(m_ref[...], w_ref[...], preferred_element_type=F32)


def _outproj(mg, w_out, x2, mod, row_fn, tm):
    m, d = x2.shape
    tn = d
    return pl.pallas_call(
        _outproj_kernel,
        grid=(m // tm, d // tn),
        in_specs=[pl.BlockSpec((tm, d), lambda i, j: (i, 0)),
                  pl.BlockSpec((d, tn), lambda i, j: (0, j)),
                  pl.BlockSpec((tm, tn), lambda i, j: (i, j)),
                  pl.BlockSpec((1, 1, 1, tn), lambda i, j: (row_fn(i), 2, 0, j))],
        out_specs=pl.BlockSpec((tm, tn), lambda i, j: (i, j)),
        out_shape=jax.ShapeDtypeStruct((m, d), F32),
        compiler_params=_cp(("arbitrary", "arbitrary")),
    )(mg, w_out, x2, mod)


def _mlp_kernel(nf, x_ref, g_ref, sh_ref, sc_ref, gt_ref, w1_ref, w2_ref, o_ref, xn_ref, acc_ref):
    f = pl.program_id(1)

    @pl.when(f == 0)
    def _():
        y = _rms(x_ref[...], g_ref[...])
        xn_ref[...] = (y * (1.0 + sc_ref[0, 0]) + sh_ref[0, 0]).astype(BF)
        acc_ref[...] = jnp.zeros(acc_ref.shape, F32)

    hmid = jnp.maximum(jnp.dot(xn_ref[...], w1_ref[...], preferred_element_type=F32), 0.0)
    acc_ref[...] += jnp.dot((hmid * hmid).astype(BF), w2_ref[...], preferred_element_type=F32)

    @pl.when(f == nf - 1)
    def _():
        o_ref[...] = x_ref[...] + gt_ref[0, 0] * acc_ref[...]


def _mlp(x2, g, mod, w1, w2, row_fn, tm):
    m, d = x2.shape
    dff = w1.shape[1]
    tf = 1024
    nf = dff // tf
    modspec = lambda k: pl.BlockSpec((1, 1, 1, d), lambda i, f: (row_fn(i), k, 0, 0))
    return pl.pallas_call(
        functools.partial(_mlp_kernel, nf),
        grid=(m // tm, nf),
        in_specs=[pl.BlockSpec((tm, d), lambda i, f: (i, 0)),
                  pl.BlockSpec((1, d), lambda i, f: (0, 0)),
                  modspec(3), modspec(4), modspec(5),
                  pl.BlockSpec((d, tf), lambda i, f: (0, f)),
                  pl.BlockSpec((tf, d), lambda i, f: (f, 0))],
        out_specs=pl.BlockSpec((tm, d), lambda i, f: (i, 0)),
        out_shape=jax.ShapeDtypeStruct((m, d), F32),
        scratch_shapes=[pltpu.VMEM((tm, d), BF), pltpu.VMEM((tm, d), F32)],
        compiler_params=_cp(("arbitrary", "arbitrary")),
    )(x2, g, mod, mod, mod, w1, w2)


def _rope_tables(t):
    rows = t // GRID_W
    row = jnp.repeat(jnp.arange(rows, dtype=F32), GRID_W)
    col = jnp.tile(jnp.arange(GRID_W, dtype=F32), rows)
    n_freq = MLA_ROPE // 4
    inv = ROPE_THETA ** (-jnp.arange(n_freq, dtype=F32) / n_freq)
    ang = jnp.concatenate([row[:, None] * inv, col[:, None] * inv], axis=-1)
    cos, sin = jnp.cos(ang), jnp.sin(ang)
    pad = jnp.zeros((t, LANES - MLA_ROPE), F32)
    return (jnp.concatenate([cos, cos, pad], axis=-1), jnp.concatenate([-sin, sin, pad], axis=-1))


def _pad_lanes(g, n):
    return jnp.pad(g.reshape(1, -1), ((0, 0), (0, n - g.shape[-1])))


def _s5_perm():
    p = jnp.arange(S5_CHUNK * LANES)
    tau, g, i = p // LANES, (p % LANES) // S5_GROUP, p % S5_GROUP
    q = g * (S5_CHUNK * S5_GROUP) + tau * S5_GROUP + i
    return (q[:, None] == p[None, :]).astype(BF)


def kernel(x, c, ctx, c_ctx, ada_w, ada_b, norm1_g, norm2_g, w_in, mla_q_lora_g, mla_kv_lora_g, mla_w_uq, mla_w_ukv, mla_qn_nope_g, mla_qn_rope_g, mla_kn_nope_g, mla_kn_rope_g, rwkv_conv, rwkv_w0, rwkv_w2, rwkv_a0, rwkv_a2, rwkv_g2, rwkv_k_k, rwkv_k_a, rwkv_r_k, rwkv_ln_g, rwkv_ln_b, s5_lam_re, s5_lam_im, s5_log_dt, s5_b_re, s5_b_im, s5_c_re, s5_c_im, s5_d, s5_glu_w, s5_glu_b, w_branch, w_out, w_mlp1, w_mlp2):
    nb, t, d = x.shape
    tc = ctx.shape[1]
    depth = ada_w.shape[0]
    ctx_row = nb

    cc = jnp.concatenate([c, c_ctx[None, :], jnp.zeros((8 - nb - 1, d), F32)], axis=0)
    mod_all = _ada(cc, ada_w, ada_b)

    bt = lambda z: jnp.swapaxes(z, -1, -2)
    mw, wot, a16 = _s5_setup(s5_lam_re, s5_lam_im, s5_log_dt, bt(s5_b_re), bt(s5_b_im), s5_c_re, s5_c_im)

    cos_t, sin_t = _rope_tables(t)
    cos_c = jnp.concatenate([jnp.ones((tc, MLA_ROPE), F32), jnp.zeros((tc, LANES - MLA_ROPE), F32)], axis=-1)
    sin_c = jnp.zeros((tc, LANES), F32)
    hi = jnp.arange(HEAD_GROUP) // RWKV_HEAD
    ones_bd = (hi[:, None] == hi[None, :]).astype(BF)
    ones_bd128 = ones_bd[:LANES, :LANES]

    tm_t = min(1024, t)
    tm_c = min(1024, nb * tc)
    lat_row = lambda tm: (lambda i: i // (t // tm))
    ctx_rowf = lambda i: ctx_row

    x2 = x.reshape(nb * t, d)
    xc2 = ctx.reshape(nb * tc, d)
    s_zero = jnp.zeros((nb, 2, 1024 // HEAD_GROUP, HEAD_GROUP, HEAD_GROUP), F32)
    h_zero = jnp.zeros((nb, 1024 // S5_GROUP, 2, 2 * S5_STATE), F32)
    perm = _s5_perm()
    w_perm_all = _permute_w_in(w_in)

    for l in range(depth):
        last = l == depth - 1
        mod = mod_all[l].reshape(8, 6, 1, d)
        w_perm = w_perm_all[l]
        g1 = norm1_g[l].reshape(1, d)
        g2n = norm2_g[l].reshape(1, d)
        w_uq = jnp.pad(mla_w_uq[l].reshape(-1, N_HEADS, MLA_NOPE + MLA_ROPE),
                       ((0, 0), (0, 0), (0, 256 - MLA_NOPE - MLA_ROPE))).reshape(-1, N_HEADS * 256).astype(BF)
        w_ukv = mla_w_ukv[l].astype(BF)
        gq = mla_q_lora_g[l].reshape(1, -1)
        gkv = mla_kv_lora_g[l].reshape(1, -1)
        gqn, gqr = mla_qn_nope_g[l].reshape(1, -1), _pad_lanes(mla_qn_rope_g[l], LANES)
        gkn, gkr = mla_kn_nope_g[l].reshape(1, -1), _pad_lanes(mla_kn_rope_g[l], LANES)
        rp = dict(w0=rwkv_w0[l], w2=rwkv_w2[l].astype(BF), a0=rwkv_a0[l], a2=rwkv_a2[l].astype(BF),
                  k_a=rwkv_k_a[l].reshape(1, -1), r_k=rwkv_r_k[l].reshape(1, -1),
                  ln_g=rwkv_ln_g[l].reshape(1, -1), ln_b=rwkv_ln_b[l].reshape(1, -1),
                  g2_pad=jnp.pad(rwkv_g2[l], ((SM_GD, 512 - SM_GD - rwkv_g2.shape[1]), (0, 0))).astype(BF))
        conv_w = rwkv_conv[l]
        k_k = rwkv_k_k[l].reshape(1, -1)
        glu_w = s5_glu_w[l].astype(BF)
        glu_b = s5_glu_b[l].reshape(1, -1)
        d_skip = s5_d[l].reshape(1, -1)
        wb = w_branch[l].astype(BF)
        wo = w_out[l].astype(BF)
        w1 = w_mlp1[l].astype(BF)
        w2 = w_mlp2[l].astype(BF)

        hc = _inproj(xc2, g1, mod, w_perm, ctx_rowf, tm_c)
        k_c, v_c = _kvprep(hc, nb, tc, gkv, w_ukv, gkn, gkr, cos_c, sin_c, tc)
        rc_c, kc_c, vc_c, kk_c = _rwkv_conv(hc.reshape(nb, tc, -1), conv_w, k_k, ones_bd128)
        yf_c, yb_c, s_ctx = _rwkv_scan(rc_c, kc_c, vc_c, kk_c, hc, rp, s_zero)
        ys_c, h_ctx = _s5_scan(hc, nb, tc, nb, perm, mw, wot, a16, h_zero, l)

        ht = _inproj(x2, g1, mod, w_perm, lat_row(tm_t), tm_t)
        tq = min(512, t)
        q_t = _qprep(ht, nb, t, gq, w_uq, gqn, gqr, cos_t, sin_t, tq)
        k_t, v_t = _kvprep(ht, nb, t, gkv, w_ukv, gkn, gkr, cos_t, sin_t, tq)
        o_a = _attention(q_t, k_t, v_t, (k_c, v_c), min(256, t), min(4096, t)).reshape(nb * t, -1)

        rc_t, kc_t, vc_t, kk_t = _rwkv_conv(ht.reshape(nb, t, -1), conv_w, k_k, ones_bd128)
        yf_t, yb_t, _ = _rwkv_scan(rc_t, kc_t, vc_t, kk_t, ht, rp, s_ctx)
        flat = lambda z: z.reshape(-1, z.shape[-1])
        o_b = _rwkv_post(flat(yf_t), flat(yb_t), flat(rc_t), flat(kc_t), flat(vc_t), ht, rp, ones_bd,
                         min(512, nb * t))

        ys_t, _ = _s5_scan(ht, nb, t, 1, perm, mw, wot, a16, h_ctx, l)
        o_c = _s5_post(ys_t, ht, d_skip, glu_w, glu_b, min(512, nb * t))

        mg = _merge(o_a, o_b, o_c, ht, wb, tm_t)
        tm_m = min(512, t)
        x2 = _outproj(mg, wo, x2, mod, lat_row(tm_m), tm_m)
        x2 = _mlp(x2, g2n, mod, w1, w2, lat_row(tm_m), tm_m)

        if not last:
            q_c = _qprep(hc, nb, tc, gq, w_uq, gqn, gqr, cos_c, sin_c, tc)
            o_a_c = _attention(q_c, k_c, v_c, None, tc, tc).reshape(nb * tc, -1)
            o_b_c = _rwkv_post(flat(yf_c), flat(yb_c), flat(rc_c), flat(kc_c), flat(vc_c), hc, rp, ones_bd,
                               min(512, nb * tc))
            o_c_c = _s5_post(ys_c, hc, d_skip, glu_w, glu_b, min(512, nb * tc))
            mg_c = _merge(o_a_c, o_b_c, o_c_c, hc, wb, tm_c)
            tm_mc = min(512, nb * tc)
            xc2 = _outproj(mg_c, wo, xc2, mod, ctx_rowf, tm_mc)
            xc2 = _mlp(xc2, g2n, mod, w1, w2, ctx_rowf, tm_mc)

    return x2.reshape(nb, t, d)
```
